```python
import math
import jax
import jax.numpy as jnp
from jax import lax
import numpy as np

D_MODEL = 2048
BATCH = 2
SEQ = 4096
DEPTH = 2

GRID_W = 64
CTX_LEN = 256
NORM_EPS = 1e-6
N_MOD = 6

S5_WIDTH = 1024
S5_GROUP = 16
S5_GROUPS = S5_WIDTH // S5_GROUP
S5_STATE = 64

RET_HEADS = 8
RET_HEAD_DIM = 128
RET_WIDTH = RET_HEADS * RET_HEAD_DIM
RET_CHUNK = 128
ROPE_BASE = 10000.0

N_BRANCH = 2
IN_SPLITS = (S5_WIDTH, S5_WIDTH + RET_WIDTH, S5_WIDTH + 2 * RET_WIDTH, S5_WIDTH + 3 * RET_WIDTH, S5_WIDTH + 4 * RET_WIDTH)
IN_COLS = S5_WIDTH + 4 * RET_WIDTH + N_BRANCH * D_MODEL

N_EXPERTS = 64
TOP_K = 8
N_GROUPS = 8
TOPK_GROUPS = 4
D_EXPERT = 512
ROUTED_SCALE = 2.5
MOE_BLOCK = 128

kernel_name = 'hybrid_s5_retention_moe_dit'


def rmsnorm(x, g):
    xf = x.astype(jnp.float32)
    y = xf * lax.rsqrt(jnp.mean(xf * xf, axis=-1, keepdims=True) + NORM_EPS)
    return (y * g.astype(jnp.float32)).astype(x.dtype)


def modulate(h, shift, scale):
    return h * (1.0 + scale) + shift


def _rot_half(x, ang):
    m = x.shape[-1] // 2
    x1, x2 = x[..., :m], x[..., m:]
    cos = jnp.cos(ang)[None, :, None, :]
    sin = jnp.sin(ang)[None, :, None, :]
    return jnp.concatenate([x1 * cos - x2 * sin, x1 * sin + x2 * cos], axis=-1)


def rope_2d(x):
    n_tok = x.shape[1]
    n_rows = n_tok // GRID_W
    rows = jnp.repeat(jnp.arange(n_rows, dtype=jnp.float32), GRID_W)
    cols = jnp.tile(jnp.arange(GRID_W, dtype=jnp.float32), n_rows)
    half = x.shape[-1] // 2
    nf = half // 2
    freqs = ROPE_BASE ** (-jnp.arange(nf, dtype=jnp.float32) / nf)
    return jnp.concatenate([_rot_half(x[..., :half], rows[:, None] * freqs),
                            _rot_half(x[..., half:], cols[:, None] * freqs)], axis=-1)


def cmul(ar, ai, br, bi):
    return ar * br - ai * bi, ar * bi + ai * br


def s5_discretise(lam_re, lam_im, log_dt, b_re, b_im):
    dt = jnp.exp(log_dt)[:, None]
    lre = jnp.minimum(lam_re, -1e-4)
    mag = jnp.exp(lre * dt)
    a_re, a_im = mag * jnp.cos(lam_im * dt), mag * jnp.sin(lam_im * dt)
    nr, ni = a_re - 1.0, a_im
    den = lre * lre + lam_im * lam_im
    f_re = (nr * lre + ni * lam_im) / den
    f_im = (ni * lre - nr * lam_im) / den
    bb_re, bb_im = cmul(f_re[..., None], f_im[..., None], b_re, b_im)
    return a_re, a_im, bb_re, bb_im


def s5_scan(a_re, a_im, bu_re, bu_im, s0_re, s0_im):
    i_re, i_im = cmul(a_re, a_im, s0_re, s0_im)
    bu_re = bu_re.at[:, 0].add(i_re)
    bu_im = bu_im.at[:, 0].add(i_im)
    ar = jnp.broadcast_to(a_re, bu_re.shape)
    ai = jnp.broadcast_to(a_im, bu_re.shape)

    def combine(e1, e2):
        a1r, a1i, b1r, b1i = e1
        a2r, a2i, b2r, b2i = e2
        nar, nai = cmul(a2r, a2i, a1r, a1i)
        nbr, nbi = cmul(a2r, a2i, b1r, b1i)
        return nar, nai, nbr + b2r, nbi + b2i

    _, _, x_re, x_im = lax.associative_scan(combine, (ar, ai, bu_re, bu_im), axis=1)
    return x_re, x_im


def s5_mixer(u_ctx, u_lat, lam_re, lam_im, log_dt, b_re, b_im, c_re, c_im, d_skip, with_ctx):
    bsz = u_ctx.shape[0]
    uc = u_ctx.astype(jnp.float32).reshape(bsz, -1, S5_GROUPS, S5_GROUP)
    ul = u_lat.astype(jnp.float32).reshape(bsz, -1, S5_GROUPS, S5_GROUP)
    d = d_skip.astype(jnp.float32)
    y_c = d * uc
    y_l = d * ul
    zero_state = jnp.zeros((bsz, S5_GROUPS, S5_STATE), jnp.float32)
    br, bi = b_re.astype(jnp.float32), b_im.astype(jnp.float32)
    for direction in range(2):
        flip = (lambda t: jnp.flip(t, axis=1)) if direction == 1 else (lambda t: t)
        a_re, a_im, bb_re, bb_im = s5_discretise(lam_re[direction].astype(jnp.float32),
                                                 lam_im[direction].astype(jnp.float32),
                                                 log_dt[direction].astype(jnp.float32), br, bi)
        cr, ci = c_re[direction].astype(jnp.float32), c_im[direction].astype(jnp.float32)
        sc, sl = flip(uc), flip(ul)
        xc_re, xc_im = s5_scan(a_re, a_im,
                               jnp.einsum('gph,blgh->blgp', bb_re, sc), jnp.einsum('gph,blgh->blgp', bb_im, sc),
                               zero_state, zero_state)
        xl_re, xl_im = s5_scan(a_re, a_im,
                               jnp.einsum('gph,blgh->blgp', bb_re, sl), jnp.einsum('gph,blgh->blgp', bb_im, sl),
                               xc_re[:, -1], xc_im[:, -1])
        y_l = y_l + flip(jnp.einsum('ghp,blgp->blgh', cr, xl_re) - jnp.einsum('ghp,blgp->blgh', ci, xl_im))
        if with_ctx:
            y_c = y_c + flip(jnp.einsum('ghp,blgp->blgh', cr, xc_re) - jnp.einsum('ghp,blgp->blgh', ci, xc_im))
    y_l = y_l.reshape(bsz, -1, S5_WIDTH)
    y_c = y_c.reshape(bsz, -1, S5_WIDTH) if with_ctx else None
    return y_c, y_l


def retention_dir(q, k, v, log_g, r0, strict):
    bsz, n_tok, n_h, _ = q.shape
    dv = v.shape[-1]
    nc = n_tok // RET_CHUNK

    def chunks(t):
        return t.reshape(bsz, nc, RET_CHUNK, n_h, t.shape[-1]).transpose(1, 0, 3, 2, 4)

    pos = jnp.arange(RET_CHUNK, dtype=jnp.float32)
    diff = pos[:, None] - pos[None, :]
    keep = diff > 0 if strict else diff >= 0
    decay = jnp.where(keep[None], jnp.exp(jnp.where(keep, diff, 0.0)[None] * log_g[:, None, None]), 0.0)
    xi = jnp.exp((pos[None, :] + 1.0) * log_g[:, None])[..., None]
    zeta = jnp.exp((RET_CHUNK - 1.0 - pos[None, :]) * log_g[:, None])[..., None]
    g_chunk = jnp.exp(RET_CHUNK * log_g)[:, None, None]

    def step(r, blk):
        qb, kb, vb = blk
        scores = jnp.einsum('bhid,bhjd->bhij', qb, kb) * decay
        o = jnp.einsum('bhij,bhje->bhie', scores, vb) + jnp.einsum('bhid,bhde->bhie', qb, r) * xi
        r = g_chunk * r + jnp.einsum('bhjd,bhje->bhde', kb * zeta, vb)
        return r, o

    r, o = lax.scan(step, r0, (chunks(q), chunks(k), chunks(v)))
    return o.transpose(1, 0, 3, 2, 4).reshape(bsz, n_tok, n_h, dv), r


def retention_mixer(qc, kc, vc, ql, kl, vl, log_decay):
    log_g = jnp.minimum(log_decay.astype(jnp.float32), -1e-6)
    zero_state = jnp.zeros((qc.shape[0], RET_HEADS, RET_HEAD_DIM, RET_HEAD_DIM), jnp.float32)
    flip = lambda t: jnp.flip(t, axis=1)
    oc_f, rc_f = retention_dir(qc, kc, vc, log_g[0], zero_state, False)
    ol_f, _ = retention_dir(ql, kl, vl, log_g[0], rc_f, False)
    oc_b, rc_b = retention_dir(flip(qc), flip(kc), flip(vc), log_g[1], zero_state, True)
    ol_b, _ = retention_dir(flip(ql), flip(kl), flip(vl), log_g[1], rc_b, True)
    return oc_f + flip(oc_b), ol_f + flip(ol_b)


def merge_branches(y_s5, o_ret, g_ret, gate_logits, w_glu, ret_norm_g, w_br_s5, w_br_ret, w_out, dtype):
    z = jax.nn.gelu(y_s5).astype(dtype)
    za, zb = jnp.split(z @ w_glu, 2, axis=-1)
    s5_branch = (za * jax.nn.sigmoid(zb)) @ w_br_s5
    on = o_ret * lax.rsqrt(jnp.mean(o_ret * o_ret, axis=-1, keepdims=True) + NORM_EPS)
    on = on.reshape(o_ret.shape[0], o_ret.shape[1], RET_WIDTH) * ret_norm_g.astype(jnp.float32)
    ret_branch = (on.astype(dtype) * jax.nn.silu(g_ret)) @ w_br_ret
    gate_s5, gate_ret = jnp.split(jax.nn.sigmoid(gate_logits), 2, axis=-1)
    return (gate_s5 * s5_branch + gate_ret * ret_branch) @ w_out


def swiglu(h, wg, wu, wd):
    return (jax.nn.silu(h @ wg) * (h @ wu)) @ wd


def moe_ffn(h, w_router, b_router, w_gate, w_up, w_down, ws_gate, ws_up, ws_down):
    n, d = h.shape
    scores = jax.nn.sigmoid((h @ w_router).astype(jnp.float32))
    biased = scores + b_router.astype(jnp.float32)
    per_group = N_EXPERTS // N_GROUPS
    group_score = lax.top_k(biased.reshape(n, N_GROUPS, per_group), 2)[0].sum(-1)
    _, top_groups = lax.top_k(group_score, TOPK_GROUPS)
    group_mask = jax.nn.one_hot(top_groups, N_GROUPS, dtype=jnp.float32).sum(1) > 0
    expert_mask = jnp.repeat(group_mask, per_group, axis=1)
    _, top_idx = lax.top_k(jnp.where(expert_mask, biased, -jnp.inf), TOP_K)
    top_w = jnp.take_along_axis(scores, top_idx, axis=1)
    top_w = top_w / jnp.sum(top_w, axis=-1, keepdims=True) * ROUTED_SCALE
    m = n * TOP_K
    flat_e = top_idx.reshape(-1)
    order = jnp.argsort(flat_e)
    sorted_e = flat_e[order]
    tok = (order // TOP_K).astype(jnp.int32)
    counts = jnp.bincount(flat_e, length=N_EXPERTS)
    padded = (counts + MOE_BLOCK - 1) // MOE_BLOCK * MOE_BLOCK
    pad_end = jnp.cumsum(padded)
    pad_start = pad_end - padded
    grp_start = jnp.cumsum(counts) - counts
    dest = pad_start[sorted_e] + jnp.arange(m, dtype=jnp.int32) - grp_start[sorted_e]
    n_blocks = -(-(m + N_EXPERTS * (MOE_BLOCK - 1)) // MOE_BLOCK)
    rows = n_blocks * MOE_BLOCK
    row_tok = jnp.full((rows,), n, jnp.int32).at[dest].set(tok)
    row_w = jnp.zeros((rows,), jnp.float32).at[dest].set(top_w.reshape(-1)[order])
    h_pad = jnp.concatenate([h, jnp.zeros((1, d), h.dtype)], axis=0)
    xb = h_pad[row_tok].reshape(n_blocks, MOE_BLOCK, d)
    blk_e = jnp.minimum(jnp.searchsorted(pad_end, jnp.arange(n_blocks, dtype=jnp.int32) * MOE_BLOCK, side='right'),
                        N_EXPERTS - 1)

    def expert_block(args):
        xblk, e = args
        return swiglu(xblk, w_gate[e], w_up[e], w_down[e])

    yb = lax.map(expert_block, (xb, blk_e)).reshape(rows, d)
    routed = jax.ops.segment_sum(yb.astype(jnp.float32) * row_w[:, None], row_tok, num_segments=n + 1)[:n]
    return (routed + swiglu(h, ws_gate, ws_up, ws_down).astype(jnp.float32)).astype(h.dtype)


def trunk_layer(xl, xc, act_c, act_cc, w_mod, b_mod, norm1_g, norm2_g, w_in,
                s5_lam_re, s5_lam_im, s5_log_dt, s5_b_re, s5_b_im, s5_c_re, s5_c_im, s5_d, s5_w_glu,
                ret_log_decay, ret_norm_g, w_br_s5, w_br_ret, w_out,
                moe_router, moe_router_bias, moe_w_gate, moe_w_up, moe_w_down, sh_w_gate, sh_w_up, sh_w_down,
                with_ctx):
    bsz, n_lat, dm = xl.shape
    n_ctx = xc.shape[1]
    mod_l = jnp.split((act_c @ w_mod + b_mod)[:, None, :], N_MOD, axis=-1)
    mod_c = jnp.split((act_cc @ w_mod + b_mod)[None, None, :], N_MOD, axis=-1)

    h = jnp.concatenate([modulate(rmsnorm(xc, norm1_g), mod_c[0], mod_c[1]),
                         modulate(rmsnorm(xl, norm1_g), mod_l[0], mod_l[1])], axis=1)
    u, q, k, v, g_ret, gate_logits = jnp.split(h @ w_in, IN_SPLITS, axis=-1)
    y_c, y_l = s5_mixer(u[:, :n_ctx], u[:, n_ctx:], s5_lam_re, s5_lam_im, s5_log_dt,
                        s5_b_re, s5_b_im, s5_c_re, s5_c_im, s5_d, with_ctx)
    heads = lambda t: t.astype(jnp.float32).reshape(bsz, t.shape[1], RET_HEADS, RET_HEAD_DIM)
    qh, kh, vh = heads(q), heads(k) * RET_HEAD_DIM ** -0.5, heads(v)
    o_c, o_l = retention_mixer(qh[:, :n_ctx], kh[:, :n_ctx], vh[:, :n_ctx],
                               rope_2d(qh[:, n_ctx:]), rope_2d(kh[:, n_ctx:]), vh[:, n_ctx:], ret_log_decay)
    if with_ctx:
        y_s5, o_ret, sel = jnp.concatenate([y_c, y_l], axis=1), jnp.concatenate([o_c, o_l], axis=1), slice(None)
    else:
        y_s5, o_ret, sel = y_l, o_l, slice(n_ctx, None)
    mix = merge_branches(y_s5, o_ret, g_ret[:, sel], gate_logits[:, sel], s5_w_glu, ret_norm_g,
                         w_br_s5, w_br_ret, w_out, xl.dtype).astype(xl.dtype)
    xl = xl + mod_l[2] * mix[:, -n_lat:]
    if with_ctx:
        xc = xc + mod_c[2] * mix[:, :n_ctx]

    hl = modulate(rmsnorm(xl, norm2_g), mod_l[3], mod_l[4])
    if with_ctx:
        tokens = jnp.concatenate([modulate(rmsnorm(xc, norm2_g), mod_c[3], mod_c[4]), hl], axis=1)
    else:
        tokens = hl
    ff = moe_ffn(tokens.reshape(-1, dm), moe_router, moe_router_bias, moe_w_gate, moe_w_up, moe_w_down,
                 sh_w_gate, sh_w_up, sh_w_down).reshape(bsz, -1, dm)
    xl = xl + mod_l[5] * ff[:, -n_lat:]
    if with_ctx:
        xc = xc + mod_c[5] * ff[:, :n_ctx]
    return xl, xc


def setup_inputs(seed: int = 0) -> dict:
    key = jax.random.key(seed)
    keys = iter(jax.random.split(key, 48))
    f32 = jnp.float32

    def nrm(shape, scale):
        return jax.random.normal(next(keys), shape, f32) * scale

    d = D_MODEL
    G, H, P = S5_GROUPS, S5_GROUP, S5_STATE
    lam_im_base = jnp.pi * jnp.arange(P, dtype=f32)
    decay_base = jnp.log(1.0 - 2.0 ** (-5.0 - jnp.arange(RET_HEADS, dtype=f32)))
    return {
        'x': nrm((BATCH, SEQ, d), 1.0),
        'c': nrm((BATCH, d), 1.0),
        'ctx': nrm((BATCH, CTX_LEN, d), 1.0),
        'c_ctx': nrm((d,), 1.0),
        'w_mod': nrm((DEPTH, d, N_MOD * d), 0.5 * d ** -0.5),
        'b_mod': nrm((DEPTH, N_MOD * d), 0.02),
        'norm1_g': 1.0 + nrm((DEPTH, d), 0.02),
        'norm2_g': 1.0 + nrm((DEPTH, d), 0.02),
        'final_g': 1.0 + nrm((d,), 0.02),
        'w_in': nrm((DEPTH, d, IN_COLS), d ** -0.5),
        's5_lam_re': -0.5 + nrm((DEPTH, 2, G, P), 0.01),
        's5_lam_im': lam_im_base + nrm((DEPTH, 2, G, P), 0.01),
        's5_log_dt': jax.random.uniform(next(keys), (DEPTH, 2, G), f32, math.log(1e-3), math.log(1e-1)),
        's5_b_re': nrm((DEPTH, G, P, H), (2 * H) ** -0.5),
        's5_b_im': nrm((DEPTH, G, P, H), (2 * H) ** -0.5),
        's5_c_re': nrm((DEPTH, 2, G, H, P), P ** -0.5),
        's5_c_im': nrm((DEPTH, 2, G, H, P), P ** -0.5),
        's5_d': nrm((DEPTH, G, H), 1.0),
        's5_w_glu': nrm((DEPTH, S5_WIDTH, 2 * S5_WIDTH), S5_WIDTH ** -0.5),
        'ret_log_decay': decay_base * (1.0 + nrm((DEPTH, 2, RET_HEADS), 0.01)),
        'ret_norm_g': 1.0 + nrm((DEPTH, RET_WIDTH), 0.02),
        'w_br_s5': nrm((DEPTH, S5_WIDTH, d), S5_WIDTH ** -0.5),
        'w_br_ret': nrm((DEPTH, RET_WIDTH, d), RET_WIDTH ** -0.5),
        'w_out': nrm((DEPTH, d, d), d ** -0.5),
        'moe_router': nrm((DEPTH, d, N_EXPERTS), d ** -0.5),
        'moe_router_bias': nrm((DEPTH, N_EXPERTS), 0.01),
        'moe_w_gate': nrm((DEPTH, N_EXPERTS, d, D_EXPERT), d ** -0.5),
        'moe_w_up': nrm((DEPTH, N_EXPERTS, d, D_EXPERT), d ** -0.5),
        'moe_w_down': nrm((DEPTH, N_EXPERTS, D_EXPERT, d), D_EXPERT ** -0.5),
        'sh_w_gate': nrm((DEPTH, d, D_EXPERT), d ** -0.5),
        'sh_w_up': nrm((DEPTH, d, D_EXPERT), d ** -0.5),
        'sh_w_down': nrm((DEPTH, D_EXPERT, d), D_EXPERT ** -0.5),
    }


def reference(x, c, ctx, c_ctx, w_mod, b_mod, norm1_g, norm2_g, final_g, w_in,
              s5_lam_re, s5_lam_im, s5_log_dt, s5_b_re, s5_b_im, s5_c_re, s5_c_im, s5_d, s5_w_glu,
              ret_log_decay, ret_norm_g, w_br_s5, w_br_ret, w_out,
              moe_router, moe_router_bias, moe_w_gate, moe_w_up, moe_w_down, sh_w_gate, sh_w_up, sh_w_down):
    act_c = jax.nn.silu(c)
    act_cc = jax.nn.silu(c_ctx)
    xl, xc = x, ctx
    for i in range(DEPTH):
        xl, xc = trunk_layer(xl, xc, act_c, act_cc, w_mod[i], b_mod[i], norm1_g[i], norm2_g[i], w_in[i],
                             s5_lam_re[i], s5_lam_im[i], s5_log_dt[i], s5_b_re[i], s5_b_im[i],
                             s5_c_re[i], s5_c_im[i], s5_d[i], s5_w_glu[i],
                             ret_log_decay[i], ret_norm_g[i], w_br_s5[i], w_br_ret[i], w_out[i],
                             moe_router[i], moe_router_bias[i], moe_w_gate[i], moe_w_up[i], moe_w_down[i],
                             sh_w_gate[i], sh_w_up[i], sh_w_down[i], with_ctx=(i < DEPTH - 1))
    return rmsnorm(xl, final_g)
```

```python
import functools

import jax
import jax.numpy as jnp
from jax import lax
from jax.experimental import pallas as pl
from jax.experimental.pallas import tpu as pltpu

F32 = jnp.float32
BF16 = jnp.bfloat16

NORM_EPS = 1e-6
N_MOD = 6
GRID_W = 64
ROPE_BASE = 10000.0

S5_GROUP = 16
S5_STATE = 64
S5_CHUNK = 16
S5_TILE = S5_CHUNK * S5_GROUP
S5_LANES = 2 * S5_STATE

RET_HEADS = 8
RET_HEAD_DIM = 128
RET_CHUNK = 128
BRANCH_WIDTH = RET_HEADS * RET_HEAD_DIM

N_EXPERTS = 64
TOP_K = 8
N_GROUPS = 8
TOPK_GROUPS = 4
ROUTED_SCALE = 2.5
MOE_ROWS = 256
COMBINE_TOKENS = 128

VMEM_LIMIT = 56 * 1024 * 1024


def _cparams(*sem):
    return pltpu.CompilerParams(dimension_semantics=sem, vmem_limit_bytes=VMEM_LIMIT)


def _silu(x):
    return x * jax.nn.sigmoid(x)


def _gelu_tanh(x):
    return 0.5 * x * (1.0 + jnp.tanh(0.7978845608028654 * (x + 0.044715 * x * x * x)))


def _rms(x):
    return x * lax.rsqrt(jnp.mean(x * x, axis=-1, keepdims=True) + NORM_EPS)


def _mod_kernel(c_ref, w_ref, b_ref, o_ref):
    a = _silu(c_ref[...]).astype(BF16)
    o_ref[...] = jnp.dot(a, w_ref[...].astype(BF16), preferred_element_type=F32) + b_ref[...]


def _modulation(cin, w_mod, b_mod):
    depth, d, n = w_mod.shape
    tn = 1024
    return pl.pallas_call(
        _mod_kernel,
        grid=(depth, n // tn),
        in_specs=[pl.BlockSpec((8, d), lambda l, j: (0, 0)),
                  pl.BlockSpec((None, d, tn), lambda l, j: (l, 0, j)),
                  pl.BlockSpec((None, 1, tn), lambda l, j: (l, 0, j))],
        out_specs=pl.BlockSpec((None, 8, tn), lambda l, j: (l, 0, j)),
        out_shape=jax.ShapeDtypeStruct((depth, 8, n), F32),
        compiler_params=_cparams("arbitrary", "arbitrary"),
    )(cin, w_mod, b_mod.reshape(depth, 1, n))


def _in_proj_kernel(x_ref, g_ref, ml_ref, mc_ref, w_ref, o_ref, h_scr, *, n_ctx, tm):
    i = pl.program_id(1)
    j = pl.program_id(2)

    @pl.when(j == 0)
    def _():
        y = _rms(x_ref[...]) * g_ref[...]
        row = i * tm + lax.broadcasted_iota(jnp.int32, (tm, 1), 0)
        is_ctx = row < n_ctx
        shift = jnp.where(is_ctx, mc_ref[0:1, :], ml_ref[0:1, :])
        scale = jnp.where(is_ctx, mc_ref[1:2, :], ml_ref[1:2, :])
        h_scr[...] = (y * (1.0 + scale) + shift).astype(BF16)

    o_ref[...] = jnp.dot(h_scr[...], w_ref[...], preferred_element_type=F32)


def _in_proj(xs, norm_g, mods, w_in_bf16, n_ctx, tm):
    bsz, t, d = xs.shape
    n = w_in_bf16.shape[1]
    tn = 1024
    return pl.pallas_call(
        functools.partial(_in_proj_kernel, n_ctx=n_ctx, tm=tm),
        grid=(bsz, t // tm, n // tn),
        in_specs=[pl.BlockSpec((None, tm, d), lambda b, i, j: (b, i, 0)),
                  pl.BlockSpec((1, d), lambda b, i, j: (0, 0)),
                  pl.BlockSpec((None, N_MOD, d), lambda b, i, j: (b, 0, 0)),
                  pl.BlockSpec((None, N_MOD, d), lambda b, i, j: (bsz, 0, 0)),
                  pl.BlockSpec((d, tn), lambda b, i, j: (0, j))],
        out_specs=pl.BlockSpec((None, tm, tn), lambda b, i, j: (b, i, j)),
        out_shape=jax.ShapeDtypeStruct((bsz, t, n), F32),
        scratch_shapes=[pltpu.VMEM((tm, d), BF16)],
        compiler_params=_cparams("arbitrary", "arbitrary", "arbitrary"),
    )(xs, norm_g.reshape(1, d), mods, mods, w_in_bf16)


def _s5_tables(lam_re, lam_im, log_dt, b_re, b_im, c_re, c_im, d_skip):
    hp = lax.Precision.HIGHEST
    g_n, p_n, h_n = b_re.shape
    c_n = S5_CHUNK
    dt = jnp.exp(log_dt)[..., None]
    lre = jnp.minimum(lam_re, -1e-4)
    steps = jnp.arange(c_n + 1, dtype=F32)[:, None, None, None]
    mag = jnp.exp(steps * (lre * dt))
    ang = steps * (lam_im * dt)
    pr, pi = mag * jnp.cos(ang), mag * jnp.sin(ang)
    a_re, a_im = pr[1], pi[1]
    den = lre * lre + lam_im * lam_im
    nr, ni = a_re - 1.0, a_im
    f_re = (nr * lre + ni * lam_im) / den
    f_im = (ni * lre - nr * lam_im) / den
    bb_re = f_re[..., None] * b_re - f_im[..., None] * b_im
    bb_im = f_re[..., None] * b_im + f_im[..., None] * b_re
    w_re = pr[..., None] * bb_re - pi[..., None] * bb_im
    w_im = pr[..., None] * bb_im + pi[..., None] * bb_re
    kl = (jnp.einsum('dghp,ndgpk->ndgkh', c_re, w_re[:c_n], precision=hp)
          - jnp.einsum('dghp,ndgpk->ndgkh', c_im, w_im[:c_n], precision=hp))
    s_i = jnp.arange(c_n)[:, None]
    t_i = jnp.arange(c_n)[None, :]
    kf = jnp.where((t_i >= s_i)[..., None, None, None], kl[:, 0][jnp.clip(t_i - s_i, 0, c_n - 1)], 0.0)
    kb = jnp.where((s_i >= t_i)[..., None, None, None], kl[:, 1][jnp.clip(s_i - t_i, 0, c_n - 1)], 0.0)
    skip = (jnp.eye(c_n, dtype=F32)[:, :, None, None, None]
            * jnp.eye(h_n, dtype=F32)[None, None, None] * d_skip[None, None, :, None, :])
    ktoep = (kf + kb + skip).transpose(2, 0, 3, 1, 4).reshape(g_n, S5_TILE, S5_TILE)

    def st(w, idx, d):
        return w[idx, d].transpose(1, 0, 3, 2).reshape(g_n, S5_TILE, p_n)

    fwd_idx = c_n - 1 - jnp.arange(c_n)
    bwd_idx = jnp.arange(c_n)
    wst = jnp.concatenate([st(w_re, fwd_idx, 0), st(w_re, bwd_idx, 1),
                           st(w_im, fwd_idx, 0), st(w_im, bwd_idx, 1)], axis=-1)

    def out_rows(d, idx):
        cr = c_re[d][None]
        ci = c_im[d][None]
        er = pr[idx, d][:, :, None, :]
        ei = pi[idx, d][:, :, None, :]
        re = (cr * er - ci * ei).transpose(1, 3, 0, 2).reshape(g_n, p_n, S5_TILE)
        im = (cr * ei + ci * er).transpose(1, 3, 0, 2).reshape(g_n, p_n, S5_TILE)
        return re, im

    fo_re, fo_im = out_rows(0, jnp.arange(c_n) + 1)
    bo_re, bo_im = out_rows(1, c_n - jnp.arange(c_n))
    wout = jnp.concatenate([fo_re, bo_re, -fo_im, -bo_im], axis=1)
    w1 = jnp.concatenate([ktoep, wst], axis=-1).astype(BF16)
    ar = jnp.concatenate([pr[c_n, 0], pr[c_n, 1]], axis=-1)
    ai = jnp.concatenate([pi[c_n, 0], pi[c_n, 1]], axis=-1)
    return w1, wout.astype(BF16), ar, ai


def _s5_kernel(u_ref, w1_ref, wo_ref, ar_ref, ai_ref, y_ref, sre, sim, xfr, xfi, xbr, xbi, *, gb, nc, ncc):
    ln = S5_LANES
    for g in range(gb):
        r = jnp.dot(u_ref[g], w1_ref[g], preferred_element_type=F32)
        y_ref[g] = r[:, :S5_TILE]
        sre[:, g * ln:(g + 1) * ln] = r[:, S5_TILE:S5_TILE + ln]
        sim[:, g * ln:(g + 1) * ln] = r[:, S5_TILE + ln:]
    are = ar_ref[...]
    aim = ai_ref[...]
    is_f = (lax.broadcasted_iota(jnp.int32, (1, gb * ln), 1) % ln) < S5_STATE

    def step(j, carry):
        xre, xim = carry
        cf = j
        cb = jnp.where(j < ncc, ncc - 1 - j, nc - 1 - (j - ncc))
        s_re = jnp.where(is_f, sre[pl.ds(cf, 1), :], sre[pl.ds(cb, 1), :])
        s_im = jnp.where(is_f, sim[pl.ds(cf, 1), :], sim[pl.ds(cb, 1), :])
        xfr[pl.ds(cf, 1), :] = xre
        xfi[pl.ds(cf, 1), :] = xim
        xbr[pl.ds(cb, 1), :] = xre
        xbi[pl.ds(cb, 1), :] = xim
        return are * xre - aim * xim + s_re, are * xim + aim * xre + s_im

    zero = jnp.zeros((1, gb * ln), F32)
    lax.fori_loop(0, nc, step, (zero, zero))

    is_f1 = lax.broadcasted_iota(jnp.int32, (1, ln), 1) < S5_STATE
    for g in range(gb):
        sl = slice(g * ln, (g + 1) * ln)
        xp = jnp.concatenate([jnp.where(is_f1, xfr[:, sl], xbr[:, sl]),
                              jnp.where(is_f1, xfi[:, sl], xbi[:, sl])], axis=1).astype(BF16)
        y_ref[g] += jnp.dot(xp, wo_ref[g], preferred_element_type=F32)


def _s5_mix(u_g, w1, wout, ar, ai, n_ctx, gb):
    bsz, g_n, nc, _ = u_g.shape
    ncc = n_ctx // S5_CHUNK
    ln = S5_LANES
    scr = pltpu.VMEM((nc, gb * ln), F32)
    return pl.pallas_call(
        functools.partial(_s5_kernel, gb=gb, nc=nc, ncc=ncc),
        grid=(bsz, g_n // gb),
        in_specs=[pl.BlockSpec((None, gb, nc, S5_TILE), lambda b, j: (b, j, 0, 0)),
                  pl.BlockSpec((gb, S5_TILE, 2 * S5_TILE), lambda b, j: (j, 0, 0)),
                  pl.BlockSpec((gb, S5_TILE, S5_TILE), lambda b, j: (j, 0, 0)),
                  pl.BlockSpec((None, 1, gb * ln), lambda b, j: (j, 0, 0)),
                  pl.BlockSpec((None, 1, gb * ln), lambda b, j: (j, 0, 0))],
        out_specs=pl.BlockSpec((None, gb, nc, S5_TILE), lambda b, j: (b, j, 0, 0)),
        out_shape=jax.ShapeDtypeStruct((bsz, g_n, nc, S5_TILE), F32),
        scratch_shapes=[scr, scr, scr, scr, scr, scr],
        compiler_params=_cparams("arbitrary", "arbitrary"),
    )(u_g, w1, wout, ar.reshape(g_n // gb, 1, gb * ln), ai.reshape(g_n // gb, 1, gb * ln))


def _rope_tables(n_ctx, n_lat):
    pos = jnp.arange(n_lat)
    rows = (pos // GRID_W).astype(F32)
    cols = (pos % GRID_W).astype(F32)
    nf = RET_HEAD_DIM // 4
    freqs = ROPE_BASE ** (-jnp.arange(nf, dtype=F32) / nf)
    ar, ac = rows[:, None] * freqs, cols[:, None] * freqs
    cos = jnp.concatenate([jnp.cos(ar), jnp.cos(ar), jnp.cos(ac), jnp.cos(ac)], axis=-1)
    sin = jnp.concatenate([-jnp.sin(ar), jnp.sin(ar), -jnp.sin(ac), jnp.sin(ac)], axis=-1)
    cos = jnp.concatenate([jnp.ones((n_ctx, RET_HEAD_DIM), F32), cos], axis=0)
    sin = jnp.concatenate([jnp.zeros((n_ctx, RET_HEAD_DIM), F32), sin], axis=0)
    return cos, sin


def _ret_kernel(q_ref, k_ref, v_ref, cos_ref, sin_ref, lg_ref, o_ref, qs, ks, vs, *, nchunk, ncc):
    c_n = RET_CHUNK
    nf = RET_HEAD_DIM // 4
    lane = lax.broadcasted_iota(jnp.int32, (1, RET_HEAD_DIM), 1)
    first = (lane % (2 * nf)) < nf

    def rope(x):
        partner = jnp.where(first, pltpu.roll(x, RET_HEAD_DIM - nf, 1), pltpu.roll(x, nf, 1))
        return x * cos_ref[...] + partner * sin_ref[...]

    qs[...] = rope(q_ref[...]).astype(BF16)
    ks[...] = rope(k_ref[...]) * (RET_HEAD_DIM ** -0.5)
    vs[...] = v_ref[...].astype(BF16)

    lg = jnp.minimum(lg_ref[...], -1e-6)
    lgf = lg[0:1, :]
    lgb = lg[1:2, :]
    ri = lax.broadcasted_iota(jnp.int32, (c_n, c_n), 0)
    ci = lax.broadcasted_iota(jnp.int32, (c_n, c_n), 1)
    diff = (ri - ci).astype(F32)
    low = diff >= 0.0
    decay = jnp.where(low, jnp.exp(jnp.where(low, diff, 0.0) * lgf), jnp.exp(jnp.where(low, 0.0, -diff) * lgb))
    pos = ri.astype(F32)
    xi_f = jnp.exp((pos + 1.0) * lgf)
    zeta_f = jnp.exp((c_n - 1.0 - pos) * lgf)
    gch_f = jnp.exp(c_n * lgf)
    xi_b = jnp.exp((c_n - pos) * lgb)
    zeta_b = jnp.exp(pos * lgb)
    gch_b = jnp.exp(c_n * lgb)
    nt = (((1,), (1,)), ((), ()))

    def fwd(c, r):
        sl = pl.ds(pl.multiple_of(c * c_n, c_n), c_n)
        qc, kf, vc = qs[sl, :], ks[sl, :], vs[sl, :]
        s = lax.dot_general(qc, kf.astype(BF16), nt, preferred_element_type=F32)
        o = jnp.dot((s * decay).astype(BF16), vc, preferred_element_type=F32)
        o = o + jnp.dot(qc, r.astype(BF16), preferred_element_type=F32) * xi_f
        o_ref[sl, :] = o
        kz = (kf * zeta_f).T.astype(BF16)
        return gch_f * r + jnp.dot(kz, vc, preferred_element_type=F32)

    zero = jnp.zeros((c_n, c_n), F32)
    lax.fori_loop(0, nchunk, fwd, zero)

    def bwd(j, r):
        c = jnp.where(j < ncc, ncc - 1 - j, nchunk - 1 - (j - ncc))
        sl = pl.ds(pl.multiple_of(c * c_n, c_n), c_n)
        qc, kf, vc = qs[sl, :], ks[sl, :], vs[sl, :]
        o_ref[sl, :] += jnp.dot(qc, r.astype(BF16), preferred_element_type=F32) * xi_b
        kz = (kf * zeta_b).T.astype(BF16)
        return gch_b * r + jnp.dot(kz, vc, preferred_element_type=F32)

    lax.fori_loop(0, nchunk, bwd, zero)


def _retention(proj, cos, sin, lg_tab, n_ctx):
    bsz, t, _ = proj.shape
    hd = RET_HEAD_DIM
    nchunk = t // RET_CHUNK
    ncc = n_ctx // RET_CHUNK
    q0 = BRANCH_WIDTH // hd
    return pl.pallas_call(
        functools.partial(_ret_kernel, nchunk=nchunk, ncc=ncc),
        grid=(bsz, RET_HEADS),
        in_specs=[pl.BlockSpec((None, t, hd), lambda b, h: (b, 0, q0 + h)),
                  pl.BlockSpec((None, t, hd), lambda b, h: (b, 0, 2 * q0 + h)),
                  pl.BlockSpec((None, t, hd), lambda b, h: (b, 0, 3 * q0 + h)),
                  pl.BlockSpec((t, hd), lambda b, h: (0, 0)),
                  pl.BlockSpec((t, hd), lambda b, h: (0, 0)),
                  pl.BlockSpec((None, 8, hd), lambda b, h: (h, 0, 0))],
        out_specs=pl.BlockSpec((None, t, hd), lambda b, h: (b, 0, h)),
        out_shape=jax.ShapeDtypeStruct((bsz, t, BRANCH_WIDTH), F32),
        scratch_shapes=[pltpu.VMEM((t, hd), BF16), pltpu.VMEM((t, hd), F32), pltpu.VMEM((t, hd), BF16)],
        compiler_params=_cparams("arbitrary", "arbitrary"),
    )(proj, proj, proj, cos, sin, lg_tab)


def _merge_kernel(*refs, ngb):
    ys_ref, or_ref, gr_ref = refs[0:3]
    gs_refs = refs[3:3 + ngb]
    gt_refs = refs[3 + ngb:3 + 2 * ngb]
    (x_ref, mod_ref, rg_ref, n2_ref, wglu_ref, wbs_ref, wbr_ref, wout_ref, wr_ref,
     xo_ref, h2_ref, lg_ref) = refs[3 + 2 * ngb:]
    w = BRANCH_WIDTH
    z = _gelu_tanh(ys_ref[...]).astype(BF16)
    zz = jnp.dot(z, wglu_ref[...], preferred_element_type=F32)
    s5b = jnp.dot((zz[:, :w] * jax.nn.sigmoid(zz[:, w:])).astype(BF16), wbs_ref[...], preferred_element_type=F32)
    o = or_ref[...]
    on = jnp.concatenate([_rms(o[:, h * RET_HEAD_DIM:(h + 1) * RET_HEAD_DIM]) for h in range(RET_HEADS)], axis=1)
    on = on * rg_ref[...]
    rb = jnp.dot((on * _silu(gr_ref[...])).astype(BF16), wbr_ref[...], preferred_element_type=F32)
    gate_s = jax.nn.sigmoid(jnp.concatenate([r[...] for r in gs_refs], axis=1))
    gate_r = jax.nn.sigmoid(jnp.concatenate([r[...] for r in gt_refs], axis=1))
    mix = jnp.dot((gate_s * s5b + gate_r * rb).astype(BF16), wout_ref[...], preferred_element_type=F32)
    xn = x_ref[...] + mod_ref[2:3, :] * mix
    xo_ref[...] = xn
    h2 = (_rms(xn) * n2_ref[...]) * (1.0 + mod_ref[4:5, :]) + mod_ref[3:4, :]
    h2_ref[...] = h2
    lg_ref[...] = jnp.dot(h2, wr_ref[...], preferred_element_type=F32, precision=lax.Precision.HIGHEST)


def _merge(y_s5, o_ret, proj, xs, mods, ret_norm_g, norm2_g, wglu, wbs, wbr, wout, w_router, n_ctx, tm):
    bsz, t, d = xs.shape
    w = BRANCH_WIDTH
    gw = 1024
    ngb = d // gw
    gs0 = 5 * w // gw
    ctx_tiles = n_ctx // tm
    row = lambda b, i: (b, i, 0)
    const = lambda b, i: (0, 0)
    one = pl.Buffered(1)

    def gate_spec(k):
        return pl.BlockSpec((None, tm, gw), lambda b, i: (b, i, gs0 + k))

    in_specs = ([pl.BlockSpec((None, tm, w), row), pl.BlockSpec((None, tm, w), row),
                 pl.BlockSpec((None, tm, w), lambda b, i: (b, i, 4))]
                + [gate_spec(k) for k in range(ngb)] + [gate_spec(ngb + k) for k in range(ngb)]
                + [pl.BlockSpec((None, tm, d), row),
                   pl.BlockSpec((None, N_MOD, d), lambda b, i: (jnp.where(i < ctx_tiles, bsz, b), 0, 0)),
                   pl.BlockSpec((1, w), const), pl.BlockSpec((1, d), const),
                   pl.BlockSpec((w, 2 * w), const, pipeline_mode=one),
                   pl.BlockSpec((w, d), const, pipeline_mode=one),
                   pl.BlockSpec((w, d), const, pipeline_mode=one),
                   pl.BlockSpec((d, d), const, pipeline_mode=one),
                   pl.BlockSpec((d, N_EXPERTS), const, pipeline_mode=one)])
    return pl.pallas_call(
        functools.partial(_merge_kernel, ngb=ngb),
        grid=(bsz, t // tm),
        in_specs=in_specs,
        out_specs=[pl.BlockSpec((None, tm, d), row), pl.BlockSpec((None, tm, d), row),
                   pl.BlockSpec((None, tm, N_EXPERTS), row)],
        out_shape=[jax.ShapeDtypeStruct((bsz, t, d), F32), jax.ShapeDtypeStruct((bsz, t, d), F32),
                   jax.ShapeDtypeStruct((bsz, t, N_EXPERTS), F32)],
        compiler_params=_cparams("arbitrary", "arbitrary"),
    )(y_s5, o_ret, proj, *([proj] * (2 * ngb)), xs, mods, ret_norm_g.reshape(1, w), norm2_g.reshape(1, d),
      wglu, wbs, wbr, wout, w_router)


def _route(logits, b_router):
    n = logits.shape[0]
    scores = jax.nn.sigmoid(logits)
    biased = scores + b_router
    per_group = N_EXPERTS // N_GROUPS
    group_score = lax.top_k(biased.reshape(n, N_GROUPS, per_group), 2)[0].sum(-1)
    _, top_groups = lax.top_k(group_score, TOPK_GROUPS)
    group_mask = jax.nn.one_hot(top_groups, N_GROUPS, dtype=F32).sum(1) > 0
    expert_mask = jnp.repeat(group_mask, per_group, axis=1)
    _, top_idx = lax.top_k(jnp.where(expert_mask, biased, -jnp.inf), TOP_K)
    top_w = jnp.take_along_axis(scores, top_idx, axis=1)
    top_w = top_w / jnp.sum(top_w, axis=-1, keepdims=True) * ROUTED_SCALE
    return top_idx, top_w


def _dispatch_plan(top_idx, n_blocks):
    n = top_idx.shape[0]
    m = n * TOP_K
    flat_e = top_idx.reshape(-1)
    order = jnp.argsort(flat_e)
    sorted_e = flat_e[order]
    tok = (order // TOP_K).astype(jnp.int32)
    counts = jnp.bincount(flat_e, length=N_EXPERTS)
    padded = (counts + MOE_ROWS - 1) // MOE_ROWS * MOE_ROWS
    pad_end = jnp.cumsum(padded)
    pad_start = pad_end - padded
    grp_start = jnp.cumsum(counts) - counts
    dest = (pad_start[sorted_e] + jnp.arange(m, dtype=jnp.int32) - grp_start[sorted_e]).astype(jnp.int32)
    row_tok = jnp.zeros((n_blocks * MOE_ROWS,), jnp.int32).at[dest].set(tok)
    pos = jnp.zeros((m,), jnp.int32).at[order].set(dest)
    blk_e = jnp.minimum(jnp.searchsorted(pad_end, jnp.arange(n_blocks, dtype=jnp.int32) * MOE_ROWS, side='right'),
                        N_EXPERTS - 1).astype(jnp.int32)
    n_used = (pad_end[-1] // MOE_ROWS).astype(jnp.int32).reshape(1)
    return row_tok, pos, blk_e, n_used


def _expert_kernel(blk_e_ref, nused_ref, tok_ref, h_hbm, wg_ref, wu_ref, wd_ref, y_ref,
                   xbuf, sem, wgb, wub, wdb):
    bm = MOE_ROWS
    i = pl.program_id(0)
    n_used = nused_ref[0]
    slot = i % 2

    def row_copy(tok, slot_, r):
        return pltpu.make_async_copy(h_hbm.at[pl.ds(tok, 1), :], xbuf.at[slot_, pl.ds(r, 1), :], sem.at[slot_])

    def issue(blk, slot_):
        def body(r, carry):
            row_copy(tok_ref[blk * bm + r], slot_, r).start()
            return carry
        lax.fori_loop(0, bm, body, 0, unroll=8)

    @pl.when(jnp.logical_and(i == 0, n_used > 0))
    def _():
        issue(0, 0)

    @pl.when(i + 1 < n_used)
    def _():
        issue(i + 1, 1 - slot)

    @pl.when(i < n_used)
    def _():
        def wait_body(r, carry):
            row_copy(0, slot, r).wait()
            return carry
        lax.fori_loop(0, bm, wait_body, 0, unroll=8)

        new_expert = jnp.logical_or(i == 0, blk_e_ref[i] != blk_e_ref[jnp.maximum(i - 1, 0)])

        @pl.when(new_expert)
        def _():
            wgb[...] = wg_ref[...].astype(BF16)
            wub[...] = wu_ref[...].astype(BF16)
            wdb[...] = wd_ref[...].astype(BF16)

        x = xbuf[slot].astype(BF16)
        g = jnp.dot(x, wgb[...], preferred_element_type=F32)
        u = jnp.dot(x, wub[...], preferred_element_type=F32)
        y_ref[...] = jnp.dot((_silu(g) * u).astype(BF16), wdb[...], preferred_element_type=F32)

    @pl.when(i >= n_used)
    def _():
        y_ref[...] = jnp.zeros_like(y_ref)


def _experts(h2, row_tok, blk_e, n_used, w_gate, w_up, w_down, n_blocks):
    n, d = h2.shape
    de = w_gate.shape[-1]
    bm = MOE_ROWS
    grid_spec = pltpu.PrefetchScalarGridSpec(
        num_scalar_prefetch=3,
        grid=(n_blocks,),
        in_specs=[pl.BlockSpec(memory_space=pl.ANY),
                  pl.BlockSpec((None, d, de), lambda i, be, nu, tk: (be[i], 0, 0)),
                  pl.BlockSpec((None, d, de), lambda i, be, nu, tk: (be[i], 0, 0)),
                  pl.BlockSpec((None, de, d), lambda i, be, nu, tk: (be[i], 0, 0))],
        out_specs=pl.BlockSpec((bm, d), lambda i, be, nu, tk: (i, 0)),
        scratch_shapes=[pltpu.VMEM((2, bm, d), F32), pltpu.SemaphoreType.DMA((2,)),
                        pltpu.VMEM((d, de), BF16), pltpu.VMEM((d, de), BF16), pltpu.VMEM((de, d), BF16)])
    return pl.pallas_call(
        _expert_kernel,
        grid_spec=grid_spec,
        out_shape=jax.ShapeDtypeStruct((n_blocks * bm, d), F32),
        compiler_params=_cparams("arbitrary"),
    )(blk_e, n_used, row_tok, h2, w_gate, w_up, w_down)


def _combine_kernel(pos_ref, ys_hbm, tw_ref, h2_ref, x_ref, mod_ref, wsg_ref, wsu_ref, wsd_ref, fg_ref,
                    o_ref, buf, sem, *, tmc, final):
    i = pl.program_id(0)
    n_tiles = pl.num_programs(0)
    slot = i % 2

    def row_copy(p, slot_, k, t):
        return pltpu.make_async_copy(ys_hbm.at[pl.ds(p, 1), :], buf.at[slot_, k, pl.ds(t, 1), :], sem.at[slot_])

    def issue(tile, slot_):
        def body(t, carry):
            base = (tile * tmc + t) * TOP_K
            for k in range(TOP_K):
                row_copy(pos_ref[base + k], slot_, k, t).start()
            return carry
        lax.fori_loop(0, tmc, body, 0)

    @pl.when(i == 0)
    def _():
        issue(0, 0)

    @pl.when(i + 1 < n_tiles)
    def _():
        issue(i + 1, 1 - slot)

    h = h2_ref[...].astype(BF16)
    sg = jnp.dot(h, wsg_ref[...], preferred_element_type=F32)
    su = jnp.dot(h, wsu_ref[...], preferred_element_type=F32)
    ff = jnp.dot((_silu(sg) * su).astype(BF16), wsd_ref[...], preferred_element_type=F32)

    def wait_body(t, carry):
        for k in range(TOP_K):
            row_copy(0, slot, k, t).wait()
        return carry
    lax.fori_loop(0, tmc, wait_body, 0)

    tw = tw_ref[...]
    routed = tw[:, 0:1] * buf[slot, 0]
    for k in range(1, TOP_K):
        routed = routed + tw[:, k:k + 1] * buf[slot, k]
    xn = x_ref[...] + mod_ref[5:6, :] * (routed + ff)
    if final:
        xn = _rms(xn) * fg_ref[...]
    o_ref[...] = xn


def _combine(ys, pos, top_w, h2, xs, mods, wsg, wsu, wsd, final_g, n_ctx, t_per_batch, final):
    n, d = h2.shape
    de = wsg.shape[-1]
    tmc = COMBINE_TOKENS
    bsz = n // t_per_batch
    tiles_per_batch = t_per_batch // tmc
    ctx_tiles = n_ctx // tmc
    const = lambda i, p: (0, 0)
    row = lambda i, p: (i, 0)

    def mod_map(i, p):
        return (jnp.where(i % tiles_per_batch < ctx_tiles, bsz, i // tiles_per_batch), 0, 0)

    grid_spec = pltpu.PrefetchScalarGridSpec(
        num_scalar_prefetch=1,
        grid=(n // tmc,),
        in_specs=[pl.BlockSpec(memory_space=pl.ANY),
                  pl.BlockSpec((tmc, TOP_K), row),
                  pl.BlockSpec((tmc, d), row),
                  pl.BlockSpec((tmc, d), row),
                  pl.BlockSpec((None, N_MOD, d), mod_map),
                  pl.BlockSpec((d, de), const), pl.BlockSpec((d, de), const), pl.BlockSpec((de, d), const),
                  pl.BlockSpec((1, d), const)],
        out_specs=pl.BlockSpec((tmc, d), row),
        scratch_shapes=[pltpu.VMEM((2, TOP_K, tmc, d), F32), pltpu.SemaphoreType.DMA((2,))])
    return pl.pallas_call(
        functools.partial(_combine_kernel, tmc=tmc, final=final),
        grid_spec=grid_spec,
        out_shape=jax.ShapeDtypeStruct((n, d), F32),
        compiler_params=_cparams("arbitrary"),
    )(pos, ys, top_w, h2, xs.reshape(n, d), mods, wsg, wsu, wsd, final_g.reshape(1, d))


def _pick_tile(n, candidates):
    for c in candidates:
        if n % c == 0:
            return c
    raise ValueError(f"no tile for {n}")


def kernel(x, c, ctx, c_ctx, w_mod, b_mod, norm1_g, norm2_g, final_g, w_in, s5_lam_re, s5_lam_im, s5_log_dt, s5_b_re, s5_b_im, s5_c_re, s5_c_im, s5_d, s5_w_glu, ret_log_decay, ret_norm_g, w_br_s5, w_br_ret, w_out, moe_router, moe_router_bias, moe_w_gate, moe_w_up, moe_w_down, sh_w_gate, sh_w_up, sh_w_down):
    bsz, n_lat, d = x.shape
    n_ctx = ctx.shape[1]
    depth = w_mod.shape[0]
    t = n_ctx + n_lat
    n_tok = bsz * t
    g_n = s5_b_re.shape[1]
    assert n_ctx % RET_CHUNK == 0 and n_lat % RET_CHUNK == 0 and bsz + 1 <= 8
    assert d % 1024 == 0 and n_tok % COMBINE_TOKENS == 0 and n_ctx % COMBINE_TOKENS == 0

    cin = jnp.zeros((8, d), F32).at[:bsz].set(c).at[bsz].set(c_ctx)
    mods_all = _modulation(cin, w_mod, b_mod).reshape(depth, 8, N_MOD, d)
    cos, sin = _rope_tables(n_ctx, n_lat)
    xs = jnp.concatenate([ctx, x], axis=1)

    tm_in = _pick_tile(t, (544, 272, 256, 128))
    tm_merge = _pick_tile(n_ctx, (256, 128))
    gb = 16
    nc = t // S5_CHUNK
    n_blocks = -(-(n_tok * TOP_K + N_EXPERTS * (MOE_ROWS - 1)) // MOE_ROWS)

    for l in range(depth):
        mods = mods_all[l]
        proj = _in_proj(xs, norm1_g[l], mods, w_in[l].astype(BF16), n_ctx, tm_in)

        w1, wo, ar, ai = _s5_tables(s5_lam_re[l], s5_lam_im[l], s5_log_dt[l], s5_b_re[l], s5_b_im[l],
                                    s5_c_re[l], s5_c_im[l], s5_d[l])
        u_g = (proj[:, :, :BRANCH_WIDTH].astype(BF16)
               .reshape(bsz, nc, S5_CHUNK, g_n, S5_GROUP).transpose(0, 3, 1, 2, 4).reshape(bsz, g_n, nc, S5_TILE))
        y_g = _s5_mix(u_g, w1, wo, ar, ai, n_ctx, gb)
        y_s5 = (y_g.reshape(bsz, g_n, nc, S5_CHUNK, S5_GROUP).transpose(0, 2, 3, 1, 4)
                .reshape(bsz, t, BRANCH_WIDTH))

        lg_tab = jnp.zeros((RET_HEADS, 8, RET_HEAD_DIM), F32).at[:, 0:2, :].set(
            jnp.broadcast_to(ret_log_decay[l].T[:, :, None], (RET_HEADS, 2, RET_HEAD_DIM)))
        o_ret = _retention(proj, cos, sin, lg_tab, n_ctx)

        xs, h2, logits = _merge(y_s5, o_ret, proj, xs, mods, ret_norm_g[l], norm2_g[l],
                                s5_w_glu[l].astype(BF16), w_br_s5[l].astype(BF16), w_br_ret[l].astype(BF16),
                                w_out[l].astype(BF16), moe_router[l], n_ctx, tm_merge)

        h2f = h2.reshape(n_tok, d)
        top_idx, top_w = _route(logits.reshape(n_tok, N_EXPERTS), moe_router_bias[l])
        row_tok, pos, blk_e, n_used = _dispatch_plan(top_idx, n_blocks)
        ys = _experts(h2f, row_tok, blk_e, n_used, moe_w_gate[l], moe_w_up[l], moe_w_down[l], n_blocks)
        out = _combine(ys, pos, top_w, h2f, xs, mods, sh_w_gate[l].astype(BF16), sh_w_up[l].astype(BF16),
                       sh_w_down[l].astype(BF16), final_g, n_ctx, t, final=(l == depth - 1))
        xs = out.reshape(bsz, t, d)

    return xs[:, n_ctx:]
```

```python
import functools

import jax
import jax.numpy as jnp
from jax import lax
from jax.experimental import pallas as pl
from jax.experimental.pallas import tpu as pltpu

F32 = jnp.float32
BF16 = jnp.bfloat16

NORM_EPS = 1e-6
N_MOD = 6
GRID_W = 64
ROPE_BASE = 10000.0

S5_GROUP = 16
S5_STATE = 64
S5_CHUNK = 16
S5_TILE = S5_CHUNK * S5_GROUP
S5_LANES = 2 * S5_STATE

RET_HEADS = 8
RET_HEAD_DIM = 128
RET_CHUNK = 128
BRANCH_WIDTH = RET_HEADS * RET_HEAD_DIM

N_EXPERTS = 64
TOP_K = 8
N_GROUPS = 8
TOPK_GROUPS = 4
ROUTED_SCALE = 2.5
MOE_ROWS = 256
COMBINE_TOKENS = 128

VMEM_LIMIT = 56 * 1024 * 1024


def _cparams(*sem):
    return pltpu.CompilerParams(dimension_semantics=sem, vmem_limit_bytes=VMEM_LIMIT)


def _silu(x):
    return x * jax.nn.sigmoid(x)


def _gelu_tanh(x):
    return 0.5 * x * (1.0 + jnp.tanh(0.7978845608028654 * (x + 0.044715 * x * x * x)))


def _rms(x):
    return x * lax.rsqrt(jnp.mean(x * x, axis=-1, keepdims=True) + NORM_EPS)


def _store_token_rows(ref, val):
    rows, d = val.shape
    nt = d // 128
    for j in range(nt):
        ref[pl.ds(j, rows, stride=nt), :] = val[:, j * 128:(j + 1) * 128]


def _load_token_rows(ref, rows, d):
    nt = d // 128
    return jnp.concatenate([ref[pl.ds(j, rows, stride=nt), :] for j in range(nt)], axis=1)


def _mod_kernel(c_ref, w_ref, b_ref, o_ref):
    a = _silu(c_ref[...]).astype(BF16)
    o_ref[...] = jnp.dot(a, w_ref[...].astype(BF16), preferred_element_type=F32) + b_ref[...]


def _modulation(cin, w_mod, b_mod):
    depth, d, n = w_mod.shape
    tn = 1024
    return pl.pallas_call(
        _mod_kernel,
        grid=(depth, n // tn),
        in_specs=[pl.BlockSpec((8, d), lambda l, j: (0, 0)),
                  pl.BlockSpec((None, d, tn), lambda l, j: (l, 0, j)),
                  pl.BlockSpec((None, 1, tn), lambda l, j: (l, 0, j))],
        out_specs=pl.BlockSpec((None, 8, tn), lambda l, j: (l, 0, j)),
        out_shape=jax.ShapeDtypeStruct((depth, 8, n), F32),
        compiler_params=_cparams("arbitrary", "arbitrary"),
    )(cin, w_mod, b_mod.reshape(depth, 1, n))


def _in_proj_kernel(x_ref, g_ref, ml_ref, mc_ref, w_ref, o_ref, h_scr, *, n_ctx, tm):
    i = pl.program_id(1)
    j = pl.program_id(2)

    @pl.when(j == 0)
    def _():
        y = _rms(x_ref[...]) * g_ref[...]
        row = i * tm + lax.broadcasted_iota(jnp.int32, (tm, 1), 0)
        is_ctx = row < n_ctx
        shift = jnp.where(is_ctx, mc_ref[0:1, :], ml_ref[0:1, :])
        scale = jnp.where(is_ctx, mc_ref[1:2, :], ml_ref[1:2, :])
        h_scr[...] = (y * (1.0 + scale) + shift).astype(BF16)

    o_ref[...] = jnp.dot(h_scr[...], w_ref[...], preferred_element_type=F32)


def _in_proj(xs, norm_g, mods, w_in_bf16, n_ctx, tm):
    bsz, t, d = xs.shape
    n = w_in_bf16.shape[1]
    tn = 1024
    return pl.pallas_call(
        functools.partial(_in_proj_kernel, n_ctx=n_ctx, tm=tm),
        grid=(bsz, t // tm, n // tn),
        in_specs=[pl.BlockSpec((None, tm, d), lambda b, i, j: (b, i, 0)),
                  pl.BlockSpec((1, d), lambda b, i, j: (0, 0)),
                  pl.BlockSpec((None, N_MOD, d), lambda b, i, j: (b, 0, 0)),
                  pl.BlockSpec((None, N_MOD, d), lambda b, i, j: (bsz, 0, 0)),
                  pl.BlockSpec((d, tn), lambda b, i, j: (0, j))],
        out_specs=pl.BlockSpec((None, tm, tn), lambda b, i, j: (b, i, j)),
        out_shape=jax.ShapeDtypeStruct((bsz, t, n), F32),
        scratch_shapes=[pltpu.VMEM((tm, d), BF16)],
        compiler_params=_cparams("arbitrary", "arbitrary", "arbitrary"),
    )(xs, norm_g.reshape(1, d), mods, mods, w_in_bf16)


def _s5_tables(lam_re, lam_im, log_dt, b_re, b_im, c_re, c_im, d_skip):
    hp = lax.Precision.HIGHEST
    g_n, p_n, h_n = b_re.shape
    c_n = S5_CHUNK
    dt = jnp.exp(log_dt)[..., None]
    lre = jnp.minimum(lam_re, -1e-4)
    steps = jnp.arange(c_n + 1, dtype=F32)[:, None, None, None]
    mag = jnp.exp(steps * (lre * dt))
    ang = steps * (lam_im * dt)
    pr, pi = mag * jnp.cos(ang), mag * jnp.sin(ang)
    a_re, a_im = pr[1], pi[1]
    den = lre * lre + lam_im * lam_im
    nr, ni = a_re - 1.0, a_im
    f_re = (nr * lre + ni * lam_im) / den
    f_im = (ni * lre - nr * lam_im) / den
    bb_re = f_re[..., None] * b_re - f_im[..., None] * b_im
    bb_im = f_re[..., None] * b_im + f_im[..., None] * b_re
    w_re = pr[..., None] * bb_re - pi[..., None] * bb_im
    w_im = pr[..., None] * bb_im + pi[..., None] * bb_re
    kl = (jnp.einsum('dghp,ndgpk->ndgkh', c_re, w_re[:c_n], precision=hp)
          - jnp.einsum('dghp,ndgpk->ndgkh', c_im, w_im[:c_n], precision=hp))
    s_i = jnp.arange(c_n)[:, None]
    t_i = jnp.arange(c_n)[None, :]
    kf = jnp.where((t_i >= s_i)[..., None, None, None], kl[:, 0][jnp.clip(t_i - s_i, 0, c_n - 1)], 0.0)
    kb = jnp.where((s_i >= t_i)[..., None, None, None], kl[:, 1][jnp.clip(s_i - t_i, 0, c_n - 1)], 0.0)
    skip = (jnp.eye(c_n, dtype=F32)[:, :, None, None, None]
            * jnp.eye(h_n, dtype=F32)[None, None, None] * d_skip[None, None, :, None, :])
    ktoep = (kf + kb + skip).transpose(2, 0, 3, 1, 4).reshape(g_n, S5_TILE, S5_TILE)

    def st(w, idx, d):
        return w[idx, d].transpose(1, 0, 3, 2).reshape(g_n, S5_TILE, p_n)

    fwd_idx = c_n - 1 - jnp.arange(c_n)
    bwd_idx = jnp.arange(c_n)
    wst = jnp.concatenate([st(w_re, fwd_idx, 0), st(w_re, bwd_idx, 1),
                           st(w_im, fwd_idx, 0), st(w_im, bwd_idx, 1)], axis=-1)

    def out_rows(d, idx):
        cr = c_re[d][None]
        ci = c_im[d][None]
        er = pr[idx, d][:, :, None, :]
        ei = pi[idx, d][:, :, None, :]
        re = (cr * er - ci * ei).transpose(1, 3, 0, 2).reshape(g_n, p_n, S5_TILE)
        im = (cr * ei + ci * er).transpose(1, 3, 0, 2).reshape(g_n, p_n, S5_TILE)
        return re, im

    fo_re, fo_im = out_rows(0, jnp.arange(c_n) + 1)
    bo_re, bo_im = out_rows(1, c_n - jnp.arange(c_n))
    wout = jnp.concatenate([fo_re, bo_re, -fo_im, -bo_im], axis=1)
    w1 = jnp.concatenate([ktoep, wst], axis=-1).astype(BF16)
    ar = jnp.concatenate([pr[c_n, 0], pr[c_n, 1]], axis=-1)
    ai = jnp.concatenate([pi[c_n, 0], pi[c_n, 1]], axis=-1)
    return w1, wout.astype(BF16), ar, ai


def _s5_kernel(u_ref, w1_ref, wo_ref, ar_ref, ai_ref, y_ref, yg, sre, sim, xfr, xfi, xbr, xbi, *, gb, nc, ncc):
    ln = S5_LANES
    c_n = S5_CHUNK
    per_tile = 128 // S5_GROUP
    lane_grp = lax.broadcasted_iota(jnp.int32, (1, 128), 1) // S5_GROUP

    def regroup(pieces, src_off, dst_offs):
        acc = None
        for piece, dst in zip(pieces, dst_offs):
            shift = ((dst - src_off) * S5_GROUP) % 128
            rolled = pltpu.roll(piece, shift, 1) if shift else piece
            acc = rolled if acc is None else jnp.where(lane_grp == dst, rolled, acc)
        return acc

    xs = [u_ref[pl.ds(s, nc, stride=c_n), :] for s in range(c_n)]
    for g in range(gb):
        tile, off = divmod(g, per_tile)
        halves = []
        for half in range(c_n // per_tile):
            pieces = [xs[half * per_tile + i][:, tile * 128:(tile + 1) * 128] for i in range(per_tile)]
            halves.append(regroup(pieces, off, range(per_tile)))
        u_g = jnp.concatenate(halves, axis=1).astype(BF16)
        r = jnp.dot(u_g, w1_ref[g], preferred_element_type=F32)
        yg[g] = r[:, :S5_TILE]
        sre[:, g * ln:(g + 1) * ln] = r[:, S5_TILE:S5_TILE + ln]
        sim[:, g * ln:(g + 1) * ln] = r[:, S5_TILE + ln:]
    are = ar_ref[...]
    aim = ai_ref[...]
    is_f = (lax.broadcasted_iota(jnp.int32, (1, gb * ln), 1) % ln) < S5_STATE

    def step(j, carry):
        xre, xim = carry
        cf = j
        cb = jnp.where(j < ncc, ncc - 1 - j, nc - 1 - (j - ncc))
        s_re = jnp.where(is_f, sre[pl.ds(cf, 1), :], sre[pl.ds(cb, 1), :])
        s_im = jnp.where(is_f, sim[pl.ds(cf, 1), :], sim[pl.ds(cb, 1), :])
        xfr[pl.ds(cf, 1), :] = xre
        xfi[pl.ds(cf, 1), :] = xim
        xbr[pl.ds(cb, 1), :] = xre
        xbi[pl.ds(cb, 1), :] = xim
        return are * xre - aim * xim + s_re, are * xim + aim * xre + s_im

    zero = jnp.zeros((1, gb * ln), F32)
    lax.fori_loop(0, nc, step, (zero, zero))

    is_f1 = lax.broadcasted_iota(jnp.int32, (1, ln), 1) < S5_STATE
    for g in range(gb):
        sl = slice(g * ln, (g + 1) * ln)
        xp = jnp.concatenate([jnp.where(is_f1, xfr[:, sl], xbr[:, sl]),
                              jnp.where(is_f1, xfi[:, sl], xbi[:, sl])], axis=1).astype(BF16)
        yg[g] += jnp.dot(xp, wo_ref[g], preferred_element_type=F32)

    for t in range(c_n):
        t_tile, t_off = divmod(t, per_tile)
        for tile in range(gb // per_tile):
            pieces = [yg[tile * per_tile + i][:, t_tile * 128:(t_tile + 1) * 128] for i in range(per_tile)]
            y_ref[pl.ds(t, nc, stride=c_n), tile * 128:(tile + 1) * 128] = regroup(pieces, t_off, range(per_tile))


def _s5_mix(proj, w1, wout, ar, ai, n_ctx, gb):
    bsz, t, _ = proj.shape
    g_n = w1.shape[0]
    nc = t // S5_CHUNK
    ncc = n_ctx // S5_CHUNK
    ln = S5_LANES
    scr = pltpu.VMEM((nc, gb * ln), F32)
    return pl.pallas_call(
        functools.partial(_s5_kernel, gb=gb, nc=nc, ncc=ncc),
        grid=(bsz, g_n // gb),
        in_specs=[pl.BlockSpec((None, t, gb * S5_GROUP), lambda b, j: (b, 0, j)),
                  pl.BlockSpec((gb, S5_TILE, 2 * S5_TILE), lambda b, j: (j, 0, 0)),
                  pl.BlockSpec((gb, S5_TILE, S5_TILE), lambda b, j: (j, 0, 0)),
                  pl.BlockSpec((None, 1, gb * ln), lambda b, j: (j, 0, 0)),
                  pl.BlockSpec((None, 1, gb * ln), lambda b, j: (j, 0, 0))],
        out_specs=pl.BlockSpec((None, t, gb * S5_GROUP), lambda b, j: (b, 0, j)),
        out_shape=jax.ShapeDtypeStruct((bsz, t, g_n * S5_GROUP), F32),
        scratch_shapes=[pltpu.VMEM((gb, nc, S5_TILE), F32), scr, scr, scr, scr, scr, scr],
        compiler_params=_cparams("arbitrary", "arbitrary"),
    )(proj, w1, wout, ar.reshape(g_n // gb, 1, gb * ln), ai.reshape(g_n // gb, 1, gb * ln))


def _rope_tables(n_ctx, n_lat):
    pos = jnp.arange(n_lat)
    rows = (pos // GRID_W).astype(F32)
    cols = (pos % GRID_W).astype(F32)
    nf = RET_HEAD_DIM // 4
    freqs = ROPE_BASE ** (-jnp.arange(nf, dtype=F32) / nf)
    ar, ac = rows[:, None] * freqs, cols[:, None] * freqs
    cos = jnp.concatenate([jnp.cos(ar), jnp.cos(ar), jnp.cos(ac), jnp.cos(ac)], axis=-1)
    sin = jnp.concatenate([-jnp.sin(ar), jnp.sin(ar), -jnp.sin(ac), jnp.sin(ac)], axis=-1)
    cos = jnp.concatenate([jnp.ones((n_ctx, RET_HEAD_DIM), F32), cos], axis=0)
    sin = jnp.concatenate([jnp.zeros((n_ctx, RET_HEAD_DIM), F32), sin], axis=0)
    return cos, sin


def _ret_kernel(q_ref, k_ref, v_ref, cos_ref, sin_ref, lg_ref, o_ref, qs, ks, vs, *, nchunk, ncc):
    c_n = RET_CHUNK
    nf = RET_HEAD_DIM // 4
    lane = lax.broadcasted_iota(jnp.int32, (1, RET_HEAD_DIM), 1)
    first = (lane % (2 * nf)) < nf

    def rope(x):
        partner = jnp.where(first, pltpu.roll(x, RET_HEAD_DIM - nf, 1), pltpu.roll(x, nf, 1))
        return x * cos_ref[...] + partner * sin_ref[...]

    qs[...] = rope(q_ref[...]).astype(BF16)
    ks[...] = rope(k_ref[...]) * (RET_HEAD_DIM ** -0.5)
    vs[...] = v_ref[...].astype(BF16)

    lg = jnp.minimum(lg_ref[...], -1e-6)
    lgf = lg[0:1, :]
    lgb = lg[1:2, :]
    ri = lax.broadcasted_iota(jnp.int32, (c_n, c_n), 0)
    ci = lax.broadcasted_iota(jnp.int32, (c_n, c_n), 1)
    diff = (ri - ci).astype(F32)
    low = diff >= 0.0
    decay = jnp.where(low, jnp.exp(jnp.where(low, diff, 0.0) * lgf), jnp.exp(jnp.where(low, 0.0, -diff) * lgb))
    pos = ri.astype(F32)
    xi_f = jnp.exp((pos + 1.0) * lgf)
    zeta_f = jnp.exp((c_n - 1.0 - pos) * lgf)
    gch_f = jnp.exp(c_n * lgf)
    xi_b = jnp.exp((c_n - pos) * lgb)
    zeta_b = jnp.exp(pos * lgb)
    gch_b = jnp.exp(c_n * lgb)
    nt = (((1,), (1,)), ((), ()))

    def fwd(c, r):
        sl = pl.ds(pl.multiple_of(c * c_n, c_n), c_n)
        qc, kf, vc = qs[sl, :], ks[sl, :], vs[sl, :]
        s = lax.dot_general(qc, kf.astype(BF16), nt, preferred_element_type=F32)
        o = jnp.dot((s * decay).astype(BF16), vc, preferred_element_type=F32)
        o = o + jnp.dot(qc, r.astype(BF16), preferred_element_type=F32) * xi_f
        o_ref[sl, :] = o
        kz = (kf * zeta_f).T.astype(BF16)
        return gch_f * r + jnp.dot(kz, vc, preferred_element_type=F32)

    zero = jnp.zeros((c_n, c_n), F32)
    lax.fori_loop(0, nchunk, fwd, zero)

    def bwd(j, r):
        c = jnp.where(j < ncc, ncc - 1 - j, nchunk - 1 - (j - ncc))
        sl = pl.ds(pl.multiple_of(c * c_n, c_n), c_n)
        qc, kf, vc = qs[sl, :], ks[sl, :], vs[sl, :]
        o_ref[sl, :] += jnp.dot(qc, r.astype(BF16), preferred_element_type=F32) * xi_b
        kz = (kf * zeta_b).T.astype(BF16)
        return gch_b * r + jnp.dot(kz, vc, preferred_element_type=F32)

    lax.fori_loop(0, nchunk, bwd, zero)


def _retention(proj, cos, sin, lg_tab, n_ctx):
    bsz, t, _ = proj.shape
    hd = RET_HEAD_DIM
    nchunk = t // RET_CHUNK
    ncc = n_ctx // RET_CHUNK
    q0 = BRANCH_WIDTH // hd
    return pl.pallas_call(
        functools.partial(_ret_kernel, nchunk=nchunk, ncc=ncc),
        grid=(bsz, RET_HEADS),
        in_specs=[pl.BlockSpec((None, t, hd), lambda b, h: (b, 0, q0 + h)),
                  pl.BlockSpec((None, t, hd), lambda b, h: (b, 0, 2 * q0 + h)),
                  pl.BlockSpec((None, t, hd), lambda b, h: (b, 0, 3 * q0 + h)),
                  pl.BlockSpec((t, hd), lambda b, h: (0, 0)),
                  pl.BlockSpec((t, hd), lambda b, h: (0, 0)),
                  pl.BlockSpec((None, 8, hd), lambda b, h: (h, 0, 0))],
        out_specs=pl.BlockSpec((None, t, hd), lambda b, h: (b, 0, h)),
        out_shape=jax.ShapeDtypeStruct((bsz, t, BRANCH_WIDTH), F32),
        scratch_shapes=[pltpu.VMEM((t, hd), BF16), pltpu.VMEM((t, hd), F32), pltpu.VMEM((t, hd), BF16)],
        compiler_params=_cparams("arbitrary", "arbitrary"),
    )(proj, proj, proj, cos, sin, lg_tab)


def _merge_kernel(*refs, ngb):
    ys_ref, or_ref, gr_ref = refs[0:3]
    gs_refs = refs[3:3 + ngb]
    gt_refs = refs[3 + ngb:3 + 2 * ngb]
    (x_ref, mod_ref, rg_ref, n2_ref, wglu_ref, wbs_ref, wbr_ref, wout_ref, wr_ref,
     xo_ref, h2_ref, lg_ref) = refs[3 + 2 * ngb:]
    w = BRANCH_WIDTH
    z = _gelu_tanh(ys_ref[...]).astype(BF16)
    zz = jnp.dot(z, wglu_ref[...], preferred_element_type=F32)
    s5b = jnp.dot((zz[:, :w] * jax.nn.sigmoid(zz[:, w:])).astype(BF16), wbs_ref[...], preferred_element_type=F32)
    o = or_ref[...]
    on = jnp.concatenate([_rms(o[:, h * RET_HEAD_DIM:(h + 1) * RET_HEAD_DIM]) for h in range(RET_HEADS)], axis=1)
    on = on * rg_ref[...]
    rb = jnp.dot((on * _silu(gr_ref[...])).astype(BF16), wbr_ref[...], preferred_element_type=F32)
    gate_s = jax.nn.sigmoid(jnp.concatenate([r[...] for r in gs_refs], axis=1))
    gate_r = jax.nn.sigmoid(jnp.concatenate([r[...] for r in gt_refs], axis=1))
    mix = jnp.dot((gate_s * s5b + gate_r * rb).astype(BF16), wout_ref[...], preferred_element_type=F32)
    xn = x_ref[...] + mod_ref[2:3, :] * mix
    xo_ref[...] = xn
    h2 = (_rms(xn) * n2_ref[...]) * (1.0 + mod_ref[4:5, :]) + mod_ref[3:4, :]
    _store_token_rows(h2_ref, h2)
    lg_ref[...] = lax.dot_general(wr_ref[...], h2, (((1,), (1,)), ((), ())), preferred_element_type=F32,
                                  precision=lax.Precision.HIGHEST)


def _merge(y_s5, o_ret, proj, xs, mods, ret_norm_g, norm2_g, wglu, wbs, wbr, wout, w_router, n_ctx, tm):
    bsz, t, d = xs.shape
    w = BRANCH_WIDTH
    gw = 1024
    ngb = d // gw
    gs0 = 5 * w // gw
    ctx_tiles = n_ctx // tm
    row = lambda b, i: (b, i, 0)
    const = lambda b, i: (0, 0)
    one = pl.Buffered(1)

    def gate_spec(k):
        return pl.BlockSpec((None, tm, gw), lambda b, i: (b, i, gs0 + k))

    in_specs = ([pl.BlockSpec((None, tm, w), row), pl.BlockSpec((None, tm, w), row),
                 pl.BlockSpec((None, tm, w), lambda b, i: (b, i, 4))]
                + [gate_spec(k) for k in range(ngb)] + [gate_spec(ngb + k) for k in range(ngb)]
                + [pl.BlockSpec((None, tm, d), row),
                   pl.BlockSpec((None, N_MOD, d), lambda b, i: (jnp.where(i < ctx_tiles, bsz, b), 0, 0)),
                   pl.BlockSpec((1, w), const), pl.BlockSpec((1, d), const),
                   pl.BlockSpec((w, 2 * w), const, pipeline_mode=one),
                   pl.BlockSpec((w, d), const, pipeline_mode=one),
                   pl.BlockSpec((w, d), const, pipeline_mode=one),
                   pl.BlockSpec((d, d), const, pipeline_mode=one),
                   pl.BlockSpec((N_EXPERTS, d), const, pipeline_mode=one)])
    nt = d // 128
    tiles = t // tm
    return pl.pallas_call(
        functools.partial(_merge_kernel, ngb=ngb),
        grid=(bsz, tiles),
        in_specs=in_specs,
        out_specs=[pl.BlockSpec((None, tm, d), row),
                   pl.BlockSpec((tm * nt, 128), lambda b, i: (b * tiles + i, 0)),
                   pl.BlockSpec((N_EXPERTS, tm), lambda b, i: (0, b * tiles + i))],
        out_shape=[jax.ShapeDtypeStruct((bsz, t, d), F32), jax.ShapeDtypeStruct((bsz * t * nt, 128), F32),
                   jax.ShapeDtypeStruct((N_EXPERTS, bsz * t), F32)],
        compiler_params=_cparams("arbitrary", "arbitrary"),
    )(y_s5, o_ret, proj, *([proj] * (2 * ngb)), xs, mods, ret_norm_g.reshape(1, w), norm2_g.reshape(1, d),
      wglu, wbs, wbr, wout, w_router.T)


def _route_kernel(lg_ref, bias_ref, tw_ref, ei_ref, rk_ref, cnt_ref, carry, *, tm):
    i = pl.program_id(0)
    ne = N_EXPERTS
    per_group = ne // N_GROUPS
    neg = -jnp.inf

    @pl.when(i == 0)
    def _():
        carry[...] = jnp.zeros_like(carry)

    scores = jax.nn.sigmoid(lg_ref[...])
    biased = scores + bias_ref[...]
    sub = lax.broadcasted_iota(jnp.int32, (per_group, tm), 0)
    gscore = []
    for g in range(N_GROUPS):
        blk = biased[g * per_group:(g + 1) * per_group, :]
        m1 = jnp.max(blk, axis=0, keepdims=True)
        first = jnp.min(jnp.where(blk == m1, sub, per_group), axis=0, keepdims=True)
        m2 = jnp.max(jnp.where(sub == first, neg, blk), axis=0, keepdims=True)
        gscore.append(m1 + m2)
    masked = []
    for g in range(N_GROUPS):
        beaten = jnp.zeros((1, tm), jnp.int32)
        for j in range(N_GROUPS):
            if j != g:
                wins = (gscore[j] >= gscore[g]) if j < g else (gscore[j] > gscore[g])
                beaten = beaten + wins.astype(jnp.int32)
        keep = beaten < TOPK_GROUPS
        masked.append(jnp.where(keep, biased[g * per_group:(g + 1) * per_group, :], neg))
    mv = jnp.concatenate(masked, axis=0)
    eidx = lax.broadcasted_iota(jnp.int32, (ne, tm), 0)
    beaten = jnp.zeros((ne, tm), jnp.int32)
    for j in range(ne):
        vj = mv[j:j + 1, :]
        wins = (vj > mv) | ((vj == mv) & (j < eidx))
        beaten = beaten + wins.astype(jnp.int32)
    sel = beaten < TOP_K
    sel_w = jnp.where(sel, scores, 0.0)
    wd = sel_w / jnp.sum(sel_w, axis=0, keepdims=True) * ROUTED_SCALE
    sel_b = sel.astype(BF16)
    r_i = lax.broadcasted_iota(jnp.int32, (ne, ne), 0)
    c_i = lax.broadcasted_iota(jnp.int32, (ne, ne), 1)
    slot = jnp.dot((c_i < r_i).astype(BF16), sel_b, preferred_element_type=F32)
    t_r = lax.broadcasted_iota(jnp.int32, (tm, tm), 0)
    t_c = lax.broadcasted_iota(jnp.int32, (tm, tm), 1)
    rank = jnp.dot(sel_b, (t_r < t_c).astype(BF16), preferred_element_type=F32) + carry[:, 0:1]
    carry[...] = carry[...] + jnp.sum(sel.astype(F32), axis=1, keepdims=True)
    cnt_ref[...] = carry[...]
    eidx_f = eidx.astype(F32)
    for k in range(TOP_K):
        mk = sel & (slot == float(k))
        tw_ref[k:k + 1, :] = jnp.sum(jnp.where(mk, wd, 0.0), axis=0, keepdims=True)
        ei_ref[k:k + 1, :] = jnp.sum(jnp.where(mk, eidx_f, 0.0), axis=0, keepdims=True).astype(jnp.int32)
        rk_ref[k:k + 1, :] = jnp.sum(jnp.where(mk, rank, 0.0), axis=0, keepdims=True).astype(jnp.int32)


def _route(logits_t, b_router, tm):
    ne, n = logits_t.shape
    col = lambda i: (0, i)
    return pl.pallas_call(
        functools.partial(_route_kernel, tm=tm),
        grid=(n // tm,),
        in_specs=[pl.BlockSpec((ne, tm), col), pl.BlockSpec((ne, 1), lambda i: (0, 0))],
        out_specs=[pl.BlockSpec((TOP_K, tm), col), pl.BlockSpec((TOP_K, tm), col), pl.BlockSpec((TOP_K, tm), col),
                   pl.BlockSpec((ne, 128), lambda i: (0, 0))],
        out_shape=[jax.ShapeDtypeStruct((TOP_K, n), F32), jax.ShapeDtypeStruct((TOP_K, n), jnp.int32),
                   jax.ShapeDtypeStruct((TOP_K, n), jnp.int32), jax.ShapeDtypeStruct((ne, 128), F32)],
        scratch_shapes=[pltpu.VMEM((ne, 128), F32)],
        compiler_params=_cparams("arbitrary"),
    )(logits_t, b_router.reshape(ne, 1))


def _row_tok_kernel(lo_ref, hi_ref, pos_ref, out_ref, *, tb):
    j = pl.program_id(0)

    @pl.when(j == 0)
    def _():
        def per_range(e, carry):
            def fill(p, c):
                out_ref[p] = 0
                return c
            lax.fori_loop(lo_ref[e], hi_ref[e], fill, 0)
            return carry
        lax.fori_loop(0, lo_ref.shape[0], per_range, 0)

    def body(n, carry):
        for k in range(TOP_K):
            out_ref[pos_ref[k, n]] = j * tb + n
        return carry
    lax.fori_loop(0, tb, body, 0, unroll=4)


def _dispatch_plan(ei8, rk8, cnt, n_blocks, tb):
    n = ei8.shape[1]
    counts = cnt[:, 0].astype(jnp.int32)
    padded = (counts + MOE_ROWS - 1) // MOE_ROWS * MOE_ROWS
    pad_end = jnp.cumsum(padded)
    pad_start = pad_end - padded
    onehot = ei8[..., None] == jnp.arange(N_EXPERTS, dtype=jnp.int32)
    pos8 = rk8 + jnp.sum(jnp.where(onehot, pad_start, 0), axis=-1)
    blk_e = jnp.minimum(jnp.sum(jnp.arange(n_blocks, dtype=jnp.int32)[:, None] * MOE_ROWS >= pad_end[None, :],
                                axis=1), N_EXPERTS - 1).astype(jnp.int32)
    n_used = (pad_end[-1] // MOE_ROWS).astype(jnp.int32).reshape(1)
    grid_spec = pltpu.PrefetchScalarGridSpec(
        num_scalar_prefetch=2,
        grid=(n // tb,),
        in_specs=[pl.BlockSpec((TOP_K, tb), lambda j, lo, hi: (0, j), memory_space=pltpu.SMEM)],
        out_specs=pl.BlockSpec(memory_space=pltpu.SMEM))
    row_tok = pl.pallas_call(
        functools.partial(_row_tok_kernel, tb=tb),
        grid_spec=grid_spec,
        out_shape=jax.ShapeDtypeStruct((n_blocks * MOE_ROWS,), jnp.int32),
        compiler_params=_cparams("arbitrary"),
    )(jnp.concatenate([pad_start + counts, pad_end[-1:]]).astype(jnp.int32),
      jnp.concatenate([pad_end, jnp.full((1,), n_blocks * MOE_ROWS)]).astype(jnp.int32), pos8)
    return row_tok, pos8.reshape(-1), blk_e, n_used


def _expert_kernel(blk_e_ref, nused_ref, tok_ref, h_hbm, wg_ref, wu_ref, wd_ref, y_ref,
                   xbuf, sem, wgb, wub, wdb):
    bm = MOE_ROWS
    i = pl.program_id(0)
    n_used = nused_ref[0]
    slot = i % 2

    nt = xbuf.shape[1] // bm
    d = nt * 128

    def row_copy(tok, slot_, r):
        return pltpu.make_async_copy(h_hbm.at[pl.ds(pl.multiple_of(tok * nt, nt), nt), :],
                                     xbuf.at[slot_, pl.ds(pl.multiple_of(r * nt, nt), nt), :], sem.at[slot_])

    def issue(blk, slot_):
        def body(r, carry):
            row_copy(tok_ref[blk * bm + r], slot_, r).start()
            return carry
        lax.fori_loop(0, bm, body, 0, unroll=8)

    @pl.when(jnp.logical_and(i == 0, n_used > 0))
    def _():
        issue(0, 0)

    @pl.when(i + 1 < n_used)
    def _():
        issue(i + 1, 1 - slot)

    @pl.when(i < n_used)
    def _():
        def wait_body(r, carry):
            row_copy(0, slot, r).wait()
            return carry
        lax.fori_loop(0, bm, wait_body, 0, unroll=8)

        new_expert = jnp.logical_or(i == 0, blk_e_ref[i] != blk_e_ref[jnp.maximum(i - 1, 0)])

        @pl.when(new_expert)
        def _():
            wgb[...] = wg_ref[...].astype(BF16)
            wub[...] = wu_ref[...].astype(BF16)
            wdb[...] = wd_ref[...].astype(BF16)

        x = _load_token_rows(xbuf.at[slot], bm, d).astype(BF16)
        g = jnp.dot(x, wgb[...], preferred_element_type=F32)
        u = jnp.dot(x, wub[...], preferred_element_type=F32)
        _store_token_rows(y_ref, jnp.dot((_silu(g) * u).astype(BF16), wdb[...], preferred_element_type=F32))

    @pl.when(i >= n_used)
    def _():
        y_ref[...] = jnp.zeros_like(y_ref)


def _experts(h2c, row_tok, blk_e, n_used, w_gate, w_up, w_down, n_blocks):
    d, de = w_gate.shape[-2:]
    nt = d // 128
    bm = MOE_ROWS
    grid_spec = pltpu.PrefetchScalarGridSpec(
        num_scalar_prefetch=3,
        grid=(n_blocks,),
        in_specs=[pl.BlockSpec(memory_space=pl.ANY),
                  pl.BlockSpec((None, d, de), lambda i, be, nu, tk: (be[i], 0, 0)),
                  pl.BlockSpec((None, d, de), lambda i, be, nu, tk: (be[i], 0, 0)),
                  pl.BlockSpec((None, de, d), lambda i, be, nu, tk: (be[i], 0, 0))],
        out_specs=pl.BlockSpec((bm * nt, 128), lambda i, be, nu, tk: (i, 0)),
        scratch_shapes=[pltpu.VMEM((2, bm * nt, 128), F32), pltpu.SemaphoreType.DMA((2,)),
                        pltpu.VMEM((d, de), BF16), pltpu.VMEM((d, de), BF16), pltpu.VMEM((de, d), BF16)])
    return pl.pallas_call(
        _expert_kernel,
        grid_spec=grid_spec,
        out_shape=jax.ShapeDtypeStruct((n_blocks * bm * nt, 128), F32),
        compiler_params=_cparams("arbitrary"),
    )(blk_e, n_used, row_tok, h2c, w_gate, w_up, w_down)


def _combine_kernel(pos_ref, ys_hbm, tw_ref, h2_ref, x_ref, mod_ref, wsg_ref, wsu_ref, wsd_ref, fg_ref,
                    o_ref, buf, sem, *, tmc, final):
    i = pl.program_id(0)
    n_tiles = pl.num_programs(0)
    n_tok = n_tiles * tmc
    slot = i % 2
    nt = buf.shape[2] // tmc
    d = nt * 128

    def row_copy(p, slot_, k, t):
        return pltpu.make_async_copy(ys_hbm.at[pl.ds(pl.multiple_of(p * nt, nt), nt), :],
                                     buf.at[slot_, k, pl.ds(pl.multiple_of(t * nt, nt), nt), :], sem.at[slot_])

    def issue(tile, slot_):
        def body(t, carry):
            for k in range(TOP_K):
                row_copy(pos_ref[k * n_tok + tile * tmc + t], slot_, k, t).start()
            return carry
        lax.fori_loop(0, tmc, body, 0)

    @pl.when(i == 0)
    def _():
        issue(0, 0)

    @pl.when(i + 1 < n_tiles)
    def _():
        issue(i + 1, 1 - slot)

    h = _load_token_rows(h2_ref, tmc, d).astype(BF16)
    sg = jnp.dot(h, wsg_ref[...], preferred_element_type=F32)
    su = jnp.dot(h, wsu_ref[...], preferred_element_type=F32)
    ff = jnp.dot((_silu(sg) * su).astype(BF16), wsd_ref[...], preferred_element_type=F32)

    def wait_body(t, carry):
        for k in range(TOP_K):
            row_copy(0, slot, k, t).wait()
        return carry
    lax.fori_loop(0, tmc, wait_body, 0)

    tw = tw_ref[...]
    routed = tw[:, 0:1] * _load_token_rows(buf.at[slot, 0], tmc, d)
    for k in range(1, TOP_K):
        routed = routed + tw[:, k:k + 1] * _load_token_rows(buf.at[slot, k], tmc, d)
    xn = x_ref[...] + mod_ref[5:6, :] * (routed + ff)
    if final:
        xn = _rms(xn) * fg_ref[...]
    o_ref[...] = xn


def _combine(ys, pos, top_w, h2c, xs, mods, wsg, wsu, wsd, final_g, n_ctx, t_per_batch, final):
    d, de = wsg.shape
    nt = d // 128
    n = h2c.shape[0] // nt
    tmc = COMBINE_TOKENS
    bsz = n // t_per_batch
    tiles_per_batch = t_per_batch // tmc
    ctx_tiles = n_ctx // tmc
    const = lambda i, p: (0, 0)
    row = lambda i, p: (i, 0)

    def mod_map(i, p):
        return (jnp.where(i % tiles_per_batch < ctx_tiles, bsz, i // tiles_per_batch), 0, 0)

    grid_spec = pltpu.PrefetchScalarGridSpec(
        num_scalar_prefetch=1,
        grid=(n // tmc,),
        in_specs=[pl.BlockSpec(memory_space=pl.ANY),
                  pl.BlockSpec((tmc, TOP_K), row),
                  pl.BlockSpec((tmc * nt, 128), row),
                  pl.BlockSpec((tmc, d), row),
                  pl.BlockSpec((None, N_MOD, d), mod_map),
                  pl.BlockSpec((d, de), const), pl.BlockSpec((d, de), const), pl.BlockSpec((de, d), const),
                  pl.BlockSpec((1, d), const)],
        out_specs=pl.BlockSpec((tmc, d), row),
        scratch_shapes=[pltpu.VMEM((2, TOP_K, tmc * nt, 128), F32), pltpu.SemaphoreType.DMA((2,))])
    return pl.pallas_call(
        functools.partial(_combine_kernel, tmc=tmc, final=final),
        grid_spec=grid_spec,
        out_shape=jax.ShapeDtypeStruct((n, d), F32),
        compiler_params=_cparams("arbitrary"),
    )(pos, ys, top_w, h2c, xs.reshape(n, d), mods, wsg, wsu, wsd, final_g.reshape(1, d))


def _pick_tile(n, candidates):
    for c in candidates:
        if n % c == 0:
            return c
    raise ValueError(f"no tile for {n}")


def kernel(x, c, ctx, c_ctx, w_mod, b_mod, norm1_g, norm2_g, final_g, w_in, s5_lam_re, s5_lam_im, s5_log_dt, s5_b_re, s5_b_im, s5_c_re, s5_c_im, s5_d, s5_w_glu, ret_log_decay, ret_norm_g, w_br_s5, w_br_ret, w_out, moe_router, moe_router_bias, moe_w_gate, moe_w_up, moe_w_down, sh_w_gate, sh_w_up, sh_w_down):
    bsz, n_lat, d = x.shape
    n_ctx = ctx.shape[1]
    depth = w_mod.shape[0]
    t = n_ctx + n_lat
    n_tok = bsz * t
    g_n = s5_b_re.shape[1]
    assert n_ctx % RET_CHUNK == 0 and n_lat % RET_CHUNK == 0 and bsz + 1 <= 8
    assert d % 1024 == 0 and n_tok % COMBINE_TOKENS == 0 and n_ctx % COMBINE_TOKENS == 0

    cin = jnp.zeros((8, d), F32).at[:bsz].set(c).at[bsz].set(c_ctx)
    mods_all = _modulation(cin, w_mod, b_mod).reshape(depth, 8, N_MOD, d)
    cos, sin = _rope_tables(n_ctx, n_lat)
    xs = jnp.concatenate([ctx, x], axis=1)

    tm_in = _pick_tile(t, (544, 272, 256, 128))
    tm_merge = _pick_tile(n_ctx, (256, 128))
    gb = 128 // S5_GROUP
    tm_route = _pick_tile(n_tok, (256, 128))
    tb_plan = _pick_tile(n_tok, (2176, 1024, 512, 256, 128))
    n_blocks = -(-(n_tok * TOP_K + N_EXPERTS * (MOE_ROWS - 1)) // MOE_ROWS)

    for l in range(depth):
        mods = mods_all[l]
        proj = _in_proj(xs, norm1_g[l], mods, w_in[l].astype(BF16), n_ctx, tm_in)

        w1, wo, ar, ai = _s5_tables(s5_lam_re[l], s5_lam_im[l], s5_log_dt[l], s5_b_re[l], s5_b_im[l],
                                    s5_c_re[l], s5_c_im[l], s5_d[l])
        y_s5 = _s5_mix(proj, w1, wo, ar, ai, n_ctx, gb)

        lg_tab = jnp.zeros((RET_HEADS, 8, RET_HEAD_DIM), F32).at[:, 0:2, :].set(
            jnp.broadcast_to(ret_log_decay[l].T[:, :, None], (RET_HEADS, 2, RET_HEAD_DIM)))
        o_ret = _retention(proj, cos, sin, lg_tab, n_ctx)

        xs, h2c, logits_t = _merge(y_s5, o_ret, proj, xs, mods, ret_norm_g[l], norm2_g[l],
                                   s5_w_glu[l].astype(BF16), w_br_s5[l].astype(BF16), w_br_ret[l].astype(BF16),
                                   w_out[l].astype(BF16), moe_router[l], n_ctx, tm_merge)

        tw8, ei8, rk8, cnt = _route(logits_t, moe_router_bias[l], tm_route)
        row_tok, pos, blk_e, n_used = _dispatch_plan(ei8, rk8, cnt, n_blocks, tb_plan)
        ys = _experts(h2c, row_tok, blk_e, n_used, moe_w_gate[l], moe_w_up[l], moe_w_down[l], n_blocks)
        out = _combine(ys, pos, tw8.T, h2c, xs, mods, sh_w_gate[l].astype(BF16), sh_w_up[l].astype(BF16),
                       sh_w_down[l].astype(BF16), final_g, n_ctx, t, final=(l == depth - 1))
        xs = out.reshape(bsz, t, d)

    return xs[:, n_ctx:]
```

```python
import functools

import jax
import jax.numpy as jnp
from jax import lax
from jax.experimental import pallas as pl
from jax.experimental.pallas import tpu as pltpu

F32 = jnp.float32
BF16 = jnp.bfloat16

NORM_EPS = 1e-6
N_MOD = 6
GRID_W = 64
ROPE_BASE = 10000.0

S5_GROUP = 16
S5_STATE = 64
S5_CHUNK = 16
S5_TILE = S5_CHUNK * S5_GROUP
S5_LANES = 2 * S5_STATE

RET_HEADS = 8
RET_HEAD_DIM = 128
RET_CHUNK = 128
BRANCH_WIDTH = RET_HEADS * RET_HEAD_DIM

N_EXPERTS = 64
TOP_K = 8
N_GROUPS = 8
TOPK_GROUPS = 4
ROUTED_SCALE = 2.5
MOE_ROWS = 256
COMBINE_TOKENS = 128

VMEM_LIMIT = 56 * 1024 * 1024


def _cparams(*sem):
    return pltpu.CompilerParams(dimension_semantics=sem, vmem_limit_bytes=VMEM_LIMIT)


def _silu(x):
    return x * jax.nn.sigmoid(x)


def _gelu_tanh(x):
    return 0.5 * x * (1.0 + jnp.tanh(0.7978845608028654 * (x + 0.044715 * x * x * x)))


def _rms(x):
    return x * lax.rsqrt(jnp.mean(x * x, axis=-1, keepdims=True) + NORM_EPS)


U32 = jnp.uint32
HIGH_HALF = 0xFFFF0000


def _bf16_bits(x):
    return lax.bitcast_convert_type(x.astype(BF16).astype(F32), U32)


def _store_token_rows(ref, val):
    rows, d = val.shape
    nw = d // 256
    for j in range(nw):
        lo = _bf16_bits(val[:, j * 128:(j + 1) * 128])
        hi = _bf16_bits(val[:, (j + nw) * 128:(j + nw + 1) * 128])
        ref[pl.ds(j, rows, stride=nw), :] = (hi & U32(HIGH_HALF)) | (lo >> 16)


def _load_token_rows(ref, rows, d):
    nw = d // 256
    words = [ref[pl.ds(j, rows, stride=nw), :] for j in range(nw)]
    lo = [lax.bitcast_convert_type(w << 16, F32) for w in words]
    hi = [lax.bitcast_convert_type(w & U32(HIGH_HALF), F32) for w in words]
    return jnp.concatenate(lo + hi, axis=1)


def _mod_kernel(c_ref, w_ref, b_ref, o_ref):
    a = _silu(c_ref[...]).astype(BF16)
    o_ref[...] = jnp.dot(a, w_ref[...].astype(BF16), preferred_element_type=F32) + b_ref[...]


def _modulation(cin, w_mod, b_mod):
    depth, d, n = w_mod.shape
    tn = 1024
    return pl.pallas_call(
        _mod_kernel,
        grid=(depth, n // tn),
        in_specs=[pl.BlockSpec((8, d), lambda l, j: (0, 0)),
                  pl.BlockSpec((None, d, tn), lambda l, j: (l, 0, j)),
                  pl.BlockSpec((None, 1, tn), lambda l, j: (l, 0, j))],
        out_specs=pl.BlockSpec((None, 8, tn), lambda l, j: (l, 0, j)),
        out_shape=jax.ShapeDtypeStruct((depth, 8, n), F32),
        compiler_params=_cparams("arbitrary", "arbitrary"),
    )(cin, w_mod, b_mod.reshape(depth, 1, n))


def _in_proj_kernel(x_ref, g_ref, ml_ref, mc_ref, w_ref, o_ref, h_scr, *, n_ctx, tm):
    i = pl.program_id(1)
    j = pl.program_id(2)

    @pl.when(j == 0)
    def _():
        y = _rms(x_ref[...]) * g_ref[...]
        row = i * tm + lax.broadcasted_iota(jnp.int32, (tm, 1), 0)
        is_ctx = row < n_ctx
        shift = jnp.where(is_ctx, mc_ref[0:1, :], ml_ref[0:1, :])
        scale = jnp.where(is_ctx, mc_ref[1:2, :], ml_ref[1:2, :])
        h_scr[...] = (y * (1.0 + scale) + shift).astype(BF16)

    o_ref[...] = jnp.dot(h_scr[...], w_ref[...], preferred_element_type=F32)


def _in_proj(xs, norm_g, mods, w_in_bf16, n_ctx, tm):
    bsz, t, d = xs.shape
    n = w_in_bf16.shape[1]
    tn = 1024
    return pl.pallas_call(
        functools.partial(_in_proj_kernel, n_ctx=n_ctx, tm=tm),
        grid=(bsz, t // tm, n // tn),
        in_specs=[pl.BlockSpec((None, tm, d), lambda b, i, j: (b, i, 0)),
                  pl.BlockSpec((1, d), lambda b, i, j: (0, 0)),
                  pl.BlockSpec((None, N_MOD, d), lambda b, i, j: (b, 0, 0)),
                  pl.BlockSpec((None, N_MOD, d), lambda b, i, j: (bsz, 0, 0)),
                  pl.BlockSpec((d, tn), lambda b, i, j: (0, j))],
        out_specs=pl.BlockSpec((None, tm, tn), lambda b, i, j: (b, i, j)),
        out_shape=jax.ShapeDtypeStruct((bsz, t, n), F32),
        scratch_shapes=[pltpu.VMEM((tm, d), BF16)],
        compiler_params=_cparams("arbitrary", "arbitrary", "arbitrary"),
    )(xs, norm_g.reshape(1, d), mods, mods, w_in_bf16)


def _s5_tables(lam_re, lam_im, log_dt, b_re, b_im, c_re, c_im, d_skip):
    hp = lax.Precision.HIGHEST
    g_n, p_n, h_n = b_re.shape
    c_n = S5_CHUNK
    dt = jnp.exp(log_dt)[..., None]
    lre = jnp.minimum(lam_re, -1e-4)
    steps = jnp.arange(c_n + 1, dtype=F32)[:, None, None, None]
    mag = jnp.exp(steps * (lre * dt))
    ang = steps * (lam_im * dt)
    pr, pi = mag * jnp.cos(ang), mag * jnp.sin(ang)
    a_re, a_im = pr[1], pi[1]
    den = lre * lre + lam_im * lam_im
    nr, ni = a_re - 1.0, a_im
    f_re = (nr * lre + ni * lam_im) / den
    f_im = (ni * lre - nr * lam_im) / den
    bb_re = f_re[..., None] * b_re - f_im[..., None] * b_im
    bb_im = f_re[..., None] * b_im + f_im[..., None] * b_re
    w_re = pr[..., None] * bb_re - pi[..., None] * bb_im
    w_im = pr[..., None] * bb_im + pi[..., None] * bb_re
    kl = (jnp.einsum('dghp,ndgpk->ndgkh', c_re, w_re[:c_n], precision=hp)
          - jnp.einsum('dghp,ndgpk->ndgkh', c_im, w_im[:c_n], precision=hp))
    s_i = jnp.arange(c_n)[:, None]
    t_i = jnp.arange(c_n)[None, :]
    kf = jnp.where((t_i >= s_i)[..., None, None, None], kl[:, 0][jnp.clip(t_i - s_i, 0, c_n - 1)], 0.0)
    kb = jnp.where((s_i >= t_i)[..., None, None, None], kl[:, 1][jnp.clip(s_i - t_i, 0, c_n - 1)], 0.0)
    skip = (jnp.eye(c_n, dtype=F32)[:, :, None, None, None]
            * jnp.eye(h_n, dtype=F32)[None, None, None] * d_skip[None, None, :, None, :])
    ktoep = (kf + kb + skip).transpose(2, 0, 3, 1, 4).reshape(g_n, S5_TILE, S5_TILE)

    def st(w, idx, d):
        return w[idx, d].transpose(1, 0, 3, 2).reshape(g_n, S5_TILE, p_n)

    fwd_idx = c_n - 1 - jnp.arange(c_n)
    bwd_idx = jnp.arange(c_n)
    wst = jnp.concatenate([st(w_re, fwd_idx, 0), st(w_re, bwd_idx, 1),
                           st(w_im, fwd_idx, 0), st(w_im, bwd_idx, 1)], axis=-1)

    def out_rows(d, idx):
        cr = c_re[d][None]
        ci = c_im[d][None]
        er = pr[idx, d][:, :, None, :]
        ei = pi[idx, d][:, :, None, :]
        re = (cr * er - ci * ei).transpose(1, 3, 0, 2).reshape(g_n, p_n, S5_TILE)
        im = (cr * ei + ci * er).transpose(1, 3, 0, 2).reshape(g_n, p_n, S5_TILE)
        return re, im

    fo_re, fo_im = out_rows(0, jnp.arange(c_n) + 1)
    bo_re, bo_im = out_rows(1, c_n - jnp.arange(c_n))
    wout = jnp.concatenate([fo_re, bo_re, -fo_im, -bo_im], axis=1)
    w1 = jnp.concatenate([ktoep, wst], axis=-1).astype(BF16)
    ar = jnp.concatenate([pr[c_n, 0], pr[c_n, 1]], axis=-1)
    ai = jnp.concatenate([pi[c_n, 0], pi[c_n, 1]], axis=-1)
    return w1, wout.astype(BF16), ar, ai


def _s5_kernel(u_ref, w1_ref, wo_ref, ar_ref, ai_ref, y_ref, yg, sre, sim, xfr, xfi, xbr, xbi, *, gb, nc, ncc):
    ln = S5_LANES
    c_n = S5_CHUNK
    per_tile = 128 // S5_GROUP
    lane_grp = lax.broadcasted_iota(jnp.int32, (1, 128), 1) // S5_GROUP

    def regroup(pieces, src_off, dst_offs):
        acc = None
        for piece, dst in zip(pieces, dst_offs):
            shift = ((dst - src_off) * S5_GROUP) % 128
            rolled = pltpu.roll(piece, shift, 1) if shift else piece
            acc = rolled if acc is None else jnp.where(lane_grp == dst, rolled, acc)
        return acc

    xs = [u_ref[pl.ds(s, nc, stride=c_n), :] for s in range(c_n)]
    for g in range(gb):
        tile, off = divmod(g, per_tile)
        halves = []
        for half in range(c_n // per_tile):
            pieces = [xs[half * per_tile + i][:, tile * 128:(tile + 1) * 128] for i in range(per_tile)]
            halves.append(regroup(pieces, off, range(per_tile)))
        u_g = jnp.concatenate(halves, axis=1).astype(BF16)
        r = jnp.dot(u_g, w1_ref[g], preferred_element_type=F32)
        yg[g] = r[:, :S5_TILE]
        sre[:, g * ln:(g + 1) * ln] = r[:, S5_TILE:S5_TILE + ln]
        sim[:, g * ln:(g + 1) * ln] = r[:, S5_TILE + ln:]
    are = ar_ref[...]
    aim = ai_ref[...]
    is_f = (lax.broadcasted_iota(jnp.int32, (1, gb * ln), 1) % ln) < S5_STATE

    def step(j, carry):
        xre, xim = carry
        cf = j
        cb = jnp.where(j < ncc, ncc - 1 - j, nc - 1 - (j - ncc))
        s_re = jnp.where(is_f, sre[pl.ds(cf, 1), :], sre[pl.ds(cb, 1), :])
        s_im = jnp.where(is_f, sim[pl.ds(cf, 1), :], sim[pl.ds(cb, 1), :])
        xfr[pl.ds(cf, 1), :] = xre
        xfi[pl.ds(cf, 1), :] = xim
        xbr[pl.ds(cb, 1), :] = xre
        xbi[pl.ds(cb, 1), :] = xim
        return are * xre - aim * xim + s_re, are * xim + aim * xre + s_im

    zero = jnp.zeros((1, gb * ln), F32)
    lax.fori_loop(0, nc, step, (zero, zero))

    is_f1 = lax.broadcasted_iota(jnp.int32, (1, ln), 1) < S5_STATE
    for g in range(gb):
        sl = slice(g * ln, (g + 1) * ln)
        xp = jnp.concatenate([jnp.where(is_f1, xfr[:, sl], xbr[:, sl]),
                              jnp.where(is_f1, xfi[:, sl], xbi[:, sl])], axis=1).astype(BF16)
        yg[g] += jnp.dot(xp, wo_ref[g], preferred_element_type=F32)

    for t in range(c_n):
        t_tile, t_off = divmod(t, per_tile)
        for tile in range(gb // per_tile):
            pieces = [yg[tile * per_tile + i][:, t_tile * 128:(t_tile + 1) * 128] for i in range(per_tile)]
            y_ref[pl.ds(t, nc, stride=c_n), tile * 128:(tile + 1) * 128] = regroup(pieces, t_off, range(per_tile))


def _s5_mix(proj, w1, wout, ar, ai, n_ctx, gb):
    bsz, t, _ = proj.shape
    g_n = w1.shape[0]
    nc = t // S5_CHUNK
    ncc = n_ctx // S5_CHUNK
    ln = S5_LANES
    scr = pltpu.VMEM((nc, gb * ln), F32)
    return pl.pallas_call(
        functools.partial(_s5_kernel, gb=gb, nc=nc, ncc=ncc),
        grid=(bsz, g_n // gb),
        in_specs=[pl.BlockSpec((None, t, gb * S5_GROUP), lambda b, j: (b, 0, j)),
                  pl.BlockSpec((gb, S5_TILE, 2 * S5_TILE), lambda b, j: (j, 0, 0)),
                  pl.BlockSpec((gb, S5_TILE, S5_TILE), lambda b, j: (j, 0, 0)),
                  pl.BlockSpec((None, 1, gb * ln), lambda b, j: (j, 0, 0)),
                  pl.BlockSpec((None, 1, gb * ln), lambda b, j: (j, 0, 0))],
        out_specs=pl.BlockSpec((None, t, gb * S5_GROUP), lambda b, j: (b, 0, j)),
        out_shape=jax.ShapeDtypeStruct((bsz, t, g_n * S5_GROUP), F32),
        scratch_shapes=[pltpu.VMEM((gb, nc, S5_TILE), F32), scr, scr, scr, scr, scr, scr],
        compiler_params=_cparams("arbitrary", "arbitrary"),
    )(proj, w1, wout, ar.reshape(g_n // gb, 1, gb * ln), ai.reshape(g_n // gb, 1, gb * ln))


def _rope_tables(n_ctx, n_lat):
    pos = jnp.arange(n_lat)
    rows = (pos // GRID_W).astype(F32)
    cols = (pos % GRID_W).astype(F32)
    nf = RET_HEAD_DIM // 4
    freqs = ROPE_BASE ** (-jnp.arange(nf, dtype=F32) / nf)
    ar, ac = rows[:, None] * freqs, cols[:, None] * freqs
    cos = jnp.concatenate([jnp.cos(ar), jnp.cos(ar), jnp.cos(ac), jnp.cos(ac)], axis=-1)
    sin = jnp.concatenate([-jnp.sin(ar), jnp.sin(ar), -jnp.sin(ac), jnp.sin(ac)], axis=-1)
    cos = jnp.concatenate([jnp.ones((n_ctx, RET_HEAD_DIM), F32), cos], axis=0)
    sin = jnp.concatenate([jnp.zeros((n_ctx, RET_HEAD_DIM), F32), sin], axis=0)
    return cos, sin


def _ret_kernel(q_ref, k_ref, v_ref, cos_ref, sin_ref, lg_ref, o_ref, qs, ks, vs, ob, *, nchunk, ncc):
    c_n = RET_CHUNK
    nf = RET_HEAD_DIM // 4
    lane = lax.broadcasted_iota(jnp.int32, (1, RET_HEAD_DIM), 1)
    first = (lane % (2 * nf)) < nf

    def rope(x):
        partner = jnp.where(first, pltpu.roll(x, RET_HEAD_DIM - nf, 1), pltpu.roll(x, nf, 1))
        return x * cos_ref[...] + partner * sin_ref[...]

    qs[...] = rope(q_ref[...]).astype(BF16)
    ks[...] = rope(k_ref[...]) * (RET_HEAD_DIM ** -0.5)
    vs[...] = v_ref[...].astype(BF16)

    lg = jnp.minimum(lg_ref[...], -1e-6)
    lgf = lg[0:1, :]
    lgb = lg[1:2, :]
    ri = lax.broadcasted_iota(jnp.int32, (c_n, c_n), 0)
    ci = lax.broadcasted_iota(jnp.int32, (c_n, c_n), 1)
    diff = (ri - ci).astype(F32)
    low = diff >= 0.0
    decay = jnp.where(low, jnp.exp(jnp.where(low, diff, 0.0) * lgf), jnp.exp(jnp.where(low, 0.0, -diff) * lgb))
    pos = ri.astype(F32)
    xi_f = jnp.exp((pos + 1.0) * lgf)
    zeta_f = jnp.exp((c_n - 1.0 - pos) * lgf)
    gch_f = jnp.exp(c_n * lgf)
    xi_b = jnp.exp((c_n - pos) * lgb)
    zeta_b = jnp.exp(pos * lgb)
    gch_b = jnp.exp(c_n * lgb)
    nt = (((1,), (1,)), ((), ()))

    def step(j, carry):
        rf, rb = carry
        sl = pl.ds(pl.multiple_of(j * c_n, c_n), c_n)
        qc, kf, vc = qs[sl, :], ks[sl, :], vs[sl, :]
        s = lax.dot_general(qc, kf.astype(BF16), nt, preferred_element_type=F32)
        o = jnp.dot((s * decay).astype(BF16), vc, preferred_element_type=F32)
        o_ref[sl, :] = o + jnp.dot(qc, rf.astype(BF16), preferred_element_type=F32) * xi_f
        rf = gch_f * rf + jnp.dot((kf * zeta_f).T.astype(BF16), vc, preferred_element_type=F32)

        cb = jnp.where(j < ncc, ncc - 1 - j, nchunk - 1 - (j - ncc))
        sb = pl.ds(pl.multiple_of(cb * c_n, c_n), c_n)
        qb, kb, vb = qs[sb, :], ks[sb, :], vs[sb, :]
        ob[sb, :] = jnp.dot(qb, rb.astype(BF16), preferred_element_type=F32) * xi_b
        rb = gch_b * rb + jnp.dot((kb * zeta_b).T.astype(BF16), vb, preferred_element_type=F32)
        return rf, rb

    zero = jnp.zeros((c_n, c_n), F32)
    lax.fori_loop(0, nchunk, step, (zero, zero), unroll=2)
    o_ref[...] += ob[...]


def _retention(proj, cos, sin, lg_tab, n_ctx):
    bsz, t, _ = proj.shape
    hd = RET_HEAD_DIM
    nchunk = t // RET_CHUNK
    ncc = n_ctx // RET_CHUNK
    q0 = BRANCH_WIDTH // hd
    return pl.pallas_call(
        functools.partial(_ret_kernel, nchunk=nchunk, ncc=ncc),
        grid=(bsz, RET_HEADS),
        in_specs=[pl.BlockSpec((None, t, hd), lambda b, h: (b, 0, q0 + h)),
                  pl.BlockSpec((None, t, hd), lambda b, h: (b, 0, 2 * q0 + h)),
                  pl.BlockSpec((None, t, hd), lambda b, h: (b, 0, 3 * q0 + h)),
                  pl.BlockSpec((t, hd), lambda b, h: (0, 0)),
                  pl.BlockSpec((t, hd), lambda b, h: (0, 0)),
                  pl.BlockSpec((None, 8, hd), lambda b, h: (h, 0, 0))],
        out_specs=pl.BlockSpec((None, t, hd), lambda b, h: (b, 0, h)),
        out_shape=jax.ShapeDtypeStruct((bsz, t, BRANCH_WIDTH), F32),
        scratch_shapes=[pltpu.VMEM((t, hd), BF16), pltpu.VMEM((t, hd), F32), pltpu.VMEM((t, hd), BF16),
                        pltpu.VMEM((t, hd), F32)],
        compiler_params=_cparams("arbitrary", "arbitrary"),
    )(proj, proj, proj, cos, sin, lg_tab)


def _merge_kernel(*refs, ngb):
    ys_ref, or_ref, gr_ref = refs[0:3]
    gs_refs = refs[3:3 + ngb]
    gt_refs = refs[3 + ngb:3 + 2 * ngb]
    (x_ref, mod_ref, rg_ref, n2_ref, wglu_ref, wbs_ref, wbr_ref, wout_ref, wr_ref,
     xo_ref, h2_ref, lg_ref) = refs[3 + 2 * ngb:]
    w = BRANCH_WIDTH
    tm, d = x_ref.shape
    sub = 128
    nw = d // 256
    for r0 in range(0, tm, sub):
        rows = slice(r0, r0 + sub)
        z = _gelu_tanh(ys_ref[rows, :]).astype(BF16)
        zz = jnp.dot(z, wglu_ref[...], preferred_element_type=F32)
        s5b = jnp.dot((zz[:, :w] * jax.nn.sigmoid(zz[:, w:])).astype(BF16), wbs_ref[...],
                      preferred_element_type=F32)
        o = or_ref[rows, :]
        on = jnp.concatenate([_rms(o[:, h * RET_HEAD_DIM:(h + 1) * RET_HEAD_DIM]) for h in range(RET_HEADS)],
                             axis=1)
        on = on * rg_ref[...]
        rb = jnp.dot((on * _silu(gr_ref[rows, :])).astype(BF16), wbr_ref[...], preferred_element_type=F32)
        gate_s = jax.nn.sigmoid(jnp.concatenate([r[rows, :] for r in gs_refs], axis=1))
        gate_r = jax.nn.sigmoid(jnp.concatenate([r[rows, :] for r in gt_refs], axis=1))
        mix = jnp.dot((gate_s * s5b + gate_r * rb).astype(BF16), wout_ref[...], preferred_element_type=F32)
        xn = x_ref[rows, :] + mod_ref[2:3, :] * mix
        xo_ref[rows, :] = xn
        h2 = (_rms(xn) * n2_ref[...]) * (1.0 + mod_ref[4:5, :]) + mod_ref[3:4, :]
        _store_token_rows(h2_ref.at[pl.ds(r0 * nw, sub * nw), :], h2)
        lg_ref[:, rows] = lax.dot_general(wr_ref[...], h2, (((1,), (1,)), ((), ())), preferred_element_type=F32,
                                          precision=lax.Precision.HIGHEST)


def _merge(y_s5, o_ret, proj, xs, mods, ret_norm_g, norm2_g, wglu, wbs, wbr, wout, w_router, n_ctx, tm):
    bsz, t, d = xs.shape
    w = BRANCH_WIDTH
    gw = 1024
    ngb = d // gw
    gs0 = 5 * w // gw
    ctx_tiles = n_ctx // tm
    row = lambda b, i: (b, i, 0)
    const = lambda b, i: (0, 0)
    one = pl.Buffered(1)

    def gate_spec(k):
        return pl.BlockSpec((None, tm, gw), lambda b, i: (b, i, gs0 + k))

    in_specs = ([pl.BlockSpec((None, tm, w), row), pl.BlockSpec((None, tm, w), row),
                 pl.BlockSpec((None, tm, w), lambda b, i: (b, i, 4))]
                + [gate_spec(k) for k in range(ngb)] + [gate_spec(ngb + k) for k in range(ngb)]
                + [pl.BlockSpec((None, tm, d), row),
                   pl.BlockSpec((None, N_MOD, d), lambda b, i: (jnp.where(i < ctx_tiles, bsz, b), 0, 0)),
                   pl.BlockSpec((1, w), const), pl.BlockSpec((1, d), const),
                   pl.BlockSpec((w, 2 * w), const, pipeline_mode=one),
                   pl.BlockSpec((w, d), const, pipeline_mode=one),
                   pl.BlockSpec((w, d), const, pipeline_mode=one),
                   pl.BlockSpec((d, d), const, pipeline_mode=one),
                   pl.BlockSpec((N_EXPERTS, d), const, pipeline_mode=one)])
    nt = d // 256
    tiles = t // tm
    return pl.pallas_call(
        functools.partial(_merge_kernel, ngb=ngb),
        grid=(bsz, tiles),
        in_specs=in_specs,
        out_specs=[pl.BlockSpec((None, tm, d), row),
                   pl.BlockSpec((tm * nt, 128), lambda b, i: (b * tiles + i, 0)),
                   pl.BlockSpec((N_EXPERTS, tm), lambda b, i: (0, b * tiles + i))],
        out_shape=[jax.ShapeDtypeStruct((bsz, t, d), F32), jax.ShapeDtypeStruct((bsz * t * nt, 128), U32),
                   jax.ShapeDtypeStruct((N_EXPERTS, bsz * t), F32)],
        compiler_params=_cparams("arbitrary", "arbitrary"),
    )(y_s5, o_ret, proj, *([proj] * (2 * ngb)), xs, mods, ret_norm_g.reshape(1, w), norm2_g.reshape(1, d),
      wglu, wbs, wbr, wout, w_router.T)


def _route_kernel(lg_ref, bias_ref, tw_ref, ei_ref, rk_ref, cnt_ref, carry, *, tm):
    i = pl.program_id(0)
    ne = N_EXPERTS
    per_group = ne // N_GROUPS
    neg = -jnp.inf

    @pl.when(i == 0)
    def _():
        carry[...] = jnp.zeros_like(carry)

    scores = jax.nn.sigmoid(lg_ref[...])
    biased = scores + bias_ref[...]
    sub = lax.broadcasted_iota(jnp.int32, (per_group, tm), 0)
    gscore = []
    for g in range(N_GROUPS):
        blk = biased[g * per_group:(g + 1) * per_group, :]
        m1 = jnp.max(blk, axis=0, keepdims=True)
        first = jnp.min(jnp.where(blk == m1, sub, per_group), axis=0, keepdims=True)
        m2 = jnp.max(jnp.where(sub == first, neg, blk), axis=0, keepdims=True)
        gscore.append(m1 + m2)
    masked = []
    for g in range(N_GROUPS):
        beaten = jnp.zeros((1, tm), jnp.int32)
        for j in range(N_GROUPS):
            if j != g:
                wins = (gscore[j] >= gscore[g]) if j < g else (gscore[j] > gscore[g])
                beaten = beaten + wins.astype(jnp.int32)
        keep = beaten < TOPK_GROUPS
        masked.append(jnp.where(keep, biased[g * per_group:(g + 1) * per_group, :], neg))
    mv = jnp.concatenate(masked, axis=0)
    eidx = lax.broadcasted_iota(jnp.int32, (ne, tm), 0)
    beaten = jnp.zeros((ne, tm), jnp.int32)
    for j in range(ne):
        vj = mv[j:j + 1, :]
        wins = (vj > mv) | ((vj == mv) & (j < eidx))
        beaten = beaten + wins.astype(jnp.int32)
    sel = beaten < TOP_K
    sel_w = jnp.where(sel, scores, 0.0)
    wd = sel_w / jnp.sum(sel_w, axis=0, keepdims=True) * ROUTED_SCALE
    sel_b = sel.astype(BF16)
    r_i = lax.broadcasted_iota(jnp.int32, (ne, ne), 0)
    c_i = lax.broadcasted_iota(jnp.int32, (ne, ne), 1)
    slot = jnp.dot((c_i < r_i).astype(BF16), sel_b, preferred_element_type=F32)
    t_r = lax.broadcasted_iota(jnp.int32, (tm, tm), 0)
    t_c = lax.broadcasted_iota(jnp.int32, (tm, tm), 1)
    rank = jnp.dot(sel_b, (t_r < t_c).astype(BF16), preferred_element_type=F32) + carry[:, 0:1]
    carry[...] = carry[...] + jnp.sum(sel.astype(F32), axis=1, keepdims=True)
    cnt_ref[...] = carry[...]
    eidx_f = eidx.astype(F32)
    for k in range(TOP_K):
        mk = sel & (slot == float(k))
        tw_ref[k:k + 1, :] = jnp.sum(jnp.where(mk, wd, 0.0), axis=0, keepdims=True)
        ei_ref[k:k + 1, :] = jnp.sum(jnp.where(mk, eidx_f, 0.0), axis=0, keepdims=True).astype(jnp.int32)
        rk_ref[k:k + 1, :] = jnp.sum(jnp.where(mk, rank, 0.0), axis=0, keepdims=True).astype(jnp.int32)


def _route(logits_t, b_router, tm):
    ne, n = logits_t.shape
    col = lambda i: (0, i)
    return pl.pallas_call(
        functools.partial(_route_kernel, tm=tm),
        grid=(n // tm,),
        in_specs=[pl.BlockSpec((ne, tm), col), pl.BlockSpec((ne, 1), lambda i: (0, 0))],
        out_specs=[pl.BlockSpec((TOP_K, tm), col), pl.BlockSpec((TOP_K, tm), col), pl.BlockSpec((TOP_K, tm), col),
                   pl.BlockSpec((ne, 128), lambda i: (0, 0))],
        out_shape=[jax.ShapeDtypeStruct((TOP_K, n), F32), jax.ShapeDtypeStruct((TOP_K, n), jnp.int32),
                   jax.ShapeDtypeStruct((TOP_K, n), jnp.int32), jax.ShapeDtypeStruct((ne, 128), F32)],
        scratch_shapes=[pltpu.VMEM((ne, 128), F32)],
        compiler_params=_cparams("arbitrary"),
    )(logits_t, b_router.reshape(ne, 1))


def _row_tok_kernel(lo_ref, hi_ref, pos_ref, out_ref, *, tb):
    j = pl.program_id(0)

    @pl.when(j == 0)
    def _():
        last = out_ref.shape[0] - 8

        def per_range(e, carry):
            lo = lo_ref[e]

            def fill(q, c):
                base = jnp.minimum(lo + q * 8, last)
                for r in range(8):
                    out_ref[base + r] = 0
                return c
            lax.fori_loop(0, (hi_ref[e] - lo + 7) // 8, fill, 0)
            return carry
        lax.fori_loop(0, lo_ref.shape[0], per_range, 0)

    def body(n, carry):
        for k in range(TOP_K):
            out_ref[pos_ref[k, n]] = j * tb + n
        return carry
    lax.fori_loop(0, tb, body, 0, unroll=8)


def _dispatch_plan(ei8, rk8, cnt, n_blocks, tb):
    n = ei8.shape[1]
    counts = cnt[:, 0].astype(jnp.int32)
    padded = (counts + MOE_ROWS - 1) // MOE_ROWS * MOE_ROWS
    pad_end = jnp.cumsum(padded)
    pad_start = pad_end - padded
    onehot = ei8[..., None] == jnp.arange(N_EXPERTS, dtype=jnp.int32)
    pos8 = rk8 + jnp.sum(jnp.where(onehot, pad_start, 0), axis=-1)
    blk_e = jnp.minimum(jnp.sum(jnp.arange(n_blocks, dtype=jnp.int32)[:, None] * MOE_ROWS >= pad_end[None, :],
                                axis=1), N_EXPERTS - 1).astype(jnp.int32)
    n_used = (pad_end[-1] // MOE_ROWS).astype(jnp.int32).reshape(1)
    grid_spec = pltpu.PrefetchScalarGridSpec(
        num_scalar_prefetch=2,
        grid=(n // tb,),
        in_specs=[pl.BlockSpec((TOP_K, tb), lambda j, lo, hi: (0, j), memory_space=pltpu.SMEM)],
        out_specs=pl.BlockSpec(memory_space=pltpu.SMEM))
    row_tok = pl.pallas_call(
        functools.partial(_row_tok_kernel, tb=tb),
        grid_spec=grid_spec,
        out_shape=jax.ShapeDtypeStruct((n_blocks * MOE_ROWS,), jnp.int32),
        compiler_params=_cparams("arbitrary"),
    )(jnp.concatenate([pad_start + counts, pad_end[-1:]]).astype(jnp.int32),
      jnp.concatenate([pad_end, jnp.full((1,), n_blocks * MOE_ROWS)]).astype(jnp.int32), pos8)
    return row_tok, pos8.reshape(-1), blk_e, n_used


def _expert_kernel(blk_e_ref, nused_ref, tok_ref, h_hbm, wg_ref, wu_ref, wd_ref, y_ref,
                   xbuf, sem, wgb, wub, wdb):
    bm = MOE_ROWS
    i = pl.program_id(0)
    n_used = nused_ref[0]
    slot = i % 2

    nt = xbuf.shape[1] // bm
    d = nt * 256

    def row_copy(tok, slot_, r):
        return pltpu.make_async_copy(h_hbm.at[pl.ds(pl.multiple_of(tok * nt, nt), nt), :],
                                     xbuf.at[slot_, pl.ds(pl.multiple_of(r * nt, nt), nt), :], sem.at[slot_])

    def issue(blk, slot_):
        def body(r, carry):
            row_copy(tok_ref[blk * bm + r], slot_, r).start()
            return carry
        lax.fori_loop(0, bm, body, 0, unroll=8)

    @pl.when(jnp.logical_and(i == 0, n_used > 0))
    def _():
        issue(0, 0)

    @pl.when(i + 1 < n_used)
    def _():
        issue(i + 1, 1 - slot)

    @pl.when(i < n_used)
    def _():
        def wait_body(r, carry):
            row_copy(0, slot, r).wait()
            return carry
        lax.fori_loop(0, bm, wait_body, 0, unroll=8)

        new_expert = jnp.logical_or(i == 0, blk_e_ref[i] != blk_e_ref[jnp.maximum(i - 1, 0)])

        @pl.when(new_expert)
        def _():
            wgb[...] = wg_ref[...].astype(BF16)
            wub[...] = wu_ref[...].astype(BF16)
            wdb[...] = wd_ref[...].astype(BF16)

        x = _load_token_rows(xbuf.at[slot], bm, d).astype(BF16)
        g = jnp.dot(x, wgb[...], preferred_element_type=F32)
        u = jnp.dot(x, wub[...], preferred_element_type=F32)
        _store_token_rows(y_ref, jnp.dot((_silu(g) * u).astype(BF16), wdb[...], preferred_element_type=F32))

    @pl.when(i >= n_used)
    def _():
        y_ref[...] = jnp.zeros_like(y_ref)


def _experts(h2c, row_tok, blk_e, n_used, w_gate, w_up, w_down, layer, n_blocks):
    d, de = w_gate.shape[-2:]
    nt = d // 256
    bm = MOE_ROWS
    wmap = lambda i, be, nu, tk: (layer, be[i], 0, 0)
    grid_spec = pltpu.PrefetchScalarGridSpec(
        num_scalar_prefetch=3,
        grid=(n_blocks,),
        in_specs=[pl.BlockSpec(memory_space=pl.ANY),
                  pl.BlockSpec((None, None, d, de), wmap),
                  pl.BlockSpec((None, None, d, de), wmap),
                  pl.BlockSpec((None, None, de, d), wmap)],
        out_specs=pl.BlockSpec((bm * nt, 128), lambda i, be, nu, tk: (i, 0)),
        scratch_shapes=[pltpu.VMEM((2, bm * nt, 128), U32), pltpu.SemaphoreType.DMA((2,)),
                        pltpu.VMEM((d, de), BF16), pltpu.VMEM((d, de), BF16), pltpu.VMEM((de, d), BF16)])
    return pl.pallas_call(
        _expert_kernel,
        grid_spec=grid_spec,
        out_shape=jax.ShapeDtypeStruct((n_blocks * bm * nt, 128), U32),
        compiler_params=_cparams("arbitrary"),
    )(blk_e, n_used, row_tok, h2c, w_gate, w_up, w_down)


def _combine_kernel(pos_ref, ys_hbm, tw_ref, h2_ref, x_ref, mod_ref, wsg_ref, wsu_ref, wsd_ref, fg_ref,
                    o_ref, buf, sem, *, tmc, final):
    i = pl.program_id(0)
    n_tiles = pl.num_programs(0)
    n_tok = n_tiles * tmc
    slot = i % 2
    nt = buf.shape[2] // tmc
    d = nt * 256

    def row_copy(p, slot_, k, t):
        return pltpu.make_async_copy(ys_hbm.at[pl.ds(pl.multiple_of(p * nt, nt), nt), :],
                                     buf.at[slot_, k, pl.ds(pl.multiple_of(t * nt, nt), nt), :], sem.at[slot_])

    def issue(tile, slot_):
        def body(t, carry):
            for k in range(TOP_K):
                row_copy(pos_ref[k * n_tok + tile * tmc + t], slot_, k, t).start()
            return carry
        lax.fori_loop(0, tmc, body, 0)

    @pl.when(i == 0)
    def _():
        issue(0, 0)

    @pl.when(i + 1 < n_tiles)
    def _():
        issue(i + 1, 1 - slot)

    h = _load_token_rows(h2_ref, tmc, d).astype(BF16)
    sg = jnp.dot(h, wsg_ref[...], preferred_element_type=F32)
    su = jnp.dot(h, wsu_ref[...], preferred_element_type=F32)
    ff = jnp.dot((_silu(sg) * su).astype(BF16), wsd_ref[...], preferred_element_type=F32)

    def wait_body(t, carry):
        for k in range(TOP_K):
            row_copy(0, slot, k, t).wait()
        return carry
    lax.fori_loop(0, tmc, wait_body, 0)

    tw = tw_ref[...]
    routed = tw[:, 0:1] * _load_token_rows(buf.at[slot, 0], tmc, d)
    for k in range(1, TOP_K):
        routed = routed + tw[:, k:k + 1] * _load_token_rows(buf.at[slot, k], tmc, d)
    xn = x_ref[...] + mod_ref[5:6, :] * (routed + ff)
    if final:
        xn = _rms(xn) * fg_ref[...]
    o_ref[...] = xn


def _combine(ys, pos, top_w, h2c, xs, mods, wsg, wsu, wsd, final_g, n_ctx, t_per_batch, final):
    d, de = wsg.shape
    nt = d // 256
    n = h2c.shape[0] // nt
    tmc = COMBINE_TOKENS
    bsz = n // t_per_batch
    tiles_per_batch = t_per_batch // tmc
    ctx_tiles = n_ctx // tmc
    const = lambda i, p: (0, 0)
    row = lambda i, p: (i, 0)

    def mod_map(i, p):
        return (jnp.where(i % tiles_per_batch < ctx_tiles, bsz, i // tiles_per_batch), 0, 0)

    grid_spec = pltpu.PrefetchScalarGridSpec(
        num_scalar_prefetch=1,
        grid=(n // tmc,),
        in_specs=[pl.BlockSpec(memory_space=pl.ANY),
                  pl.BlockSpec((tmc, TOP_K), row),
                  pl.BlockSpec((tmc * nt, 128), row),
                  pl.BlockSpec((tmc, d), row),
                  pl.BlockSpec((None, N_MOD, d), mod_map),
                  pl.BlockSpec((d, de), const), pl.BlockSpec((d, de), const), pl.BlockSpec((de, d), const),
                  pl.BlockSpec((1, d), const)],
        out_specs=pl.BlockSpec((tmc, d), row),
        scratch_shapes=[pltpu.VMEM((2, TOP_K, tmc * nt, 128), U32), pltpu.SemaphoreType.DMA((2,))])
    return pl.pallas_call(
        functools.partial(_combine_kernel, tmc=tmc, final=final),
        grid_spec=grid_spec,
        out_shape=jax.ShapeDtypeStruct((n, d), F32),
        compiler_params=_cparams("arbitrary"),
    )(pos, ys, top_w, h2c, xs.reshape(n, d), mods, wsg, wsu, wsd, final_g.reshape(1, d))


def _pick_tile(n, candidates):
    for c in candidates:
        if n % c == 0:
            return c
    raise ValueError(f"no tile for {n}")


def kernel(x, c, ctx, c_ctx, w_mod, b_mod, norm1_g, norm2_g, final_g, w_in, s5_lam_re, s5_lam_im, s5_log_dt, s5_b_re, s5_b_im, s5_c_re, s5_c_im, s5_d, s5_w_glu, ret_log_decay, ret_norm_g, w_br_s5, w_br_ret, w_out, moe_router, moe_router_bias, moe_w_gate, moe_w_up, moe_w_down, sh_w_gate, sh_w_up, sh_w_down):
    bsz, n_lat, d = x.shape
    n_ctx = ctx.shape[1]
    depth = w_mod.shape[0]
    t = n_ctx + n_lat
    n_tok = bsz * t
    g_n = s5_b_re.shape[1]
    assert n_ctx % RET_CHUNK == 0 and n_lat % RET_CHUNK == 0 and bsz + 1 <= 8
    assert d % 1024 == 0 and n_tok % COMBINE_TOKENS == 0 and n_ctx % COMBINE_TOKENS == 0

    cin = jnp.zeros((8, d), F32).at[:bsz].set(c).at[bsz].set(c_ctx)
    mods_all = _modulation(cin, w_mod, b_mod).reshape(depth, 8, N_MOD, d)
    cos, sin = _rope_tables(n_ctx, n_lat)
    xs = jnp.concatenate([ctx, x], axis=1)

    tm_in = _pick_tile(t, (544, 272, 256, 128))
    tm_merge = _pick_tile(n_ctx, (256, 128))
    gb = 128 // S5_GROUP
    tm_route = _pick_tile(n_tok, (256, 128))
    tb_plan = _pick_tile(n_tok, (2176, 1024, 512, 256, 128))
    n_blocks = -(-(n_tok * TOP_K + N_EXPERTS * (MOE_ROWS - 1)) // MOE_ROWS)

    for l in range(depth):
        mods = mods_all[l]
        proj = _in_proj(xs, norm1_g[l], mods, w_in[l].astype(BF16), n_ctx, tm_in)

        w1, wo, ar, ai = _s5_tables(s5_lam_re[l], s5_lam_im[l], s5_log_dt[l], s5_b_re[l], s5_b_im[l],
                                    s5_c_re[l], s5_c_im[l], s5_d[l])
        y_s5 = _s5_mix(proj, w1, wo, ar, ai, n_ctx, gb)

        lg_tab = jnp.zeros((RET_HEADS, 8, RET_HEAD_DIM), F32).at[:, 0:2, :].set(
            jnp.broadcast_to(ret_log_decay[l].T[:, :, None], (RET_HEADS, 2, RET_HEAD_DIM)))
        o_ret = _retention(proj, cos, sin, lg_tab, n_ctx)

        xs, h2c, logits_t = _merge(y_s5, o_ret, proj, xs, mods, ret_norm_g[l], norm2_g[l],
                                   s5_w_glu[l].astype(BF16), w_br_s5[l].astype(BF16), w_br_ret[l].astype(BF16),
                                   w_out[l].astype(BF16), moe_router[l], n_ctx, tm_merge)

        tw8, ei8, rk8, cnt = _route(logits_t, moe_router_bias[l], tm_route)
        row_tok, pos, blk_e, n_used = _dispatch_plan(ei8, rk8, cnt, n_blocks, tb_plan)
        ys = _experts(h2c, row_tok, blk_e, n_used, moe_w_gate, moe_w_up, moe_w_down, l, n_blocks)
        out = _combine(ys, pos, tw8.T, h2c, xs, mods, sh_w_gate[l].astype(BF16), sh_w_up[l].astype(BF16),
                       sh_w_down[l].astype(BF16), final_g, n_ctx, t, final=(l == depth - 1))
        xs = out.reshape(bsz, t, d)

    return xs[:, n_ctx:]
```

```python
import functools

import jax
import jax.numpy as jnp
from jax import lax
from jax.experimental import pallas as pl
from jax.experimental.pallas import tpu as pltpu

F32 = jnp.float32
BF16 = jnp.bfloat16

NORM_EPS = 1e-6
N_MOD = 6
GRID_W = 64
ROPE_BASE = 10000.0

S5_GROUP = 16
S5_STATE = 64
S5_CHUNK = 16
S5_TILE = S5_CHUNK * S5_GROUP
S5_LANES = 2 * S5_STATE

RET_HEADS = 8
RET_HEAD_DIM = 128
RET_CHUNK = 128
BRANCH_WIDTH = RET_HEADS * RET_HEAD_DIM

N_EXPERTS = 64
TOP_K = 8
N_GROUPS = 8
TOPK_GROUPS = 4
ROUTED_SCALE = 2.5
MOE_ROWS = 256
COMBINE_TOKENS = 128

VMEM_LIMIT = 56 * 1024 * 1024
EXPERT_VMEM_LIMIT = 61 * 1024 * 1024


def _cparams(*sem):
    return pltpu.CompilerParams(dimension_semantics=sem, vmem_limit_bytes=VMEM_LIMIT)


def _silu(x):
    return x * jax.nn.sigmoid(x)


def _gelu_tanh(x):
    return 0.5 * x * (1.0 + jnp.tanh(0.7978845608028654 * (x + 0.044715 * x * x * x)))


def _rms(x):
    return x * lax.rsqrt(jnp.mean(x * x, axis=-1, keepdims=True) + NORM_EPS)


U32 = jnp.uint32
HIGH_HALF = 0xFFFF0000


def _bf16_bits(x):
    return lax.bitcast_convert_type(x.astype(BF16).astype(F32), U32)


def _store_token_rows(ref, val):
    rows, d = val.shape
    nw = d // 256
    for j in range(nw):
        lo = _bf16_bits(val[:, j * 128:(j + 1) * 128])
        hi = _bf16_bits(val[:, (j + nw) * 128:(j + nw + 1) * 128])
        ref[pl.ds(j, rows, stride=nw), :] = (hi & U32(HIGH_HALF)) | (lo >> 16)


def _load_token_rows(ref, rows, d):
    nw = d // 256
    words = [ref[pl.ds(j, rows, stride=nw), :] for j in range(nw)]
    lo = [lax.bitcast_convert_type(w << 16, F32) for w in words]
    hi = [lax.bitcast_convert_type(w & U32(HIGH_HALF), F32) for w in words]
    return jnp.concatenate(lo + hi, axis=1)


def _mod_kernel(c_ref, w_ref, b_ref, o_ref):
    a = _silu(c_ref[...]).astype(BF16)
    o_ref[...] = jnp.dot(a, w_ref[...].astype(BF16), preferred_element_type=F32) + b_ref[...]


def _modulation(cin, w_mod, b_mod):
    depth, d, n = w_mod.shape
    tn = 1024
    return pl.pallas_call(
        _mod_kernel,
        grid=(depth, n // tn),
        in_specs=[pl.BlockSpec((8, d), lambda l, j: (0, 0)),
                  pl.BlockSpec((None, d, tn), lambda l, j: (l, 0, j)),
                  pl.BlockSpec((None, 1, tn), lambda l, j: (l, 0, j))],
        out_specs=pl.BlockSpec((None, 8, tn), lambda l, j: (l, 0, j)),
        out_shape=jax.ShapeDtypeStruct((depth, 8, n), F32),
        compiler_params=_cparams("arbitrary", "arbitrary"),
    )(cin, w_mod, b_mod.reshape(depth, 1, n))


def _in_proj_kernel(x_ref, g_ref, ml_ref, mc_ref, w_ref, o_ref, h_scr, *, n_ctx, tm):
    i = pl.program_id(1)
    j = pl.program_id(2)

    @pl.when(j == 0)
    def _():
        y = _rms(x_ref[...]) * g_ref[...]
        row = i * tm + lax.broadcasted_iota(jnp.int32, (tm, 1), 0)
        is_ctx = row < n_ctx
        shift = jnp.where(is_ctx, mc_ref[0:1, :], ml_ref[0:1, :])
        scale = jnp.where(is_ctx, mc_ref[1:2, :], ml_ref[1:2, :])
        h_scr[...] = (y * (1.0 + scale) + shift).astype(BF16)

    o_ref[...] = jnp.dot(h_scr[...], w_ref[...], preferred_element_type=F32)


def _in_proj(xs, norm_g, mods, w_in_bf16, n_ctx, tm):
    bsz, t, d = xs.shape
    n = w_in_bf16.shape[1]
    tn = 1024
    return pl.pallas_call(
        functools.partial(_in_proj_kernel, n_ctx=n_ctx, tm=tm),
        grid=(bsz, t // tm, n // tn),
        in_specs=[pl.BlockSpec((None, tm, d), lambda b, i, j: (b, i, 0)),
                  pl.BlockSpec((1, d), lambda b, i, j: (0, 0)),
                  pl.BlockSpec((None, N_MOD, d), lambda b, i, j: (b, 0, 0)),
                  pl.BlockSpec((None, N_MOD, d), lambda b, i, j: (bsz, 0, 0)),
                  pl.BlockSpec((d, tn), lambda b, i, j: (0, j))],
        out_specs=pl.BlockSpec((None, tm, tn), lambda b, i, j: (b, i, j)),
        out_shape=jax.ShapeDtypeStruct((bsz, t, n), F32),
        scratch_shapes=[pltpu.VMEM((tm, d), BF16)],
        compiler_params=_cparams("arbitrary", "arbitrary", "arbitrary"),
    )(xs, norm_g.reshape(1, d), mods, mods, w_in_bf16)


def _s5_tables(lam_re, lam_im, log_dt, b_re, b_im, c_re, c_im, d_skip):
    hp = lax.Precision.HIGHEST
    g_n, p_n, h_n = b_re.shape
    c_n = S5_CHUNK
    dt = jnp.exp(log_dt)[..., None]
    lre = jnp.minimum(lam_re, -1e-4)
    steps = jnp.arange(c_n + 1, dtype=F32)[:, None, None, None]
    mag = jnp.exp(steps * (lre * dt))
    ang = steps * (lam_im * dt)
    pr, pi = mag * jnp.cos(ang), mag * jnp.sin(ang)
    a_re, a_im = pr[1], pi[1]
    den = lre * lre + lam_im * lam_im
    nr, ni = a_re - 1.0, a_im
    f_re = (nr * lre + ni * lam_im) / den
    f_im = (ni * lre - nr * lam_im) / den
    bb_re = f_re[..., None] * b_re - f_im[..., None] * b_im
    bb_im = f_re[..., None] * b_im + f_im[..., None] * b_re
    w_re = pr[..., None] * bb_re - pi[..., None] * bb_im
    w_im = pr[..., None] * bb_im + pi[..., None] * bb_re
    kl = (jnp.einsum('dghp,ndgpk->ndgkh', c_re, w_re[:c_n], precision=hp)
          - jnp.einsum('dghp,ndgpk->ndgkh', c_im, w_im[:c_n], precision=hp))
    pad = jnp.zeros((c_n - 1,) + kl.shape[2:], F32)
    f_lags = jnp.concatenate([pad, kl[:, 0]], axis=0)
    b_lags = jnp.concatenate([kl[::-1, 1], pad], axis=0)
    kf = jnp.stack([f_lags[c_n - 1 - s:2 * c_n - 1 - s] for s in range(c_n)], axis=0)
    kb = jnp.stack([b_lags[c_n - 1 - s:2 * c_n - 1 - s] for s in range(c_n)], axis=0)
    skip = (jnp.eye(c_n, dtype=F32)[:, :, None, None, None]
            * jnp.eye(h_n, dtype=F32)[None, None, None] * d_skip[None, None, :, None, :])
    ktoep = (kf + kb + skip).transpose(2, 0, 3, 1, 4).reshape(g_n, S5_TILE, S5_TILE)

    def st(w, idx, d):
        return w[idx, d].transpose(1, 0, 3, 2).reshape(g_n, S5_TILE, p_n)

    fwd_idx = c_n - 1 - jnp.arange(c_n)
    bwd_idx = jnp.arange(c_n)
    wst = jnp.concatenate([st(w_re, fwd_idx, 0), st(w_re, bwd_idx, 1),
                           st(w_im, fwd_idx, 0), st(w_im, bwd_idx, 1)], axis=-1)

    def out_rows(d, idx):
        cr = c_re[d][None]
        ci = c_im[d][None]
        er = pr[idx, d][:, :, None, :]
        ei = pi[idx, d][:, :, None, :]
        re = (cr * er - ci * ei).transpose(1, 3, 0, 2).reshape(g_n, p_n, S5_TILE)
        im = (cr * ei + ci * er).transpose(1, 3, 0, 2).reshape(g_n, p_n, S5_TILE)
        return re, im

    fo_re, fo_im = out_rows(0, jnp.arange(c_n) + 1)
    bo_re, bo_im = out_rows(1, c_n - jnp.arange(c_n))
    wout = jnp.concatenate([fo_re, bo_re, -fo_im, -bo_im], axis=1)
    w1 = jnp.concatenate([ktoep, wst], axis=-1).astype(BF16)
    ar = jnp.concatenate([pr[c_n, 0], pr[c_n, 1]], axis=-1)
    ai = jnp.concatenate([pi[c_n, 0], pi[c_n, 1]], axis=-1)
    return w1, wout.astype(BF16), ar, ai


def _s5_kernel(u_ref, w1_ref, wo_ref, ar_ref, ai_ref, y_ref, yg, sre, sim, xfr, xfi, xbr, xbi, *, gb, nc, ncc):
    ln = S5_LANES
    c_n = S5_CHUNK
    per_tile = 128 // S5_GROUP
    lane_grp = lax.broadcasted_iota(jnp.int32, (1, 128), 1) // S5_GROUP

    def regroup(pieces, src_off, dst_offs):
        acc = None
        for piece, dst in zip(pieces, dst_offs):
            shift = ((dst - src_off) * S5_GROUP) % 128
            rolled = pltpu.roll(piece, shift, 1) if shift else piece
            acc = rolled if acc is None else jnp.where(lane_grp == dst, rolled, acc)
        return acc

    xs = [u_ref[pl.ds(s, nc, stride=c_n), :] for s in range(c_n)]
    for g in range(gb):
        tile, off = divmod(g, per_tile)
        halves = []
        for half in range(c_n // per_tile):
            pieces = [xs[half * per_tile + i][:, tile * 128:(tile + 1) * 128] for i in range(per_tile)]
            halves.append(regroup(pieces, off, range(per_tile)))
        u_g = jnp.concatenate(halves, axis=1).astype(BF16)
        r = jnp.dot(u_g, w1_ref[g], preferred_element_type=F32)
        yg[g] = r[:, :S5_TILE]
        sre[:, g * ln:(g + 1) * ln] = r[:, S5_TILE:S5_TILE + ln]
        sim[:, g * ln:(g + 1) * ln] = r[:, S5_TILE + ln:]
    are = ar_ref[...]
    aim = ai_ref[...]
    is_f = (lax.broadcasted_iota(jnp.int32, (1, gb * ln), 1) % ln) < S5_STATE

    def step(j, carry):
        xre, xim = carry
        cf = j
        cb = jnp.where(j < ncc, ncc - 1 - j, nc - 1 - (j - ncc))
        s_re = jnp.where(is_f, sre[pl.ds(cf, 1), :], sre[pl.ds(cb, 1), :])
        s_im = jnp.where(is_f, sim[pl.ds(cf, 1), :], sim[pl.ds(cb, 1), :])
        xfr[pl.ds(cf, 1), :] = xre
        xfi[pl.ds(cf, 1), :] = xim
        xbr[pl.ds(cb, 1), :] = xre
        xbi[pl.ds(cb, 1), :] = xim
        return are * xre - aim * xim + s_re, are * xim + aim * xre + s_im

    zero = jnp.zeros((1, gb * ln), F32)
    lax.fori_loop(0, nc, step, (zero, zero))

    is_f1 = lax.broadcasted_iota(jnp.int32, (1, ln), 1) < S5_STATE
    for g in range(gb):
        sl = slice(g * ln, (g + 1) * ln)
        xp = jnp.concatenate([jnp.where(is_f1, xfr[:, sl], xbr[:, sl]),
                              jnp.where(is_f1, xfi[:, sl], xbi[:, sl])], axis=1).astype(BF16)
        yg[g] += jnp.dot(xp, wo_ref[g], preferred_element_type=F32)

    for t in range(c_n):
        t_tile, t_off = divmod(t, per_tile)
        for tile in range(gb // per_tile):
            pieces = [yg[tile * per_tile + i][:, t_tile * 128:(t_tile + 1) * 128] for i in range(per_tile)]
            y_ref[pl.ds(t, nc, stride=c_n), tile * 128:(tile + 1) * 128] = regroup(pieces, t_off, range(per_tile))


def _s5_mix(proj, w1, wout, ar, ai, n_ctx, gb):
    bsz, t, _ = proj.shape
    g_n = w1.shape[0]
    nc = t // S5_CHUNK
    ncc = n_ctx // S5_CHUNK
    ln = S5_LANES
    scr = pltpu.VMEM((nc, gb * ln), F32)
    return pl.pallas_call(
        functools.partial(_s5_kernel, gb=gb, nc=nc, ncc=ncc),
        grid=(bsz, g_n // gb),
        in_specs=[pl.BlockSpec((None, t, gb * S5_GROUP), lambda b, j: (b, 0, j)),
                  pl.BlockSpec((gb, S5_TILE, 2 * S5_TILE), lambda b, j: (j, 0, 0)),
                  pl.BlockSpec((gb, S5_TILE, S5_TILE), lambda b, j: (j, 0, 0)),
                  pl.BlockSpec((None, 1, gb * ln), lambda b, j: (j, 0, 0)),
                  pl.BlockSpec((None, 1, gb * ln), lambda b, j: (j, 0, 0))],
        out_specs=pl.BlockSpec((None, t, gb * S5_GROUP), lambda b, j: (b, 0, j)),
        out_shape=jax.ShapeDtypeStruct((bsz, t, g_n * S5_GROUP), F32),
        scratch_shapes=[pltpu.VMEM((gb, nc, S5_TILE), F32), scr, scr, scr, scr, scr, scr],
        compiler_params=_cparams("arbitrary", "arbitrary"),
    )(proj, w1, wout, ar.reshape(g_n // gb, 1, gb * ln), ai.reshape(g_n // gb, 1, gb * ln))


def _rope_tables(n_ctx, n_lat):
    pos = jnp.arange(n_lat)
    rows = (pos // GRID_W).astype(F32)
    cols = (pos % GRID_W).astype(F32)
    nf = RET_HEAD_DIM // 4
    freqs = ROPE_BASE ** (-jnp.arange(nf, dtype=F32) / nf)
    ar, ac = rows[:, None] * freqs, cols[:, None] * freqs
    cos = jnp.concatenate([jnp.cos(ar), jnp.cos(ar), jnp.cos(ac), jnp.cos(ac)], axis=-1)
    sin = jnp.concatenate([-jnp.sin(ar), jnp.sin(ar), -jnp.sin(ac), jnp.sin(ac)], axis=-1)
    cos = jnp.concatenate([jnp.ones((n_ctx, RET_HEAD_DIM), F32), cos], axis=0)
    sin = jnp.concatenate([jnp.zeros((n_ctx, RET_HEAD_DIM), F32), sin], axis=0)
    return cos, sin


def _ret_kernel(q_ref, k_ref, v_ref, cos_ref, sin_ref, lg_ref, o_ref, qs, ks, vs, ob, *, nchunk, ncc):
    c_n = RET_CHUNK
    nf = RET_HEAD_DIM // 4
    lane = lax.broadcasted_iota(jnp.int32, (1, RET_HEAD_DIM), 1)
    first = (lane % (2 * nf)) < nf

    def rope(x):
        partner = jnp.where(first, pltpu.roll(x, RET_HEAD_DIM - nf, 1), pltpu.roll(x, nf, 1))
        return x * cos_ref[...] + partner * sin_ref[...]

    qs[...] = rope(q_ref[...]).astype(BF16)
    ks[...] = rope(k_ref[...]) * (RET_HEAD_DIM ** -0.5)
    vs[...] = v_ref[...].astype(BF16)

    lg = jnp.minimum(lg_ref[...], -1e-6)
    lgf = lg[0:1, :]
    lgb = lg[1:2, :]
    ri = lax.broadcasted_iota(jnp.int32, (c_n, c_n), 0)
    ci = lax.broadcasted_iota(jnp.int32, (c_n, c_n), 1)
    diff = (ri - ci).astype(F32)
    low = diff >= 0.0
    decay = jnp.where(low, jnp.exp(jnp.where(low, diff, 0.0) * lgf), jnp.exp(jnp.where(low, 0.0, -diff) * lgb))
    pos = ri.astype(F32)
    xi_f = jnp.exp((pos + 1.0) * lgf)
    zeta_f = jnp.exp((c_n - 1.0 - pos) * lgf)
    gch_f = jnp.exp(c_n * lgf)
    xi_b = jnp.exp((c_n - pos) * lgb)
    zeta_b = jnp.exp(pos * lgb)
    gch_b = jnp.exp(c_n * lgb)
    nt = (((1,), (1,)), ((), ()))

    def step(j, carry):
        rf, rb = carry
        sl = pl.ds(pl.multiple_of(j * c_n, c_n), c_n)
        qc, kf, vc = qs[sl, :], ks[sl, :], vs[sl, :]
        s = lax.dot_general(qc, kf.astype(BF16), nt, preferred_element_type=F32)
        o = jnp.dot((s * decay).astype(BF16), vc, preferred_element_type=F32)
        o_ref[sl, :] = o + jnp.dot(qc, rf.astype(BF16), preferred_element_type=F32) * xi_f
        rf = gch_f * rf + jnp.dot((kf * zeta_f).T.astype(BF16), vc, preferred_element_type=F32)

        cb = jnp.where(j < ncc, ncc - 1 - j, nchunk - 1 - (j - ncc))
        sb = pl.ds(pl.multiple_of(cb * c_n, c_n), c_n)
        qb, kb, vb = qs[sb, :], ks[sb, :], vs[sb, :]
        ob[sb, :] = jnp.dot(qb, rb.astype(BF16), preferred_element_type=F32) * xi_b
        rb = gch_b * rb + jnp.dot((kb * zeta_b).T.astype(BF16), vb, preferred_element_type=F32)
        return rf, rb

    zero = jnp.zeros((c_n, c_n), F32)
    lax.fori_loop(0, nchunk, step, (zero, zero), unroll=2)
    o_ref[...] += ob[...]


def _retention(proj, cos, sin, lg_tab, n_ctx):
    bsz, t, _ = proj.shape
    hd = RET_HEAD_DIM
    nchunk = t // RET_CHUNK
    ncc = n_ctx // RET_CHUNK
    q0 = BRANCH_WIDTH // hd
    return pl.pallas_call(
        functools.partial(_ret_kernel, nchunk=nchunk, ncc=ncc),
        grid=(bsz, RET_HEADS),
        in_specs=[pl.BlockSpec((None, t, hd), lambda b, h: (b, 0, q0 + h)),
                  pl.BlockSpec((None, t, hd), lambda b, h: (b, 0, 2 * q0 + h)),
                  pl.BlockSpec((None, t, hd), lambda b, h: (b, 0, 3 * q0 + h)),
                  pl.BlockSpec((t, hd), lambda b, h: (0, 0)),
                  pl.BlockSpec((t, hd), lambda b, h: (0, 0)),
                  pl.BlockSpec((None, 8, hd), lambda b, h: (h, 0, 0))],
        out_specs=pl.BlockSpec((None, t, hd), lambda b, h: (b, 0, h)),
        out_shape=jax.ShapeDtypeStruct((bsz, t, BRANCH_WIDTH), F32),
        scratch_shapes=[pltpu.VMEM((t, hd), BF16), pltpu.VMEM((t, hd), F32), pltpu.VMEM((t, hd), BF16),
                        pltpu.VMEM((t, hd), F32)],
        compiler_params=_cparams("arbitrary", "arbitrary"),
    )(proj, proj, proj, cos, sin, lg_tab)


def _merge_kernel(*refs, ngb):
    ys_ref, or_ref, gr_ref = refs[0:3]
    gs_refs = refs[3:3 + ngb]
    gt_refs = refs[3 + ngb:3 + 2 * ngb]
    (x_ref, mod_ref, rg_ref, n2_ref, wglu_ref, wbs_ref, wbr_ref, wout_ref, wr_ref,
     xo_ref, h2_ref, lg_ref) = refs[3 + 2 * ngb:]
    w = BRANCH_WIDTH
    tm, d = x_ref.shape
    sub = 128
    nw = d // 256
    for r0 in range(0, tm, sub):
        rows = slice(r0, r0 + sub)
        z = _gelu_tanh(ys_ref[rows, :]).astype(BF16)
        zz = jnp.dot(z, wglu_ref[...], preferred_element_type=F32)
        s5b = jnp.dot((zz[:, :w] * jax.nn.sigmoid(zz[:, w:])).astype(BF16), wbs_ref[...],
                      preferred_element_type=F32)
        o = or_ref[rows, :]
        on = jnp.concatenate([_rms(o[:, h * RET_HEAD_DIM:(h + 1) * RET_HEAD_DIM]) for h in range(RET_HEADS)],
                             axis=1)
        on = on * rg_ref[...]
        rb = jnp.dot((on * _silu(gr_ref[rows, :])).astype(BF16), wbr_ref[...], preferred_element_type=F32)
        gate_s = jax.nn.sigmoid(jnp.concatenate([r[rows, :] for r in gs_refs], axis=1))
        gate_r = jax.nn.sigmoid(jnp.concatenate([r[rows, :] for r in gt_refs], axis=1))
        mix = jnp.dot((gate_s * s5b + gate_r * rb).astype(BF16), wout_ref[...], preferred_element_type=F32)
        xn = x_ref[rows, :] + mod_ref[2:3, :] * mix
        xo_ref[rows, :] = xn
        h2 = (_rms(xn) * n2_ref[...]) * (1.0 + mod_ref[4:5, :]) + mod_ref[3:4, :]
        _store_token_rows(h2_ref.at[pl.ds(r0 * nw, sub * nw), :], h2)
        lg_ref[:, rows] = lax.dot_general(wr_ref[...], h2, (((1,), (1,)), ((), ())), preferred_element_type=F32,
                                          precision=lax.Precision.HIGHEST)


def _merge(y_s5, o_ret, proj, xs, mods, ret_norm_g, norm2_g, wglu, wbs, wbr, wout, w_router, n_ctx, tm):
    bsz, t, d = xs.shape
    w = BRANCH_WIDTH
    gw = 1024
    ngb = d // gw
    gs0 = 5 * w // gw
    ctx_tiles = n_ctx // tm
    row = lambda b, i: (b, i, 0)
    const = lambda b, i: (0, 0)
    one = pl.Buffered(1)

    def gate_spec(k):
        return pl.BlockSpec((None, tm, gw), lambda b, i: (b, i, gs0 + k))

    in_specs = ([pl.BlockSpec((None, tm, w), row), pl.BlockSpec((None, tm, w), row),
                 pl.BlockSpec((None, tm, w), lambda b, i: (b, i, 4))]
                + [gate_spec(k) for k in range(ngb)] + [gate_spec(ngb + k) for k in range(ngb)]
                + [pl.BlockSpec((None, tm, d), row),
                   pl.BlockSpec((None, N_MOD, d), lambda b, i: (jnp.where(i < ctx_tiles, bsz, b), 0, 0)),
                   pl.BlockSpec((1, w), const), pl.BlockSpec((1, d), const),
                   pl.BlockSpec((w, 2 * w), const, pipeline_mode=one),
                   pl.BlockSpec((w, d), const, pipeline_mode=one),
                   pl.BlockSpec((w, d), const, pipeline_mode=one),
                   pl.BlockSpec((d, d), const, pipeline_mode=one),
                   pl.BlockSpec((N_EXPERTS, d), const, pipeline_mode=one)])
    nt = d // 256
    tiles = t // tm
    return pl.pallas_call(
        functools.partial(_merge_kernel, ngb=ngb),
        grid=(bsz, tiles),
        in_specs=in_specs,
        out_specs=[pl.BlockSpec((None, tm, d), row),
                   pl.BlockSpec((tm * nt, 128), lambda b, i: (b * tiles + i, 0)),
                   pl.BlockSpec((N_EXPERTS, tm), lambda b, i: (0, b * tiles + i))],
        out_shape=[jax.ShapeDtypeStruct((bsz, t, d), F32), jax.ShapeDtypeStruct((bsz * t * nt, 128), U32),
                   jax.ShapeDtypeStruct((N_EXPERTS, bsz * t), F32)],
        compiler_params=_cparams("arbitrary", "arbitrary"),
    )(y_s5, o_ret, proj, *([proj] * (2 * ngb)), xs, mods, ret_norm_g.reshape(1, w), norm2_g.reshape(1, d),
      wglu, wbs, wbr, wout, w_router.T)


def _route_kernel(lg_ref, bias_ref, tw_ref, ei_ref, rk_ref, cnt_ref, carry, *, tm):
    i = pl.program_id(0)
    ne = N_EXPERTS
    per_group = ne // N_GROUPS
    neg = -jnp.inf

    @pl.when(i == 0)
    def _():
        carry[...] = jnp.zeros_like(carry)

    scores = jax.nn.sigmoid(lg_ref[...])
    biased = scores + bias_ref[...]
    sub = lax.broadcasted_iota(jnp.int32, (per_group, tm), 0)
    gscore = []
    for g in range(N_GROUPS):
        blk = biased[g * per_group:(g + 1) * per_group, :]
        m1 = jnp.max(blk, axis=0, keepdims=True)
        first = jnp.min(jnp.where(blk == m1, sub, per_group), axis=0, keepdims=True)
        m2 = jnp.max(jnp.where(sub == first, neg, blk), axis=0, keepdims=True)
        gscore.append(m1 + m2)
    masked = []
    for g in range(N_GROUPS):
        beaten = jnp.zeros((1, tm), jnp.int32)
        for j in range(N_GROUPS):
            if j != g:
                wins = (gscore[j] >= gscore[g]) if j < g else (gscore[j] > gscore[g])
                beaten = beaten + wins.astype(jnp.int32)
        keep = beaten < TOPK_GROUPS
        masked.append(jnp.where(keep, biased[g * per_group:(g + 1) * per_group, :], neg))
    mv = jnp.concatenate(masked, axis=0)
    eidx = lax.broadcasted_iota(jnp.int32, (ne, tm), 0)
    beaten = jnp.zeros((ne, tm), jnp.int32)
    for j in range(ne):
        vj = mv[j:j + 1, :]
        wins = (vj > mv) | ((vj == mv) & (j < eidx))
        beaten = beaten + wins.astype(jnp.int32)
    sel = beaten < TOP_K
    sel_w = jnp.where(sel, scores, 0.0)
    wd = sel_w / jnp.sum(sel_w, axis=0, keepdims=True) * ROUTED_SCALE
    sel_b = sel.astype(BF16)
    r_i = lax.broadcasted_iota(jnp.int32, (ne, ne), 0)
    c_i = lax.broadcasted_iota(jnp.int32, (ne, ne), 1)
    slot = jnp.dot((c_i < r_i).astype(BF16), sel_b, preferred_element_type=F32)
    t_r = lax.broadcasted_iota(jnp.int32, (tm, tm), 0)
    t_c = lax.broadcasted_iota(jnp.int32, (tm, tm), 1)
    rank = jnp.dot(sel_b, (t_r < t_c).astype(BF16), preferred_element_type=F32) + carry[:, 0:1]
    carry[...] = carry[...] + jnp.sum(sel.astype(F32), axis=1, keepdims=True)
    cnt_ref[...] = carry[...]
    eidx_f = eidx.astype(F32)
    for k in range(TOP_K):
        mk = sel & (slot == float(k))
        tw_ref[k:k + 1, :] = jnp.sum(jnp.where(mk, wd, 0.0), axis=0, keepdims=True)
        ei_ref[k:k + 1, :] = jnp.sum(jnp.where(mk, eidx_f, 0.0), axis=0, keepdims=True).astype(jnp.int32)
        rk_ref[k:k + 1, :] = jnp.sum(jnp.where(mk, rank, 0.0), axis=0, keepdims=True).astype(jnp.int32)


def _route(logits_t, b_router, tm, batch, n):
    ne = logits_t.shape[0]
    col = lambda i: (0, i)
    tiles = n // tm
    return pl.pallas_call(
        functools.partial(_route_kernel, tm=tm),
        grid=(tiles,),
        in_specs=[pl.BlockSpec((ne, tm), lambda i: (0, batch * tiles + i)), pl.BlockSpec((ne, 1), lambda i: (0, 0))],
        out_specs=[pl.BlockSpec((TOP_K, tm), col), pl.BlockSpec((TOP_K, tm), col), pl.BlockSpec((TOP_K, tm), col),
                   pl.BlockSpec((ne, 128), lambda i: (0, 0))],
        out_shape=[jax.ShapeDtypeStruct((TOP_K, n), F32), jax.ShapeDtypeStruct((TOP_K, n), jnp.int32),
                   jax.ShapeDtypeStruct((TOP_K, n), jnp.int32), jax.ShapeDtypeStruct((ne, 128), F32)],
        scratch_shapes=[pltpu.VMEM((ne, 128), F32)],
        compiler_params=_cparams("arbitrary"),
    )(logits_t, b_router.reshape(ne, 1))


def _row_tok_kernel(lo_ref, hi_ref, pos_ref, out_ref, *, tb):
    j = pl.program_id(0)

    @pl.when(j == 0)
    def _():
        last = out_ref.shape[0] - 8

        def per_range(e, carry):
            lo = lo_ref[e]

            def fill(q, c):
                base = jnp.minimum(lo + q * 8, last)
                for r in range(8):
                    out_ref[base + r] = 0
                return c
            lax.fori_loop(0, (hi_ref[e] - lo + 7) // 8, fill, 0)
            return carry
        lax.fori_loop(0, lo_ref.shape[0], per_range, 0)

    def body(n, carry):
        for k in range(TOP_K):
            out_ref[pos_ref[k, n]] = j * tb + n
        return carry
    lax.fori_loop(0, tb, body, 0, unroll=8)


def _dispatch_plan(ei8, rk8, cnt, n_blocks, tb):
    n = ei8.shape[1]
    counts = cnt[:, 0].astype(jnp.int32)
    padded = (counts + MOE_ROWS - 1) // MOE_ROWS * MOE_ROWS
    pad_end = jnp.cumsum(padded)
    pad_start = pad_end - padded
    onehot = ei8[..., None] == jnp.arange(N_EXPERTS, dtype=jnp.int32)
    pos8 = rk8 + jnp.sum(jnp.where(onehot, pad_start, 0), axis=-1)
    blk_e = jnp.minimum(jnp.sum(jnp.arange(n_blocks, dtype=jnp.int32)[:, None] * MOE_ROWS >= pad_end[None, :],
                                axis=1), N_EXPERTS - 1).astype(jnp.int32)
    n_used = (pad_end[-1] // MOE_ROWS).astype(jnp.int32).reshape(1)
    grid_spec = pltpu.PrefetchScalarGridSpec(
        num_scalar_prefetch=2,
        grid=(n // tb,),
        in_specs=[pl.BlockSpec((TOP_K, tb), lambda j, lo, hi: (0, j), memory_space=pltpu.SMEM)],
        out_specs=pl.BlockSpec(memory_space=pltpu.SMEM))
    row_tok = pl.pallas_call(
        functools.partial(_row_tok_kernel, tb=tb),
        grid_spec=grid_spec,
        out_shape=jax.ShapeDtypeStruct((n_blocks * MOE_ROWS,), jnp.int32),
        compiler_params=_cparams("arbitrary"),
    )(jnp.concatenate([pad_start + counts, pad_end[-1:]]).astype(jnp.int32),
      jnp.concatenate([pad_end, jnp.full((1,), n_blocks * MOE_ROWS)]).astype(jnp.int32), pos8)
    return row_tok, pos8.reshape(-1), blk_e, n_used


def _expert_kernel(blk_e_ref, nused_ref, tok_ref, h_ref, wg_ref, wu_ref, wd_ref, y_ref, xg, wgb, wub, wdb):
    bm = MOE_ROWS
    i = pl.program_id(0)
    n_used = nused_ref[0]
    nt = xg.shape[0] // bm
    d = nt * 256

    @pl.when(i < n_used)
    def _():
        def gather(r, carry):
            tok = tok_ref[i * bm + r]
            xg[pl.ds(pl.multiple_of(r * nt, nt), nt), :] = h_ref[pl.ds(pl.multiple_of(tok * nt, nt), nt), :]
            return carry
        lax.fori_loop(0, bm, gather, 0, unroll=8)

        new_expert = jnp.logical_or(i == 0, blk_e_ref[i] != blk_e_ref[jnp.maximum(i - 1, 0)])

        @pl.when(new_expert)
        def _():
            wgb[...] = wg_ref[...].astype(BF16)
            wub[...] = wu_ref[...].astype(BF16)
            wdb[...] = wd_ref[...].astype(BF16)

        x = _load_token_rows(xg, bm, d).astype(BF16)
        g = jnp.dot(x, wgb[...], preferred_element_type=F32)
        u = jnp.dot(x, wub[...], preferred_element_type=F32)
        _store_token_rows(y_ref, jnp.dot((_silu(g) * u).astype(BF16), wdb[...], preferred_element_type=F32))

    @pl.when(i >= n_used)
    def _():
        y_ref[...] = jnp.zeros_like(y_ref)


def _experts(h2c, row_tok, blk_e, n_used, w_gate, w_up, w_down, layer, n_blocks, batch, n):
    d, de = w_gate.shape[-2:]
    nt = d // 256
    bm = MOE_ROWS
    wmap = lambda i, be, nu, tk: (layer, be[i], 0, 0)
    grid_spec = pltpu.PrefetchScalarGridSpec(
        num_scalar_prefetch=3,
        grid=(n_blocks,),
        in_specs=[pl.BlockSpec((n * nt, 128), lambda i, be, nu, tk: (batch, 0), pipeline_mode=pl.Buffered(1)),
                  pl.BlockSpec((None, None, d, de), wmap),
                  pl.BlockSpec((None, None, d, de), wmap),
                  pl.BlockSpec((None, None, de, d), wmap)],
        out_specs=pl.BlockSpec((bm * nt, 128), lambda i, be, nu, tk: (i, 0)),
        scratch_shapes=[pltpu.VMEM((bm * nt, 128), U32),
                        pltpu.VMEM((d, de), BF16), pltpu.VMEM((d, de), BF16), pltpu.VMEM((de, d), BF16)])
    return pl.pallas_call(
        _expert_kernel,
        grid_spec=grid_spec,
        out_shape=jax.ShapeDtypeStruct((n_blocks * bm * nt, 128), U32),
        compiler_params=pltpu.CompilerParams(dimension_semantics=("arbitrary",),
                                             vmem_limit_bytes=EXPERT_VMEM_LIMIT),
    )(blk_e, n_used, row_tok, h2c, w_gate, w_up, w_down)


def _combine_kernel(pos_ref, ys_hbm, tw_ref, h2_ref, x_ref, mod_ref, wsg_ref, wsu_ref, wsd_ref, fg_ref,
                    o_ref, buf, sem, *, tmc, final):
    i = pl.program_id(0)
    n_tiles = pl.num_programs(0)
    n_tok = n_tiles * tmc
    slot = i % 2
    nt = buf.shape[2] // tmc
    d = nt * 256

    def row_copy(p, slot_, k, t):
        return pltpu.make_async_copy(ys_hbm.at[pl.ds(pl.multiple_of(p * nt, nt), nt), :],
                                     buf.at[slot_, k, pl.ds(pl.multiple_of(t * nt, nt), nt), :], sem.at[slot_])

    def issue(tile, slot_):
        def body(t, carry):
            for k in range(TOP_K):
                row_copy(pos_ref[k * n_tok + tile * tmc + t], slot_, k, t).start(priority=k % 2)
            return carry
        lax.fori_loop(0, tmc, body, 0)

    @pl.when(i == 0)
    def _():
        issue(0, 0)

    @pl.when(i + 1 < n_tiles)
    def _():
        issue(i + 1, 1 - slot)

    h = _load_token_rows(h2_ref, tmc, d).astype(BF16)
    sg = jnp.dot(h, wsg_ref[...], preferred_element_type=F32)
    su = jnp.dot(h, wsu_ref[...], preferred_element_type=F32)
    ff = jnp.dot((_silu(sg) * su).astype(BF16), wsd_ref[...], preferred_element_type=F32)

    def wait_body(t, carry):
        for k in range(TOP_K):
            row_copy(0, slot, k, t).wait()
        return carry
    lax.fori_loop(0, tmc, wait_body, 0)

    tw = tw_ref[...]
    routed = tw[:, 0:1] * _load_token_rows(buf.at[slot, 0], tmc, d)
    for k in range(1, TOP_K):
        routed = routed + tw[:, k:k + 1] * _load_token_rows(buf.at[slot, k], tmc, d)
    xn = x_ref[...] + mod_ref[5:6, :] * (routed + ff)
    if final:
        xn = _rms(xn) * fg_ref[...]
    o_ref[...] = xn


def _combine(ys, pos, top_w, h2c, xs_flat, mods, wsg, wsu, wsd, final_g, n_ctx, batch, n, bsz, final):
    d, de = wsg.shape
    nt = d // 256
    tmc = COMBINE_TOKENS
    tiles = n // tmc
    ctx_tiles = n_ctx // tmc
    const = lambda i, p: (0, 0)
    row = lambda i, p: (batch * tiles + i, 0)

    grid_spec = pltpu.PrefetchScalarGridSpec(
        num_scalar_prefetch=1,
        grid=(tiles,),
        in_specs=[pl.BlockSpec(memory_space=pl.ANY),
                  pl.BlockSpec((tmc, TOP_K), lambda i, p: (i, 0)),
                  pl.BlockSpec((tmc * nt, 128), row),
                  pl.BlockSpec((tmc, d), row),
                  pl.BlockSpec((None, N_MOD, d), lambda i, p: (jnp.where(i < ctx_tiles, bsz, batch), 0, 0)),
                  pl.BlockSpec((d, de), const), pl.BlockSpec((d, de), const), pl.BlockSpec((de, d), const),
                  pl.BlockSpec((1, d), const)],
        out_specs=pl.BlockSpec((tmc, d), row),
        scratch_shapes=[pltpu.VMEM((2, TOP_K, tmc * nt, 128), U32), pltpu.SemaphoreType.DMA((2,))])
    return pl.pallas_call(
        functools.partial(_combine_kernel, tmc=tmc, final=final),
        grid_spec=grid_spec,
        out_shape=jax.ShapeDtypeStruct(xs_flat.shape, F32),
        input_output_aliases={4: 0},
        compiler_params=_cparams("arbitrary"),
    )(pos, ys, top_w, h2c, xs_flat, mods, wsg, wsu, wsd, final_g.reshape(1, d))


def _pick_tile(n, candidates):
    for c in candidates:
        if n % c == 0:
            return c
    raise ValueError(f"no tile for {n}")


def kernel(x, c, ctx, c_ctx, w_mod, b_mod, norm1_g, norm2_g, final_g, w_in, s5_lam_re, s5_lam_im, s5_log_dt, s5_b_re, s5_b_im, s5_c_re, s5_c_im, s5_d, s5_w_glu, ret_log_decay, ret_norm_g, w_br_s5, w_br_ret, w_out, moe_router, moe_router_bias, moe_w_gate, moe_w_up, moe_w_down, sh_w_gate, sh_w_up, sh_w_down):
    bsz, n_lat, d = x.shape
    n_ctx = ctx.shape[1]
    depth = w_mod.shape[0]
    t = n_ctx + n_lat
    n_tok = bsz * t
    g_n = s5_b_re.shape[1]
    assert n_ctx % RET_CHUNK == 0 and n_lat % RET_CHUNK == 0 and bsz + 1 <= 8
    assert d % 1024 == 0 and n_tok % COMBINE_TOKENS == 0 and n_ctx % COMBINE_TOKENS == 0

    cin = jnp.zeros((8, d), F32).at[:bsz].set(c).at[bsz].set(c_ctx)
    mods_all = _modulation(cin, w_mod, b_mod).reshape(depth, 8, N_MOD, d)
    cos, sin = _rope_tables(n_ctx, n_lat)
    xs = jnp.concatenate([ctx, x], axis=1)

    tm_in = _pick_tile(t, (544, 272, 256, 128))
    tm_merge = _pick_tile(n_ctx, (256, 128))
    gb = 128 // S5_GROUP
    tm_route = _pick_tile(t, (256, 128))
    tb_plan = _pick_tile(t, (2176, 1024, 512, 256, 128))
    n_blocks = -(-(t * TOP_K + N_EXPERTS * (MOE_ROWS - 1)) // MOE_ROWS)

    for l in range(depth):
        mods = mods_all[l]
        proj = _in_proj(xs, norm1_g[l], mods, w_in[l].astype(BF16), n_ctx, tm_in)

        w1, wo, ar, ai = _s5_tables(s5_lam_re[l], s5_lam_im[l], s5_log_dt[l], s5_b_re[l], s5_b_im[l],
                                    s5_c_re[l], s5_c_im[l], s5_d[l])
        y_s5 = _s5_mix(proj, w1, wo, ar, ai, n_ctx, gb)

        lg_tab = jnp.zeros((RET_HEADS, 8, RET_HEAD_DIM), F32).at[:, 0:2, :].set(
            jnp.broadcast_to(ret_log_decay[l].T[:, :, None], (RET_HEADS, 2, RET_HEAD_DIM)))
        o_ret = _retention(proj, cos, sin, lg_tab, n_ctx)

        xs, h2c, logits_t = _merge(y_s5, o_ret, proj, xs, mods, ret_norm_g[l], norm2_g[l],
                                   s5_w_glu[l].astype(BF16), w_br_s5[l].astype(BF16), w_br_ret[l].astype(BF16),
                                   w_out[l].astype(BF16), moe_router[l], n_ctx, tm_merge)

        wsg, wsu, wsd = sh_w_gate[l].astype(BF16), sh_w_up[l].astype(BF16), sh_w_down[l].astype(BF16)
        xs_flat = xs.reshape(n_tok, d)
        for b in range(bsz):
            tw8, ei8, rk8, cnt = _route(logits_t, moe_router_bias[l], tm_route, b, t)
            row_tok, pos, blk_e, n_used = _dispatch_plan(ei8, rk8, cnt, n_blocks, tb_plan)
            ys = _experts(h2c, row_tok, blk_e, n_used, moe_w_gate, moe_w_up, moe_w_down, l, n_blocks, b, t)
            xs_flat = _combine(ys, pos, tw8.T, h2c, xs_flat, mods, wsg, wsu, wsd, final_g, n_ctx, b, t, bsz,
                               final=(l == depth - 1))
        xs = xs_flat.reshape(bsz, t, d)

    return xs[:, n_ctx:]
```

```python
import functools

import jax
import jax.numpy as jnp
from jax import lax
from jax.experimental import pallas as pl
from jax.experimental.pallas import tpu as pltpu

F32 = jnp.float32
BF16 = jnp.bfloat16

NORM_EPS = 1e-6
N_MOD = 6
GRID_W = 64
ROPE_BASE = 10000.0

S5_GROUP = 16
S5_STATE = 64
S5_CHUNK = 16
S5_TILE = S5_CHUNK * S5_GROUP
S5_LANES = 2 * S5_STATE

RET_HEADS = 8
RET_HEAD_DIM = 128
RET_CHUNK = 128
BRANCH_WIDTH = RET_HEADS * RET_HEAD_DIM

N_EXPERTS = 64
TOP_K = 8
N_GROUPS = 8
TOPK_GROUPS = 4
ROUTED_SCALE = 2.5
MOE_ROWS = 256
COMBINE_TOKENS = 128

VMEM_LIMIT = 56 * 1024 * 1024
EXPERT_VMEM_LIMIT = 61 * 1024 * 1024


def _cparams(*sem):
    return pltpu.CompilerParams(dimension_semantics=sem, vmem_limit_bytes=VMEM_LIMIT)


def _silu(x):
    return x * jax.nn.sigmoid(x)


def _gelu_tanh(x):
    return 0.5 * x * (1.0 + jnp.tanh(0.7978845608028654 * (x + 0.044715 * x * x * x)))


def _rms(x):
    return x * lax.rsqrt(jnp.mean(x * x, axis=-1, keepdims=True) + NORM_EPS)


U32 = jnp.uint32
HIGH_HALF = 0xFFFF0000


def _bf16_bits(x):
    return lax.bitcast_convert_type(x.astype(BF16).astype(F32), U32)


def _store_token_rows(ref, val):
    rows, d = val.shape
    nw = d // 256
    for j in range(nw):
        lo = _bf16_bits(val[:, j * 128:(j + 1) * 128])
        hi = _bf16_bits(val[:, (j + nw) * 128:(j + nw + 1) * 128])
        ref[pl.ds(j, rows, stride=nw), :] = (hi & U32(HIGH_HALF)) | (lo >> 16)


def _load_token_rows(ref, rows, d):
    nw = d // 256
    words = [ref[pl.ds(j, rows, stride=nw), :] for j in range(nw)]
    lo = [lax.bitcast_convert_type(w << 16, F32) for w in words]
    hi = [lax.bitcast_convert_type(w & U32(HIGH_HALF), F32) for w in words]
    return jnp.concatenate(lo + hi, axis=1)


def _mod_kernel(c_ref, w_ref, b_ref, o_ref):
    a = _silu(c_ref[...]).astype(BF16)
    o_ref[...] = jnp.dot(a, w_ref[...].astype(BF16), preferred_element_type=F32) + b_ref[...]


def _modulation(cin, w_mod, b_mod):
    depth, d, n = w_mod.shape
    tn = 1024
    return pl.pallas_call(
        _mod_kernel,
        grid=(depth, n // tn),
        in_specs=[pl.BlockSpec((8, d), lambda l, j: (0, 0)),
                  pl.BlockSpec((None, d, tn), lambda l, j: (l, 0, j)),
                  pl.BlockSpec((None, 1, tn), lambda l, j: (l, 0, j))],
        out_specs=pl.BlockSpec((None, 8, tn), lambda l, j: (l, 0, j)),
        out_shape=jax.ShapeDtypeStruct((depth, 8, n), F32),
        compiler_params=_cparams("arbitrary", "arbitrary"),
    )(cin, w_mod, b_mod.reshape(depth, 1, n))


def _in_proj_kernel(x_ref, g_ref, ml_ref, mc_ref, w_ref, o_ref, h_scr, *, n_ctx, tm):
    i = pl.program_id(1)
    j = pl.program_id(2)

    @pl.when(j == 0)
    def _():
        y = _rms(x_ref[...]) * g_ref[...]
        row = i * tm + lax.broadcasted_iota(jnp.int32, (tm, 1), 0)
        is_ctx = row < n_ctx
        shift = jnp.where(is_ctx, mc_ref[0:1, :], ml_ref[0:1, :])
        scale = jnp.where(is_ctx, mc_ref[1:2, :], ml_ref[1:2, :])
        h_scr[...] = (y * (1.0 + scale) + shift).astype(BF16)

    o_ref[...] = jnp.dot(h_scr[...], w_ref[...], preferred_element_type=F32)


def _in_proj(xs, norm_g, mods, w_in_bf16, n_ctx, tm):
    bsz, t, d = xs.shape
    n = w_in_bf16.shape[1]
    tn = 1024
    return pl.pallas_call(
        functools.partial(_in_proj_kernel, n_ctx=n_ctx, tm=tm),
        grid=(bsz, t // tm, n // tn),
        in_specs=[pl.BlockSpec((None, tm, d), lambda b, i, j: (b, i, 0)),
                  pl.BlockSpec((1, d), lambda b, i, j: (0, 0)),
                  pl.BlockSpec((None, N_MOD, d), lambda b, i, j: (b, 0, 0)),
                  pl.BlockSpec((None, N_MOD, d), lambda b, i, j: (bsz, 0, 0)),
                  pl.BlockSpec((d, tn), lambda b, i, j: (0, j))],
        out_specs=pl.BlockSpec((None, tm, tn), lambda b, i, j: (b, i, j)),
        out_shape=jax.ShapeDtypeStruct((bsz, t, n), F32),
        scratch_shapes=[pltpu.VMEM((tm, d), BF16)],
        compiler_params=_cparams("arbitrary", "arbitrary", "arbitrary"),
    )(xs, norm_g.reshape(1, d), mods, mods, w_in_bf16)


def _s5_tables(lam_re, lam_im, log_dt, b_re, b_im, c_re, c_im, d_skip):
    hp = lax.Precision.HIGHEST
    g_n, p_n, h_n = b_re.shape
    c_n = S5_CHUNK
    dt = jnp.exp(log_dt)[..., None]
    lre = jnp.minimum(lam_re, -1e-4)
    steps = jnp.arange(c_n + 1, dtype=F32)[:, None, None, None]
    mag = jnp.exp(steps * (lre * dt))
    ang = steps * (lam_im * dt)
    pr, pi = mag * jnp.cos(ang), mag * jnp.sin(ang)
    a_re, a_im = pr[1], pi[1]
    den = lre * lre + lam_im * lam_im
    nr, ni = a_re - 1.0, a_im
    f_re = (nr * lre + ni * lam_im) / den
    f_im = (ni * lre - nr * lam_im) / den
    bb_re = f_re[..., None] * b_re - f_im[..., None] * b_im
    bb_im = f_re[..., None] * b_im + f_im[..., None] * b_re
    w_re = pr[..., None] * bb_re - pi[..., None] * bb_im
    w_im = pr[..., None] * bb_im + pi[..., None] * bb_re
    kl = (jnp.einsum('dghp,ndgpk->ndgkh', c_re, w_re[:c_n], precision=hp)
          - jnp.einsum('dghp,ndgpk->ndgkh', c_im, w_im[:c_n], precision=hp))
    pad = jnp.zeros((c_n - 1,) + kl.shape[2:], F32)
    f_lags = jnp.concatenate([pad, kl[:, 0]], axis=0)
    b_lags = jnp.concatenate([kl[::-1, 1], pad], axis=0)
    kf = jnp.stack([f_lags[c_n - 1 - s:2 * c_n - 1 - s] for s in range(c_n)], axis=0)
    kb = jnp.stack([b_lags[c_n - 1 - s:2 * c_n - 1 - s] for s in range(c_n)], axis=0)
    skip = (jnp.eye(c_n, dtype=F32)[:, :, None, None, None]
            * jnp.eye(h_n, dtype=F32)[None, None, None] * d_skip[None, None, :, None, :])
    ktoep = (kf + kb + skip).transpose(2, 0, 3, 1, 4).reshape(g_n, S5_TILE, S5_TILE)

    def st(w, idx, d):
        return w[idx, d].transpose(1, 0, 3, 2).reshape(g_n, S5_TILE, p_n)

    fwd_idx = c_n - 1 - jnp.arange(c_n)
    bwd_idx = jnp.arange(c_n)
    wst = jnp.concatenate([st(w_re, fwd_idx, 0), st(w_re, bwd_idx, 1),
                           st(w_im, fwd_idx, 0), st(w_im, bwd_idx, 1)], axis=-1)

    def out_rows(d, idx):
        cr = c_re[d][None]
        ci = c_im[d][None]
        er = pr[idx, d][:, :, None, :]
        ei = pi[idx, d][:, :, None, :]
        re = (cr * er - ci * ei).transpose(1, 3, 0, 2).reshape(g_n, p_n, S5_TILE)
        im = (cr * ei + ci * er).transpose(1, 3, 0, 2).reshape(g_n, p_n, S5_TILE)
        return re, im

    fo_re, fo_im = out_rows(0, jnp.arange(c_n) + 1)
    bo_re, bo_im = out_rows(1, c_n - jnp.arange(c_n))
    wout = jnp.concatenate([fo_re, bo_re, -fo_im, -bo_im], axis=1)
    w1 = jnp.concatenate([ktoep, wst], axis=-1).astype(BF16)
    ar = jnp.concatenate([pr[c_n, 0], pr[c_n, 1]], axis=-1)
    ai = jnp.concatenate([pi[c_n, 0], pi[c_n, 1]], axis=-1)
    return w1, wout.astype(BF16), ar, ai


def _s5_kernel(u_ref, w1_ref, wo_ref, ar_ref, ai_ref, y_ref, yg, sre, sim, xfr, xfi, xbr, xbi, *, gb, nc, ncc):
    ln = S5_LANES
    c_n = S5_CHUNK
    per_tile = 128 // S5_GROUP
    lane_grp = lax.broadcasted_iota(jnp.int32, (1, 128), 1) // S5_GROUP

    def regroup(pieces, src_off, dst_offs):
        acc = None
        for piece, dst in zip(pieces, dst_offs):
            shift = ((dst - src_off) * S5_GROUP) % 128
            rolled = pltpu.roll(piece, shift, 1) if shift else piece
            acc = rolled if acc is None else jnp.where(lane_grp == dst, rolled, acc)
        return acc

    xs = [u_ref[pl.ds(s, nc, stride=c_n), :] for s in range(c_n)]
    for g in range(gb):
        tile, off = divmod(g, per_tile)
        halves = []
        for half in range(c_n // per_tile):
            pieces = [xs[half * per_tile + i][:, tile * 128:(tile + 1) * 128] for i in range(per_tile)]
            halves.append(regroup(pieces, off, range(per_tile)))
        u_g = jnp.concatenate(halves, axis=1).astype(BF16)
        r = jnp.dot(u_g, w1_ref[g], preferred_element_type=F32)
        yg[g] = r[:, :S5_TILE]
        sre[:, g * ln:(g + 1) * ln] = r[:, S5_TILE:S5_TILE + ln]
        sim[:, g * ln:(g + 1) * ln] = r[:, S5_TILE + ln:]
    are = ar_ref[...]
    aim = ai_ref[...]
    is_f = (lax.broadcasted_iota(jnp.int32, (1, gb * ln), 1) % ln) < S5_STATE

    def step(j, carry):
        xre, xim = carry
        cf = j
        cb = jnp.where(j < ncc, ncc - 1 - j, nc - 1 - (j - ncc))
        s_re = jnp.where(is_f, sre[pl.ds(cf, 1), :], sre[pl.ds(cb, 1), :])
        s_im = jnp.where(is_f, sim[pl.ds(cf, 1), :], sim[pl.ds(cb, 1), :])
        xfr[pl.ds(cf, 1), :] = xre
        xfi[pl.ds(cf, 1), :] = xim
        xbr[pl.ds(cb, 1), :] = xre
        xbi[pl.ds(cb, 1), :] = xim
        return are * xre - aim * xim + s_re, are * xim + aim * xre + s_im

    zero = jnp.zeros((1, gb * ln), F32)
    lax.fori_loop(0, nc, step, (zero, zero))

    is_f1 = lax.broadcasted_iota(jnp.int32, (1, ln), 1) < S5_STATE
    for g in range(gb):
        sl = slice(g * ln, (g + 1) * ln)
        xp = jnp.concatenate([jnp.where(is_f1, xfr[:, sl], xbr[:, sl]),
                              jnp.where(is_f1, xfi[:, sl], xbi[:, sl])], axis=1).astype(BF16)
        yg[g] += jnp.dot(xp, wo_ref[g], preferred_element_type=F32)

    for t in range(c_n):
        t_tile, t_off = divmod(t, per_tile)
        for tile in range(gb // per_tile):
            pieces = [yg[tile * per_tile + i][:, t_tile * 128:(t_tile + 1) * 128] for i in range(per_tile)]
            y_ref[pl.ds(t, nc, stride=c_n), tile * 128:(tile + 1) * 128] = regroup(pieces, t_off, range(per_tile))


def _s5_mix(proj, w1, wout, ar, ai, n_ctx, gb):
    bsz, t, _ = proj.shape
    g_n = w1.shape[0]
    nc = t // S5_CHUNK
    ncc = n_ctx // S5_CHUNK
    ln = S5_LANES
    scr = pltpu.VMEM((nc, gb * ln), F32)
    return pl.pallas_call(
        functools.partial(_s5_kernel, gb=gb, nc=nc, ncc=ncc),
        grid=(bsz, g_n // gb),
        in_specs=[pl.BlockSpec((None, t, gb * S5_GROUP), lambda b, j: (b, 0, j)),
                  pl.BlockSpec((gb, S5_TILE, 2 * S5_TILE), lambda b, j: (j, 0, 0)),
                  pl.BlockSpec((gb, S5_TILE, S5_TILE), lambda b, j: (j, 0, 0)),
                  pl.BlockSpec((None, 1, gb * ln), lambda b, j: (j, 0, 0)),
                  pl.BlockSpec((None, 1, gb * ln), lambda b, j: (j, 0, 0))],
        out_specs=pl.BlockSpec((None, t, gb * S5_GROUP), lambda b, j: (b, 0, j)),
        out_shape=jax.ShapeDtypeStruct((bsz, t, g_n * S5_GROUP), F32),
        scratch_shapes=[pltpu.VMEM((gb, nc, S5_TILE), F32), scr, scr, scr, scr, scr, scr],
        compiler_params=_cparams("arbitrary", "arbitrary"),
    )(proj, w1, wout, ar.reshape(g_n // gb, 1, gb * ln), ai.reshape(g_n // gb, 1, gb * ln))


def _rope_tables(n_ctx, n_lat):
    pos = jnp.arange(n_lat)
    rows = (pos // GRID_W).astype(F32)
    cols = (pos % GRID_W).astype(F32)
    nf = RET_HEAD_DIM // 4
    freqs = ROPE_BASE ** (-jnp.arange(nf, dtype=F32) / nf)
    ar, ac = rows[:, None] * freqs, cols[:, None] * freqs
    cos = jnp.concatenate([jnp.cos(ar), jnp.cos(ar), jnp.cos(ac), jnp.cos(ac)], axis=-1)
    sin = jnp.concatenate([-jnp.sin(ar), jnp.sin(ar), -jnp.sin(ac), jnp.sin(ac)], axis=-1)
    cos = jnp.concatenate([jnp.ones((n_ctx, RET_HEAD_DIM), F32), cos], axis=0)
    sin = jnp.concatenate([jnp.zeros((n_ctx, RET_HEAD_DIM), F32), sin], axis=0)
    return cos, sin


def _ret_kernel(q_ref, k_ref, v_ref, cos_ref, sin_ref, lg_ref, o_ref, qs, ks, vs, ob, *, nchunk, ncc):
    c_n = RET_CHUNK
    nf = RET_HEAD_DIM // 4
    lane = lax.broadcasted_iota(jnp.int32, (1, RET_HEAD_DIM), 1)
    first = (lane % (2 * nf)) < nf

    def rope(x):
        partner = jnp.where(first, pltpu.roll(x, RET_HEAD_DIM - nf, 1), pltpu.roll(x, nf, 1))
        return x * cos_ref[...] + partner * sin_ref[...]

    qs[...] = rope(q_ref[...]).astype(BF16)
    ks[...] = rope(k_ref[...]) * (RET_HEAD_DIM ** -0.5)
    vs[...] = v_ref[...].astype(BF16)

    lg = jnp.minimum(lg_ref[...], -1e-6)
    lgf = lg[0:1, :]
    lgb = lg[1:2, :]
    ri = lax.broadcasted_iota(jnp.int32, (c_n, c_n), 0)
    ci = lax.broadcasted_iota(jnp.int32, (c_n, c_n), 1)
    diff = (ri - ci).astype(F32)
    low = diff >= 0.0
    decay = jnp.where(low, jnp.exp(jnp.where(low, diff, 0.0) * lgf), jnp.exp(jnp.where(low, 0.0, -diff) * lgb))
    pos = ri.astype(F32)
    xi_f = jnp.exp((pos + 1.0) * lgf)
    zeta_f = jnp.exp((c_n - 1.0 - pos) * lgf)
    gch_f = jnp.exp(c_n * lgf)
    xi_b = jnp.exp((c_n - pos) * lgb)
    zeta_b = jnp.exp(pos * lgb)
    gch_b = jnp.exp(c_n * lgb)
    nt = (((1,), (1,)), ((), ()))

    def step(j, carry):
        rf, rb = carry
        sl = pl.ds(pl.multiple_of(j * c_n, c_n), c_n)
        qc, kf, vc = qs[sl, :], ks[sl, :], vs[sl, :]
        s = lax.dot_general(qc, kf.astype(BF16), nt, preferred_element_type=F32)
        o = jnp.dot((s * decay).astype(BF16), vc, preferred_element_type=F32)
        o_ref[sl, :] = o + jnp.dot(qc, rf.astype(BF16), preferred_element_type=F32) * xi_f
        rf = gch_f * rf + jnp.dot((kf * zeta_f).T.astype(BF16), vc, preferred_element_type=F32)

        cb = jnp.where(j < ncc, ncc - 1 - j, nchunk - 1 - (j - ncc))
        sb = pl.ds(pl.multiple_of(cb * c_n, c_n), c_n)
        qb, kb, vb = qs[sb, :], ks[sb, :], vs[sb, :]
        ob[sb, :] = jnp.dot(qb, rb.astype(BF16), preferred_element_type=F32) * xi_b
        rb = gch_b * rb + jnp.dot((kb * zeta_b).T.astype(BF16), vb, preferred_element_type=F32)
        return rf, rb

    zero = jnp.zeros((c_n, c_n), F32)
    lax.fori_loop(0, nchunk, step, (zero, zero), unroll=2)
    o_ref[...] += ob[...]


def _retention(proj, cos, sin, lg_tab, n_ctx):
    bsz, t, _ = proj.shape
    hd = RET_HEAD_DIM
    nchunk = t // RET_CHUNK
    ncc = n_ctx // RET_CHUNK
    q0 = BRANCH_WIDTH // hd
    return pl.pallas_call(
        functools.partial(_ret_kernel, nchunk=nchunk, ncc=ncc),
        grid=(bsz, RET_HEADS),
        in_specs=[pl.BlockSpec((None, t, hd), lambda b, h: (b, 0, q0 + h)),
                  pl.BlockSpec((None, t, hd), lambda b, h: (b, 0, 2 * q0 + h)),
                  pl.BlockSpec((None, t, hd), lambda b, h: (b, 0, 3 * q0 + h)),
                  pl.BlockSpec((t, hd), lambda b, h: (0, 0)),
                  pl.BlockSpec((t, hd), lambda b, h: (0, 0)),
                  pl.BlockSpec((None, 8, hd), lambda b, h: (h, 0, 0))],
        out_specs=pl.BlockSpec((None, t, hd), lambda b, h: (b, 0, h)),
        out_shape=jax.ShapeDtypeStruct((bsz, t, BRANCH_WIDTH), F32),
        scratch_shapes=[pltpu.VMEM((t, hd), BF16), pltpu.VMEM((t, hd), F32), pltpu.VMEM((t, hd), BF16),
                        pltpu.VMEM((t, hd), F32)],
        compiler_params=_cparams("arbitrary", "arbitrary"),
    )(proj, proj, proj, cos, sin, lg_tab)


def _merge_kernel(*refs, ngb):
    ys_ref, or_ref, gr_ref = refs[0:3]
    gs_refs = refs[3:3 + ngb]
    gt_refs = refs[3 + ngb:3 + 2 * ngb]
    (x_ref, mod_ref, rg_ref, n2_ref, wglu_ref, wbs_ref, wbr_ref, wout_ref, wr_ref,
     xo_ref, h2_ref, lg_ref) = refs[3 + 2 * ngb:]
    w = BRANCH_WIDTH
    tm, d = x_ref.shape
    sub = 128
    nw = d // 256
    for r0 in range(0, tm, sub):
        rows = slice(r0, r0 + sub)
        z = _gelu_tanh(ys_ref[rows, :]).astype(BF16)
        zz = jnp.dot(z, wglu_ref[...], preferred_element_type=F32)
        s5b = jnp.dot((zz[:, :w] * jax.nn.sigmoid(zz[:, w:])).astype(BF16), wbs_ref[...],
                      preferred_element_type=F32)
        o = or_ref[rows, :]
        on = jnp.concatenate([_rms(o[:, h * RET_HEAD_DIM:(h + 1) * RET_HEAD_DIM]) for h in range(RET_HEADS)],
                             axis=1)
        on = on * rg_ref[...]
        rb = jnp.dot((on * _silu(gr_ref[rows, :])).astype(BF16), wbr_ref[...], preferred_element_type=F32)
        gate_s = jax.nn.sigmoid(jnp.concatenate([r[rows, :] for r in gs_refs], axis=1))
        gate_r = jax.nn.sigmoid(jnp.concatenate([r[rows, :] for r in gt_refs], axis=1))
        mix = jnp.dot((gate_s * s5b + gate_r * rb).astype(BF16), wout_ref[...], preferred_element_type=F32)
        xn = x_ref[rows, :] + mod_ref[2:3, :] * mix
        xo_ref[rows, :] = xn
        h2 = (_rms(xn) * n2_ref[...]) * (1.0 + mod_ref[4:5, :]) + mod_ref[3:4, :]
        _store_token_rows(h2_ref.at[pl.ds(r0 * nw, sub * nw), :], h2)
        lg_ref[:, rows] = lax.dot_general(wr_ref[...], h2, (((1,), (1,)), ((), ())), preferred_element_type=F32,
                                          precision=lax.Precision.HIGHEST)


def _merge(y_s5, o_ret, proj, xs, mods, ret_norm_g, norm2_g, wglu, wbs, wbr, wout, w_router, n_ctx, tm):
    bsz, t, d = xs.shape
    w = BRANCH_WIDTH
    gw = 1024
    ngb = d // gw
    gs0 = 5 * w // gw
    ctx_tiles = n_ctx // tm
    row = lambda b, i: (b, i, 0)
    const = lambda b, i: (0, 0)
    one = pl.Buffered(1)

    def gate_spec(k):
        return pl.BlockSpec((None, tm, gw), lambda b, i: (b, i, gs0 + k))

    in_specs = ([pl.BlockSpec((None, tm, w), row), pl.BlockSpec((None, tm, w), row),
                 pl.BlockSpec((None, tm, w), lambda b, i: (b, i, 4))]
                + [gate_spec(k) for k in range(ngb)] + [gate_spec(ngb + k) for k in range(ngb)]
                + [pl.BlockSpec((None, tm, d), row),
                   pl.BlockSpec((None, N_MOD, d), lambda b, i: (jnp.where(i < ctx_tiles, bsz, b), 0, 0)),
                   pl.BlockSpec((1, w), const), pl.BlockSpec((1, d), const),
                   pl.BlockSpec((w, 2 * w), const, pipeline_mode=one),
                   pl.BlockSpec((w, d), const, pipeline_mode=one),
                   pl.BlockSpec((w, d), const, pipeline_mode=one),
                   pl.BlockSpec((d, d), const, pipeline_mode=one),
                   pl.BlockSpec((N_EXPERTS, d), const, pipeline_mode=one)])
    nt = d // 256
    tiles = t // tm
    return pl.pallas_call(
        functools.partial(_merge_kernel, ngb=ngb),
        grid=(bsz, tiles),
        in_specs=in_specs,
        out_specs=[pl.BlockSpec((None, tm, d), row),
                   pl.BlockSpec((tm * nt, 128), lambda b, i: (b * tiles + i, 0)),
                   pl.BlockSpec((N_EXPERTS, tm), lambda b, i: (0, b * tiles + i))],
        out_shape=[jax.ShapeDtypeStruct((bsz, t, d), F32), jax.ShapeDtypeStruct((bsz * t * nt, 128), U32),
                   jax.ShapeDtypeStruct((N_EXPERTS, bsz * t), F32)],
        compiler_params=_cparams("arbitrary", "arbitrary"),
    )(y_s5, o_ret, proj, *([proj] * (2 * ngb)), xs, mods, ret_norm_g.reshape(1, w), norm2_g.reshape(1, d),
      wglu, wbs, wbr, wout, w_router.T)


def _route_kernel(lg_ref, bias_ref, tw_ref, ei_ref, rk_ref, cnt_ref, carry, *, tm):
    i = pl.program_id(0)
    ne = N_EXPERTS
    per_group = ne // N_GROUPS
    neg = -jnp.inf

    @pl.when(i == 0)
    def _():
        carry[...] = jnp.zeros_like(carry)

    scores = jax.nn.sigmoid(lg_ref[...])
    biased = scores + bias_ref[...]
    sub = lax.broadcasted_iota(jnp.int32, (per_group, tm), 0)
    gscore = []
    for g in range(N_GROUPS):
        blk = biased[g * per_group:(g + 1) * per_group, :]
        m1 = jnp.max(blk, axis=0, keepdims=True)
        first = jnp.min(jnp.where(blk == m1, sub, per_group), axis=0, keepdims=True)
        m2 = jnp.max(jnp.where(sub == first, neg, blk), axis=0, keepdims=True)
        gscore.append(m1 + m2)
    masked = []
    for g in range(N_GROUPS):
        beaten = jnp.zeros((1, tm), jnp.int32)
        for j in range(N_GROUPS):
            if j != g:
                wins = (gscore[j] >= gscore[g]) if j < g else (gscore[j] > gscore[g])
                beaten = beaten + wins.astype(jnp.int32)
        keep = beaten < TOPK_GROUPS
        masked.append(jnp.where(keep, biased[g * per_group:(g + 1) * per_group, :], neg))
    mv = jnp.concatenate(masked, axis=0)
    eidx = lax.broadcasted_iota(jnp.int32, (ne, tm), 0)
    beaten = jnp.zeros((ne, tm), jnp.int32)
    for j in range(ne):
        vj = mv[j:j + 1, :]
        wins = (vj > mv) | ((vj == mv) & (j < eidx))
        beaten = beaten + wins.astype(jnp.int32)
    sel = beaten < TOP_K
    sel_w = jnp.where(sel, scores, 0.0)
    wd = sel_w / jnp.sum(sel_w, axis=0, keepdims=True) * ROUTED_SCALE
    sel_b = sel.astype(BF16)
    r_i = lax.broadcasted_iota(jnp.int32, (ne, ne), 0)
    c_i = lax.broadcasted_iota(jnp.int32, (ne, ne), 1)
    slot = jnp.dot((c_i < r_i).astype(BF16), sel_b, preferred_element_type=F32)
    t_r = lax.broadcasted_iota(jnp.int32, (tm, tm), 0)
    t_c = lax.broadcasted_iota(jnp.int32, (tm, tm), 1)
    rank = jnp.dot(sel_b, (t_r < t_c).astype(BF16), preferred_element_type=F32) + carry[:, 0:1]
    carry[...] = carry[...] + jnp.sum(sel.astype(F32), axis=1, keepdims=True)
    cnt_ref[...] = carry[...]
    eidx_f = eidx.astype(F32)
    for k in range(TOP_K):
        mk = sel & (slot == float(k))
        tw_ref[k:k + 1, :] = jnp.sum(jnp.where(mk, wd, 0.0), axis=0, keepdims=True)
        ei_ref[k:k + 1, :] = jnp.sum(jnp.where(mk, eidx_f, 0.0), axis=0, keepdims=True).astype(jnp.int32)
        rk_ref[k:k + 1, :] = jnp.sum(jnp.where(mk, rank, 0.0), axis=0, keepdims=True).astype(jnp.int32)


def _route(logits_t, b_router, tm, batch, n):
    ne = logits_t.shape[0]
    col = lambda i: (0, i)
    tiles = n // tm
    return pl.pallas_call(
        functools.partial(_route_kernel, tm=tm),
        grid=(tiles,),
        in_specs=[pl.BlockSpec((ne, tm), lambda i: (0, batch * tiles + i)), pl.BlockSpec((ne, 1), lambda i: (0, 0))],
        out_specs=[pl.BlockSpec((TOP_K, tm), col), pl.BlockSpec((TOP_K, tm), col), pl.BlockSpec((TOP_K, tm), col),
                   pl.BlockSpec((ne, 128), lambda i: (0, 0))],
        out_shape=[jax.ShapeDtypeStruct((TOP_K, n), F32), jax.ShapeDtypeStruct((TOP_K, n), jnp.int32),
                   jax.ShapeDtypeStruct((TOP_K, n), jnp.int32), jax.ShapeDtypeStruct((ne, 128), F32)],
        scratch_shapes=[pltpu.VMEM((ne, 128), F32)],
        compiler_params=_cparams("arbitrary"),
    )(logits_t, b_router.reshape(ne, 1))


def _row_tok_kernel(lo_ref, hi_ref, pos_ref, out_ref, *, tb):
    j = pl.program_id(0)

    @pl.when(j == 0)
    def _():
        last = out_ref.shape[0] - 8

        def per_range(e, carry):
            lo = lo_ref[e]

            def fill(q, c):
                base = jnp.minimum(lo + q * 8, last)
                for r in range(8):
                    out_ref[base + r] = 0
                return c
            lax.fori_loop(0, (hi_ref[e] - lo + 7) // 8, fill, 0)
            return carry
        lax.fori_loop(0, lo_ref.shape[0], per_range, 0)

    def body(n, carry):
        for k in range(TOP_K):
            out_ref[pos_ref[k, n]] = j * tb + n
        return carry
    lax.fori_loop(0, tb, body, 0, unroll=8)


def _dispatch_plan(ei8, rk8, cnt, n_blocks, tb):
    n = ei8.shape[1]
    counts = cnt[:, 0].astype(jnp.int32)
    padded = (counts + MOE_ROWS - 1) // MOE_ROWS * MOE_ROWS
    pad_end = jnp.cumsum(padded)
    pad_start = pad_end - padded
    onehot = ei8[..., None] == jnp.arange(N_EXPERTS, dtype=jnp.int32)
    pos8 = rk8 + jnp.sum(jnp.where(onehot, pad_start, 0), axis=-1)
    blk_e = jnp.minimum(jnp.sum(jnp.arange(n_blocks, dtype=jnp.int32)[:, None] * MOE_ROWS >= pad_end[None, :],
                                axis=1), N_EXPERTS - 1).astype(jnp.int32)
    n_used = (pad_end[-1] // MOE_ROWS).astype(jnp.int32).reshape(1)
    run_end = jnp.sum(jnp.where(blk_e[:, None] == jnp.arange(N_EXPERTS, dtype=jnp.int32), pad_end // MOE_ROWS, 0),
                      axis=1)
    nxt_e = jnp.where(run_end < n_used[0], blk_e[jnp.minimum(run_end, n_blocks - 1)], -1).astype(jnp.int32)
    grid_spec = pltpu.PrefetchScalarGridSpec(
        num_scalar_prefetch=2,
        grid=(n // tb,),
        in_specs=[pl.BlockSpec((TOP_K, tb), lambda j, lo, hi: (0, j), memory_space=pltpu.SMEM)],
        out_specs=pl.BlockSpec(memory_space=pltpu.SMEM))
    row_tok = pl.pallas_call(
        functools.partial(_row_tok_kernel, tb=tb),
        grid_spec=grid_spec,
        out_shape=jax.ShapeDtypeStruct((n_blocks * MOE_ROWS,), jnp.int32),
        compiler_params=_cparams("arbitrary"),
    )(jnp.concatenate([pad_start + counts, pad_end[-1:]]).astype(jnp.int32),
      jnp.concatenate([pad_end, jnp.full((1,), n_blocks * MOE_ROWS)]).astype(jnp.int32), pos8)
    return row_tok, pos8.reshape(-1), blk_e, nxt_e, n_used


def _expert_kernel(blk_e_ref, nxt_e_ref, nused_ref, tok_ref, h_ref, wg_hbm, wu_hbm, wd_hbm, y_ref,
                   xg, stage_g, stage_u, stage_d, sem, wgb, wub, wdb, *, layer):
    bm = MOE_ROWS
    i = pl.program_id(0)
    n_used = nused_ref[0]
    nt = xg.shape[0] // bm
    d = nt * 256

    def weight_copies(e):
        return (pltpu.make_async_copy(wg_hbm.at[layer, e], stage_g, sem.at[0]),
                pltpu.make_async_copy(wu_hbm.at[layer, e], stage_u, sem.at[1]),
                pltpu.make_async_copy(wd_hbm.at[layer, e], stage_d, sem.at[2]))

    @pl.when(i < n_used)
    def _():
        e = blk_e_ref[i]

        @pl.when(i == 0)
        def _():
            for cp in weight_copies(e):
                cp.start()

        def gather(r, carry):
            tok = tok_ref[i * bm + r]
            xg[pl.ds(pl.multiple_of(r * nt, nt), nt), :] = h_ref[pl.ds(pl.multiple_of(tok * nt, nt), nt), :]
            return carry
        lax.fori_loop(0, bm, gather, 0, unroll=8)

        new_expert = jnp.logical_or(i == 0, e != blk_e_ref[jnp.maximum(i - 1, 0)])

        @pl.when(new_expert)
        def _():
            for cp in weight_copies(e):
                cp.wait()
            wgb[...] = stage_g[...].astype(BF16)
            wub[...] = stage_u[...].astype(BF16)
            wdb[...] = stage_d[...].astype(BF16)
            nxt = nxt_e_ref[i]

            @pl.when(nxt >= 0)
            def _():
                for cp in weight_copies(nxt):
                    cp.start()

        x = _load_token_rows(xg, bm, d).astype(BF16)
        g = jnp.dot(x, wgb[...], preferred_element_type=F32)
        u = jnp.dot(x, wub[...], preferred_element_type=F32)
        _store_token_rows(y_ref, jnp.dot((_silu(g) * u).astype(BF16), wdb[...], preferred_element_type=F32))

    @pl.when(i >= n_used)
    def _():
        y_ref[...] = jnp.zeros_like(y_ref)


def _experts(h2c, row_tok, blk_e, nxt_e, n_used, w_gate, w_up, w_down, layer, n_blocks):
    d, de = w_gate.shape[-2:]
    nt = d // 256
    bm = MOE_ROWS
    hbm = pl.BlockSpec(memory_space=pl.ANY)
    grid_spec = pltpu.PrefetchScalarGridSpec(
        num_scalar_prefetch=4,
        grid=(n_blocks,),
        in_specs=[pl.BlockSpec(h2c.shape, lambda i, *_: (0, 0), pipeline_mode=pl.Buffered(1)), hbm, hbm, hbm],
        out_specs=pl.BlockSpec((bm * nt, 128), lambda i, *_: (i, 0)),
        scratch_shapes=[pltpu.VMEM((bm * nt, 128), U32),
                        pltpu.VMEM((d, de), F32), pltpu.VMEM((d, de), F32), pltpu.VMEM((de, d), F32),
                        pltpu.SemaphoreType.DMA((3,)),
                        pltpu.VMEM((d, de), BF16), pltpu.VMEM((d, de), BF16), pltpu.VMEM((de, d), BF16)])
    return pl.pallas_call(
        functools.partial(_expert_kernel, layer=layer),
        grid_spec=grid_spec,
        out_shape=jax.ShapeDtypeStruct((n_blocks * bm * nt, 128), U32),
        compiler_params=pltpu.CompilerParams(dimension_semantics=("arbitrary",),
                                             vmem_limit_bytes=EXPERT_VMEM_LIMIT),
    )(blk_e, nxt_e, n_used, row_tok, h2c, w_gate, w_up, w_down)


def _combine_kernel(pos_ref, ys_hbm, tw_ref, h2_ref, x_ref, mod_ref, wsg_ref, wsu_ref, wsd_ref, fg_ref,
                    o_ref, buf, sem, *, tmc, final):
    i = pl.program_id(0)
    n_tiles = pl.num_programs(0)
    n_tok = n_tiles * tmc
    slot = i % 2
    nt = buf.shape[2] // tmc
    d = nt * 256

    def row_copy(p, slot_, k, t):
        return pltpu.make_async_copy(ys_hbm.at[pl.ds(pl.multiple_of(p * nt, nt), nt), :],
                                     buf.at[slot_, k, pl.ds(pl.multiple_of(t * nt, nt), nt), :], sem.at[slot_])

    def issue(tile, slot_):
        def body(t, carry):
            for k in range(TOP_K):
                row_copy(pos_ref[k * n_tok + tile * tmc + t], slot_, k, t).start(priority=k % 2)
            return carry
        lax.fori_loop(0, tmc, body, 0)

    @pl.when(i == 0)
    def _():
        issue(0, 0)

    @pl.when(i + 1 < n_tiles)
    def _():
        issue(i + 1, 1 - slot)

    h = _load_token_rows(h2_ref, tmc, d).astype(BF16)
    sg = jnp.dot(h, wsg_ref[...], preferred_element_type=F32)
    su = jnp.dot(h, wsu_ref[...], preferred_element_type=F32)
    ff = jnp.dot((_silu(sg) * su).astype(BF16), wsd_ref[...], preferred_element_type=F32)

    def wait_body(t, carry):
        for k in range(TOP_K):
            row_copy(0, slot, k, t).wait()
        return carry
    lax.fori_loop(0, tmc, wait_body, 0)

    tw = tw_ref[...]
    routed = tw[:, 0:1] * _load_token_rows(buf.at[slot, 0], tmc, d)
    for k in range(1, TOP_K):
        routed = routed + tw[:, k:k + 1] * _load_token_rows(buf.at[slot, k], tmc, d)
    xn = x_ref[...] + mod_ref[5:6, :] * (routed + ff)
    if final:
        xn = _rms(xn) * fg_ref[...]
    o_ref[...] = xn


def _combine(ys, pos, top_w, h2c, xs_flat, mods, wsg, wsu, wsd, final_g, n_ctx, t_per_batch, final):
    d, de = wsg.shape
    nt = d // 256
    n = xs_flat.shape[0]
    tmc = COMBINE_TOKENS
    tiles = n // tmc
    bsz = n // t_per_batch
    tiles_per_batch = t_per_batch // tmc
    ctx_tiles = n_ctx // tmc
    const = lambda i, p: (0, 0)
    row = lambda i, p: (i, 0)

    def mod_map(i, p):
        return (jnp.where(i % tiles_per_batch < ctx_tiles, bsz, i // tiles_per_batch), 0, 0)

    grid_spec = pltpu.PrefetchScalarGridSpec(
        num_scalar_prefetch=1,
        grid=(tiles,),
        in_specs=[pl.BlockSpec(memory_space=pl.ANY),
                  pl.BlockSpec((tmc, TOP_K), row),
                  pl.BlockSpec((tmc * nt, 128), row),
                  pl.BlockSpec((tmc, d), row),
                  pl.BlockSpec((None, N_MOD, d), mod_map),
                  pl.BlockSpec((d, de), const), pl.BlockSpec((d, de), const), pl.BlockSpec((de, d), const),
                  pl.BlockSpec((1, d), const)],
        out_specs=pl.BlockSpec((tmc, d), row),
        scratch_shapes=[pltpu.VMEM((2, TOP_K, tmc * nt, 128), U32), pltpu.SemaphoreType.DMA((2,))])
    return pl.pallas_call(
        functools.partial(_combine_kernel, tmc=tmc, final=final),
        grid_spec=grid_spec,
        out_shape=jax.ShapeDtypeStruct(xs_flat.shape, F32),
        input_output_aliases={4: 0},
        compiler_params=_cparams("arbitrary"),
    )(pos, ys, top_w, h2c, xs_flat, mods, wsg, wsu, wsd, final_g.reshape(1, d))


def _pick_tile(n, candidates):
    for c in candidates:
        if n % c == 0:
            return c
    raise ValueError(f"no tile for {n}")


def kernel(x, c, ctx, c_ctx, w_mod, b_mod, norm1_g, norm2_g, final_g, w_in, s5_lam_re, s5_lam_im, s5_log_dt, s5_b_re, s5_b_im, s5_c_re, s5_c_im, s5_d, s5_w_glu, ret_log_decay, ret_norm_g, w_br_s5, w_br_ret, w_out, moe_router, moe_router_bias, moe_w_gate, moe_w_up, moe_w_down, sh_w_gate, sh_w_up, sh_w_down):
    bsz, n_lat, d = x.shape
    n_ctx = ctx.shape[1]
    depth = w_mod.shape[0]
    t = n_ctx + n_lat
    n_tok = bsz * t
    g_n = s5_b_re.shape[1]
    assert n_ctx % RET_CHUNK == 0 and n_lat % RET_CHUNK == 0 and bsz + 1 <= 8
    assert d % 1024 == 0 and n_tok % COMBINE_TOKENS == 0 and n_ctx % COMBINE_TOKENS == 0

    cin = jnp.zeros((8, d), F32).at[:bsz].set(c).at[bsz].set(c_ctx)
    mods_all = _modulation(cin, w_mod, b_mod).reshape(depth, 8, N_MOD, d)
    cos, sin = _rope_tables(n_ctx, n_lat)
    xs = jnp.concatenate([ctx, x], axis=1)

    tm_in = _pick_tile(t, (544, 272, 256, 128))
    tm_merge = _pick_tile(n_ctx, (256, 128))
    gb = 128 // S5_GROUP
    tm_route = _pick_tile(n_tok, (256, 128))
    tb_plan = _pick_tile(n_tok, (2176, 1024, 512, 256, 128))
    n_blocks = -(-(n_tok * TOP_K + N_EXPERTS * (MOE_ROWS - 1)) // MOE_ROWS)

    for l in range(depth):
        mods = mods_all[l]
        proj = _in_proj(xs, norm1_g[l], mods, w_in[l].astype(BF16), n_ctx, tm_in)

        w1, wo, ar, ai = _s5_tables(s5_lam_re[l], s5_lam_im[l], s5_log_dt[l], s5_b_re[l], s5_b_im[l],
                                    s5_c_re[l], s5_c_im[l], s5_d[l])
        y_s5 = _s5_mix(proj, w1, wo, ar, ai, n_ctx, gb)

        lg_tab = jnp.zeros((RET_HEADS, 8, RET_HEAD_DIM), F32).at[:, 0:2, :].set(
            jnp.broadcast_to(ret_log_decay[l].T[:, :, None], (RET_HEADS, 2, RET_HEAD_DIM)))
        o_ret = _retention(proj, cos, sin, lg_tab, n_ctx)

        xs, h2c, logits_t = _merge(y_s5, o_ret, proj, xs, mods, ret_norm_g[l], norm2_g[l],
                                   s5_w_glu[l].astype(BF16), w_br_s5[l].astype(BF16), w_br_ret[l].astype(BF16),
                                   w_out[l].astype(BF16), moe_router[l], n_ctx, tm_merge)

        tw8, ei8, rk8, cnt = _route(logits_t, moe_router_bias[l], tm_route, 0, n_tok)
        row_tok, pos, blk_e, nxt_e, n_used = _dispatch_plan(ei8, rk8, cnt, n_blocks, tb_plan)
        ys = _experts(h2c, row_tok, blk_e, nxt_e, n_used, moe_w_gate, moe_w_up, moe_w_down, l, n_blocks)
        xs = _combine(ys, pos, tw8.T, h2c, xs.reshape(n_tok, d), mods, sh_w_gate[l].astype(BF16),
                      sh_w_up[l].astype(BF16), sh_w_down[l].astype(BF16), final_g, n_ctx, t,
                      final=(l == depth - 1)).reshape(bsz, t, d)

    return xs[:, n_ctx:]
```

```python
import functools

import jax
import jax.numpy as jnp
from jax import lax
from jax.experimental import pallas as pl
from jax.experimental.pallas import tpu as pltpu

F32 = jnp.float32
BF16 = jnp.bfloat16

NORM_EPS = 1e-6
N_MOD = 6
GRID_W = 64
ROPE_BASE = 10000.0

S5_GROUP = 16
S5_STATE = 64
S5_CHUNK = 16
S5_TILE = S5_CHUNK * S5_GROUP
S5_LANES = 2 * S5_STATE

RET_HEADS = 8
RET_HEAD_DIM = 128
RET_CHUNK = 128
BRANCH_WIDTH = RET_HEADS * RET_HEAD_DIM

N_EXPERTS = 64
TOP_K = 8
N_GROUPS = 8
TOPK_GROUPS = 4
ROUTED_SCALE = 2.5
MOE_ROWS = 256
COMBINE_TOKENS = 128

VMEM_LIMIT = 56 * 1024 * 1024
EXPERT_VMEM_LIMIT = 61 * 1024 * 1024


def _cparams(*sem):
    return pltpu.CompilerParams(dimension_semantics=sem, vmem_limit_bytes=VMEM_LIMIT)


def _silu(x):
    return x * jax.nn.sigmoid(x)


def _gelu_tanh(x):
    return 0.5 * x * (1.0 + jnp.tanh(0.7978845608028654 * (x + 0.044715 * x * x * x)))


def _rms(x):
    return x * lax.rsqrt(jnp.mean(x * x, axis=-1, keepdims=True) + NORM_EPS)


U32 = jnp.uint32
HIGH_HALF = 0xFFFF0000


def _bf16_bits(x):
    return lax.bitcast_convert_type(x.astype(BF16).astype(F32), U32)


def _store_token_rows(ref, val):
    rows, d = val.shape
    nw = d // 256
    for j in range(nw):
        lo = _bf16_bits(val[:, j * 128:(j + 1) * 128])
        hi = _bf16_bits(val[:, (j + nw) * 128:(j + nw + 1) * 128])
        ref[pl.ds(j, rows, stride=nw), :] = (hi & U32(HIGH_HALF)) | (lo >> 16)


def _load_token_rows(ref, rows, d):
    nw = d // 256
    words = [ref[pl.ds(j, rows, stride=nw), :] for j in range(nw)]
    lo = [lax.bitcast_convert_type(w << 16, F32) for w in words]
    hi = [lax.bitcast_convert_type(w & U32(HIGH_HALF), F32) for w in words]
    return jnp.concatenate(lo + hi, axis=1)


def _mod_kernel(c_ref, w_ref, b_ref, o_ref):
    a = _silu(c_ref[...]).astype(BF16)
    o_ref[...] = jnp.dot(a, w_ref[...].astype(BF16), preferred_element_type=F32) + b_ref[...]


def _modulation(cin, w_mod, b_mod):
    depth, d, n = w_mod.shape
    tn = 1024
    return pl.pallas_call(
        _mod_kernel,
        grid=(depth, n // tn),
        in_specs=[pl.BlockSpec((8, d), lambda l, j: (0, 0)),
                  pl.BlockSpec((None, d, tn), lambda l, j: (l, 0, j)),
                  pl.BlockSpec((None, 1, tn), lambda l, j: (l, 0, j))],
        out_specs=pl.BlockSpec((None, 8, tn), lambda l, j: (l, 0, j)),
        out_shape=jax.ShapeDtypeStruct((depth, 8, n), F32),
        compiler_params=_cparams("arbitrary", "arbitrary"),
    )(cin, w_mod, b_mod.reshape(depth, 1, n))


def _in_proj_kernel(x_ref, g_ref, ml_ref, mc_ref, w_ref, o_ref, h_scr, *, n_ctx, tm):
    i = pl.program_id(1)
    j = pl.program_id(2)

    @pl.when(j == 0)
    def _():
        y = _rms(x_ref[...]) * g_ref[...]
        row = i * tm + lax.broadcasted_iota(jnp.int32, (tm, 1), 0)
        is_ctx = row < n_ctx
        shift = jnp.where(is_ctx, mc_ref[0:1, :], ml_ref[0:1, :])
        scale = jnp.where(is_ctx, mc_ref[1:2, :], ml_ref[1:2, :])
        h_scr[...] = (y * (1.0 + scale) + shift).astype(BF16)

    o_ref[...] = jnp.dot(h_scr[...], w_ref[...], preferred_element_type=F32)


def _in_proj(xs, norm_g, mods, w_in_bf16, n_ctx, tm):
    bsz, t, d = xs.shape
    n = w_in_bf16.shape[1]
    tn = 1024
    return pl.pallas_call(
        functools.partial(_in_proj_kernel, n_ctx=n_ctx, tm=tm),
        grid=(bsz, t // tm, n // tn),
        in_specs=[pl.BlockSpec((None, tm, d), lambda b, i, j: (b, i, 0)),
                  pl.BlockSpec((1, d), lambda b, i, j: (0, 0)),
                  pl.BlockSpec((None, N_MOD, d), lambda b, i, j: (b, 0, 0)),
                  pl.BlockSpec((None, N_MOD, d), lambda b, i, j: (bsz, 0, 0)),
                  pl.BlockSpec((d, tn), lambda b, i, j: (0, j))],
        out_specs=pl.BlockSpec((None, tm, tn), lambda b, i, j: (b, i, j)),
        out_shape=jax.ShapeDtypeStruct((bsz, t, n), F32),
        scratch_shapes=[pltpu.VMEM((tm, d), BF16)],
        compiler_params=_cparams("arbitrary", "arbitrary", "arbitrary"),
    )(xs, norm_g.reshape(1, d), mods, mods, w_in_bf16)


def _s5_tables(lam_re, lam_im, log_dt, b_re, b_im, c_re, c_im, d_skip):
    hp = lax.Precision.HIGHEST
    g_n, p_n, h_n = b_re.shape
    c_n = S5_CHUNK
    dt = jnp.exp(log_dt)[..., None]
    lre = jnp.minimum(lam_re, -1e-4)
    steps = jnp.arange(c_n + 1, dtype=F32)[:, None, None, None]
    mag = jnp.exp(steps * (lre * dt))
    ang = steps * (lam_im * dt)
    pr, pi = mag * jnp.cos(ang), mag * jnp.sin(ang)
    a_re, a_im = pr[1], pi[1]
    den = lre * lre + lam_im * lam_im
    nr, ni = a_re - 1.0, a_im
    f_re = (nr * lre + ni * lam_im) / den
    f_im = (ni * lre - nr * lam_im) / den
    bb_re = f_re[..., None] * b_re - f_im[..., None] * b_im
    bb_im = f_re[..., None] * b_im + f_im[..., None] * b_re
    w_re = pr[..., None] * bb_re - pi[..., None] * bb_im
    w_im = pr[..., None] * bb_im + pi[..., None] * bb_re
    kl = (jnp.einsum('dghp,ndgpk->ndgkh', c_re, w_re[:c_n], precision=hp)
          - jnp.einsum('dghp,ndgpk->ndgkh', c_im, w_im[:c_n], precision=hp))
    s_i = jnp.arange(c_n)[:, None]
    t_i = jnp.arange(c_n)[None, :]
    kf = jnp.where((t_i >= s_i)[..., None, None, None], kl[:, 0][jnp.clip(t_i - s_i, 0, c_n - 1)], 0.0)
    kb = jnp.where((s_i >= t_i)[..., None, None, None], kl[:, 1][jnp.clip(s_i - t_i, 0, c_n - 1)], 0.0)
    skip = (jnp.eye(c_n, dtype=F32)[:, :, None, None, None]
            * jnp.eye(h_n, dtype=F32)[None, None, None] * d_skip[None, None, :, None, :])
    ktoep = (kf + kb + skip).transpose(2, 0, 3, 1, 4).reshape(g_n, S5_TILE, S5_TILE)

    def st(w, idx, d):
        return w[idx, d].transpose(1, 0, 3, 2).reshape(g_n, S5_TILE, p_n)

    fwd_idx = c_n - 1 - jnp.arange(c_n)
    bwd_idx = jnp.arange(c_n)
    wst = jnp.concatenate([st(w_re, fwd_idx, 0), st(w_re, bwd_idx, 1),
                           st(w_im, fwd_idx, 0), st(w_im, bwd_idx, 1)], axis=-1)

    def out_rows(d, idx):
        cr = c_re[d][None]
        ci = c_im[d][None]
        er = pr[idx, d][:, :, None, :]
        ei = pi[idx, d][:, :, None, :]
        re = (cr * er - ci * ei).transpose(1, 3, 0, 2).reshape(g_n, p_n, S5_TILE)
        im = (cr * ei + ci * er).transpose(1, 3, 0, 2).reshape(g_n, p_n, S5_TILE)
        return re, im

    fo_re, fo_im = out_rows(0, jnp.arange(c_n) + 1)
    bo_re, bo_im = out_rows(1, c_n - jnp.arange(c_n))
    wout = jnp.concatenate([fo_re, bo_re, -fo_im, -bo_im], axis=1)
    w1 = jnp.concatenate([ktoep, wst], axis=-1).astype(BF16)
    ar = jnp.concatenate([pr[c_n, 0], pr[c_n, 1]], axis=-1)
    ai = jnp.concatenate([pi[c_n, 0], pi[c_n, 1]], axis=-1)
    return w1, wout.astype(BF16), ar, ai


def _s5_kernel(u_ref, w1_ref, wo_ref, ar_ref, ai_ref, y_ref, yg, sre, sim, xfr, xfi, xbr, xbi, *, gb, nc, ncc):
    ln = S5_LANES
    c_n = S5_CHUNK
    per_tile = 128 // S5_GROUP
    lane_grp = lax.broadcasted_iota(jnp.int32, (1, 128), 1) // S5_GROUP

    def regroup(pieces, src_off, dst_offs):
        acc = None
        for piece, dst in zip(pieces, dst_offs):
            shift = ((dst - src_off) * S5_GROUP) % 128
            rolled = pltpu.roll(piece, shift, 1) if shift else piece
            acc = rolled if acc is None else jnp.where(lane_grp == dst, rolled, acc)
        return acc

    xs = [u_ref[pl.ds(s, nc, stride=c_n), :] for s in range(c_n)]
    for g in range(gb):
        tile, off = divmod(g, per_tile)
        halves = []
        for half in range(c_n // per_tile):
            pieces = [xs[half * per_tile + i][:, tile * 128:(tile + 1) * 128] for i in range(per_tile)]
            halves.append(regroup(pieces, off, range(per_tile)))
        u_g = jnp.concatenate(halves, axis=1).astype(BF16)
        r = jnp.dot(u_g, w1_ref[g], preferred_element_type=F32)
        yg[g] = r[:, :S5_TILE]
        sre[:, g * ln:(g + 1) * ln] = r[:, S5_TILE:S5_TILE + ln]
        sim[:, g * ln:(g + 1) * ln] = r[:, S5_TILE + ln:]
    are = ar_ref[...]
    aim = ai_ref[...]
    is_f = (lax.broadcasted_iota(jnp.int32, (1, gb * ln), 1) % ln) < S5_STATE

    def step(j, carry):
        xre, xim = carry
        cf = j
        cb = jnp.where(j < ncc, ncc - 1 - j, nc - 1 - (j - ncc))
        s_re = jnp.where(is_f, sre[pl.ds(cf, 1), :], sre[pl.ds(cb, 1), :])
        s_im = jnp.where(is_f, sim[pl.ds(cf, 1), :], sim[pl.ds(cb, 1), :])
        xfr[pl.ds(cf, 1), :] = xre
        xfi[pl.ds(cf, 1), :] = xim
        xbr[pl.ds(cb, 1), :] = xre
        xbi[pl.ds(cb, 1), :] = xim
        return are * xre - aim * xim + s_re, are * xim + aim * xre + s_im

    zero = jnp.zeros((1, gb * ln), F32)
    lax.fori_loop(0, nc, step, (zero, zero))

    is_f1 = lax.broadcasted_iota(jnp.int32, (1, ln), 1) < S5_STATE
    for g in range(gb):
        sl = slice(g * ln, (g + 1) * ln)
        xp = jnp.concatenate([jnp.where(is_f1, xfr[:, sl], xbr[:, sl]),
                              jnp.where(is_f1, xfi[:, sl], xbi[:, sl])], axis=1).astype(BF16)
        yg[g] += jnp.dot(xp, wo_ref[g], preferred_element_type=F32)

    for t in range(c_n):
        t_tile, t_off = divmod(t, per_tile)
        for tile in range(gb // per_tile):
            pieces = [yg[tile * per_tile + i][:, t_tile * 128:(t_tile + 1) * 128] for i in range(per_tile)]
            y_ref[pl.ds(t, nc, stride=c_n), tile * 128:(tile + 1) * 128] = regroup(pieces, t_off, range(per_tile))


def _s5_mix(proj, w1, wout, ar, ai, layer, n_ctx, gb):
    bsz, t, _ = proj.shape
    depth, g_n = w1.shape[:2]
    nc = t // S5_CHUNK
    ncc = n_ctx // S5_CHUNK
    ln = S5_LANES
    scr = pltpu.VMEM((nc, gb * ln), F32)
    return pl.pallas_call(
        functools.partial(_s5_kernel, gb=gb, nc=nc, ncc=ncc),
        grid=(bsz, g_n // gb),
        in_specs=[pl.BlockSpec((None, t, gb * S5_GROUP), lambda b, j: (b, 0, j)),
                  pl.BlockSpec((None, gb, S5_TILE, 2 * S5_TILE), lambda b, j: (layer, j, 0, 0)),
                  pl.BlockSpec((None, gb, S5_TILE, S5_TILE), lambda b, j: (layer, j, 0, 0)),
                  pl.BlockSpec((None, None, 1, gb * ln), lambda b, j: (layer, j, 0, 0)),
                  pl.BlockSpec((None, None, 1, gb * ln), lambda b, j: (layer, j, 0, 0))],
        out_specs=pl.BlockSpec((None, t, gb * S5_GROUP), lambda b, j: (b, 0, j)),
        out_shape=jax.ShapeDtypeStruct((bsz, t, g_n * S5_GROUP), F32),
        scratch_shapes=[pltpu.VMEM((gb, nc, S5_TILE), F32), scr, scr, scr, scr, scr, scr],
        compiler_params=_cparams("arbitrary", "arbitrary"),
    )(proj, w1, wout, ar.reshape(depth, g_n // gb, 1, gb * ln), ai.reshape(depth, g_n // gb, 1, gb * ln))


def _rope_tables(n_ctx, n_lat):
    pos = jnp.arange(n_lat)
    rows = (pos // GRID_W).astype(F32)
    cols = (pos % GRID_W).astype(F32)
    nf = RET_HEAD_DIM // 4
    freqs = ROPE_BASE ** (-jnp.arange(nf, dtype=F32) / nf)
    ar, ac = rows[:, None] * freqs, cols[:, None] * freqs
    cos = jnp.concatenate([jnp.cos(ar), jnp.cos(ar), jnp.cos(ac), jnp.cos(ac)], axis=-1)
    sin = jnp.concatenate([-jnp.sin(ar), jnp.sin(ar), -jnp.sin(ac), jnp.sin(ac)], axis=-1)
    cos = jnp.concatenate([jnp.ones((n_ctx, RET_HEAD_DIM), F32), cos], axis=0)
    sin = jnp.concatenate([jnp.zeros((n_ctx, RET_HEAD_DIM), F32), sin], axis=0)
    return cos, sin


def _ret_kernel(q_ref, k_ref, v_ref, cos_ref, sin_ref, lg_ref, o_ref, qs, ks, vs, ob, *, nchunk, ncc):
    c_n = RET_CHUNK
    nf = RET_HEAD_DIM // 4
    lane = lax.broadcasted_iota(jnp.int32, (1, RET_HEAD_DIM), 1)
    first = (lane % (2 * nf)) < nf

    def rope(x):
        partner = jnp.where(first, pltpu.roll(x, RET_HEAD_DIM - nf, 1), pltpu.roll(x, nf, 1))
        return x * cos_ref[...] + partner * sin_ref[...]

    qs[...] = rope(q_ref[...]).astype(BF16)
    ks[...] = rope(k_ref[...]) * (RET_HEAD_DIM ** -0.5)
    vs[...] = v_ref[...].astype(BF16)

    lg = jnp.minimum(lg_ref[...], -1e-6)
    lgf = lg[0:1, :]
    lgb = lg[1:2, :]
    ri = lax.broadcasted_iota(jnp.int32, (c_n, c_n), 0)
    ci = lax.broadcasted_iota(jnp.int32, (c_n, c_n), 1)
    diff = (ri - ci).astype(F32)
    low = diff >= 0.0
    decay = jnp.where(low, jnp.exp(jnp.where(low, diff, 0.0) * lgf), jnp.exp(jnp.where(low, 0.0, -diff) * lgb))
    pos = ri.astype(F32)
    xi_f = jnp.exp((pos + 1.0) * lgf)
    zeta_f = jnp.exp((c_n - 1.0 - pos) * lgf)
    gch_f = jnp.exp(c_n * lgf)
    xi_b = jnp.exp((c_n - pos) * lgb)
    zeta_b = jnp.exp(pos * lgb)
    gch_b = jnp.exp(c_n * lgb)
    nt = (((1,), (1,)), ((), ()))

    def step(j, carry):
        rf, rb = carry
        sl = pl.ds(pl.multiple_of(j * c_n, c_n), c_n)
        qc, kf, vc = qs[sl, :], ks[sl, :], vs[sl, :]
        s = lax.dot_general(qc, kf.astype(BF16), nt, preferred_element_type=F32)
        o = jnp.dot((s * decay).astype(BF16), vc, preferred_element_type=F32)
        o_ref[sl, :] = o + jnp.dot(qc, rf.astype(BF16), preferred_element_type=F32) * xi_f
        rf = gch_f * rf + jnp.dot((kf * zeta_f).T.astype(BF16), vc, preferred_element_type=F32)

        cb = jnp.where(j < ncc, ncc - 1 - j, nchunk - 1 - (j - ncc))
        sb = pl.ds(pl.multiple_of(cb * c_n, c_n), c_n)
        qb, kb, vb = qs[sb, :], ks[sb, :], vs[sb, :]
        ob[sb, :] = jnp.dot(qb, rb.astype(BF16), preferred_element_type=F32) * xi_b
        rb = gch_b * rb + jnp.dot((kb * zeta_b).T.astype(BF16), vb, preferred_element_type=F32)
        return rf, rb

    zero = jnp.zeros((c_n, c_n), F32)
    lax.fori_loop(0, nchunk, step, (zero, zero), unroll=2)
    o_ref[...] += ob[...]


def _retention(proj, cos, sin, lg_tab, n_ctx):
    bsz, t, _ = proj.shape
    hd = RET_HEAD_DIM
    nchunk = t // RET_CHUNK
    ncc = n_ctx // RET_CHUNK
    q0 = BRANCH_WIDTH // hd
    return pl.pallas_call(
        functools.partial(_ret_kernel, nchunk=nchunk, ncc=ncc),
        grid=(bsz, RET_HEADS),
        in_specs=[pl.BlockSpec((None, t, hd), lambda b, h: (b, 0, q0 + h)),
                  pl.BlockSpec((None, t, hd), lambda b, h: (b, 0, 2 * q0 + h)),
                  pl.BlockSpec((None, t, hd), lambda b, h: (b, 0, 3 * q0 + h)),
                  pl.BlockSpec((t, hd), lambda b, h: (0, 0)),
                  pl.BlockSpec((t, hd), lambda b, h: (0, 0)),
                  pl.BlockSpec((None, 8, hd), lambda b, h: (h, 0, 0))],
        out_specs=pl.BlockSpec((None, t, hd), lambda b, h: (b, 0, h)),
        out_shape=jax.ShapeDtypeStruct((bsz, t, BRANCH_WIDTH), F32),
        scratch_shapes=[pltpu.VMEM((t, hd), BF16), pltpu.VMEM((t, hd), F32), pltpu.VMEM((t, hd), BF16),
                        pltpu.VMEM((t, hd), F32)],
        compiler_params=_cparams("arbitrary", "arbitrary"),
    )(proj, proj, proj, cos, sin, lg_tab)


def _merge_kernel(*refs, ngb):
    ys_ref, or_ref, gr_ref = refs[0:3]
    gs_refs = refs[3:3 + ngb]
    gt_refs = refs[3 + ngb:3 + 2 * ngb]
    (x_ref, mod_ref, rg_ref, n2_ref, wglu_ref, wbs_ref, wbr_ref, wout_ref, wr_ref,
     xo_ref, h2_ref, lg_ref) = refs[3 + 2 * ngb:]
    w = BRANCH_WIDTH
    tm, d = x_ref.shape
    sub = 128
    nw = d // 256
    for r0 in range(0, tm, sub):
        rows = slice(r0, r0 + sub)
        z = _gelu_tanh(ys_ref[rows, :]).astype(BF16)
        zz = jnp.dot(z, wglu_ref[...], preferred_element_type=F32)
        s5b = jnp.dot((zz[:, :w] * jax.nn.sigmoid(zz[:, w:])).astype(BF16), wbs_ref[...],
                      preferred_element_type=F32)
        o = or_ref[rows, :]
        on = jnp.concatenate([_rms(o[:, h * RET_HEAD_DIM:(h + 1) * RET_HEAD_DIM]) for h in range(RET_HEADS)],
                             axis=1)
        on = on * rg_ref[...]
        rb = jnp.dot((on * _silu(gr_ref[rows, :])).astype(BF16), wbr_ref[...], preferred_element_type=F32)
        gate_s = jax.nn.sigmoid(jnp.concatenate([r[rows, :] for r in gs_refs], axis=1))
        gate_r = jax.nn.sigmoid(jnp.concatenate([r[rows, :] for r in gt_refs], axis=1))
        mix = jnp.dot((gate_s * s5b + gate_r * rb).astype(BF16), wout_ref[...], preferred_element_type=F32)
        xn = x_ref[rows, :] + mod_ref[2:3, :] * mix
        xo_ref[rows, :] = xn
        h2 = (_rms(xn) * n2_ref[...]) * (1.0 + mod_ref[4:5, :]) + mod_ref[3:4, :]
        _store_token_rows(h2_ref.at[pl.ds(r0 * nw, sub * nw), :], h2)
        lg_ref[:, rows] = lax.dot_general(wr_ref[...], h2, (((1,), (1,)), ((), ())), preferred_element_type=F32,
                                          precision=lax.Precision.HIGHEST)


def _merge(y_s5, o_ret, proj, xs, mods, ret_norm_g, norm2_g, wglu, wbs, wbr, wout, w_router, n_ctx, tm,
           latent_only):
    bsz, t, d = xs.shape
    w = BRANCH_WIDTH
    gw = 1024
    ngb = d // gw
    gs0 = 5 * w // gw
    ctx_tiles = n_ctx // tm
    skip = ctx_tiles if latent_only else 0
    row = lambda b, i: (b, i + skip, 0)
    const = lambda b, i: (0, 0)
    one = pl.Buffered(1)

    def gate_spec(k):
        return pl.BlockSpec((None, tm, gw), lambda b, i: (b, i + skip, gs0 + k))

    in_specs = ([pl.BlockSpec((None, tm, w), row), pl.BlockSpec((None, tm, w), row),
                 pl.BlockSpec((None, tm, w), lambda b, i: (b, i + skip, 4))]
                + [gate_spec(k) for k in range(ngb)] + [gate_spec(ngb + k) for k in range(ngb)]
                + [pl.BlockSpec((None, tm, d), row),
                   pl.BlockSpec((None, N_MOD, d), lambda b, i: (jnp.where(i + skip < ctx_tiles, bsz, b), 0, 0)),
                   pl.BlockSpec((1, w), const), pl.BlockSpec((1, d), const),
                   pl.BlockSpec((w, 2 * w), const, pipeline_mode=one),
                   pl.BlockSpec((w, d), const, pipeline_mode=one),
                   pl.BlockSpec((w, d), const, pipeline_mode=one),
                   pl.BlockSpec((d, d), const, pipeline_mode=one),
                   pl.BlockSpec((N_EXPERTS, d), const, pipeline_mode=one)])
    nt = d // 256
    tiles = t // tm - skip
    t_out = tiles * tm
    return pl.pallas_call(
        functools.partial(_merge_kernel, ngb=ngb),
        grid=(bsz, tiles),
        in_specs=in_specs,
        out_specs=[pl.BlockSpec((None, tm, d), lambda b, i: (b, i, 0)),
                   pl.BlockSpec((tm * nt, 128), lambda b, i: (b * tiles + i, 0)),
                   pl.BlockSpec((N_EXPERTS, tm), lambda b, i: (0, b * tiles + i))],
        out_shape=[jax.ShapeDtypeStruct((bsz, t_out, d), F32), jax.ShapeDtypeStruct((bsz * t_out * nt, 128), U32),
                   jax.ShapeDtypeStruct((N_EXPERTS, bsz * t_out), F32)],
        compiler_params=_cparams("arbitrary", "arbitrary"),
    )(y_s5, o_ret, proj, *([proj] * (2 * ngb)), xs, mods, ret_norm_g.reshape(1, w), norm2_g.reshape(1, d),
      wglu, wbs, wbr, wout, w_router.T)


def _route_kernel(lg_ref, bias_ref, tw_ref, ei_ref, rk_ref, cnt_ref, carry, *, tm):
    i = pl.program_id(0)
    ne = N_EXPERTS
    per_group = ne // N_GROUPS
    neg = -jnp.inf

    @pl.when(i == 0)
    def _():
        carry[...] = jnp.zeros_like(carry)

    scores = jax.nn.sigmoid(lg_ref[...])
    biased = scores + bias_ref[...]
    sub = lax.broadcasted_iota(jnp.int32, (per_group, tm), 0)
    gscore = []
    for g in range(N_GROUPS):
        blk = biased[g * per_group:(g + 1) * per_group, :]
        m1 = jnp.max(blk, axis=0, keepdims=True)
        first = jnp.min(jnp.where(blk == m1, sub, per_group), axis=0, keepdims=True)
        m2 = jnp.max(jnp.where(sub == first, neg, blk), axis=0, keepdims=True)
        gscore.append(m1 + m2)
    masked = []
    for g in range(N_GROUPS):
        beaten = jnp.zeros((1, tm), jnp.int32)
        for j in range(N_GROUPS):
            if j != g:
                wins = (gscore[j] >= gscore[g]) if j < g else (gscore[j] > gscore[g])
                beaten = beaten + wins.astype(jnp.int32)
        keep = beaten < TOPK_GROUPS
        masked.append(jnp.where(keep, biased[g * per_group:(g + 1) * per_group, :], neg))
    mv = jnp.concatenate(masked, axis=0)
    eidx = lax.broadcasted_iota(jnp.int32, (ne, tm), 0)
    beaten = jnp.zeros((ne, tm), jnp.int32)
    for j in range(ne):
        vj = mv[j:j + 1, :]
        wins = (vj > mv) | ((vj == mv) & (j < eidx))
        beaten = beaten + wins.astype(jnp.int32)
    sel = beaten < TOP_K
    sel_w = jnp.where(sel, scores, 0.0)
    wd = sel_w / jnp.sum(sel_w, axis=0, keepdims=True) * ROUTED_SCALE
    sel_b = sel.astype(BF16)
    r_i = lax.broadcasted_iota(jnp.int32, (ne, ne), 0)
    c_i = lax.broadcasted_iota(jnp.int32, (ne, ne), 1)
    slot = jnp.dot((c_i < r_i).astype(BF16), sel_b, preferred_element_type=F32)
    t_r = lax.broadcasted_iota(jnp.int32, (tm, tm), 0)
    t_c = lax.broadcasted_iota(jnp.int32, (tm, tm), 1)
    rank = jnp.dot(sel_b, (t_r < t_c).astype(BF16), preferred_element_type=F32) + carry[:, 0:1]
    carry[...] = carry[...] + jnp.sum(sel.astype(F32), axis=1, keepdims=True)
    cnt_ref[...] = carry[...]
    eidx_f = eidx.astype(F32)
    for k in range(TOP_K):
        mk = sel & (slot == float(k))
        tw_ref[k:k + 1, :] = jnp.sum(jnp.where(mk, wd, 0.0), axis=0, keepdims=True)
        ei_ref[k:k + 1, :] = jnp.sum(jnp.where(mk, eidx_f, 0.0), axis=0, keepdims=True).astype(jnp.int32)
        rk_ref[k:k + 1, :] = jnp.sum(jnp.where(mk, rank, 0.0), axis=0, keepdims=True).astype(jnp.int32)


def _route(logits_t, b_router, tm, batch, n):
    ne = logits_t.shape[0]
    col = lambda i: (0, i)
    tiles = n // tm
    return pl.pallas_call(
        functools.partial(_route_kernel, tm=tm),
        grid=(tiles,),
        in_specs=[pl.BlockSpec((ne, tm), lambda i: (0, batch * tiles + i)), pl.BlockSpec((ne, 1), lambda i: (0, 0))],
        out_specs=[pl.BlockSpec((TOP_K, tm), col), pl.BlockSpec((TOP_K, tm), col), pl.BlockSpec((TOP_K, tm), col),
                   pl.BlockSpec((ne, 128), lambda i: (0, 0))],
        out_shape=[jax.ShapeDtypeStruct((TOP_K, n), F32), jax.ShapeDtypeStruct((TOP_K, n), jnp.int32),
                   jax.ShapeDtypeStruct((TOP_K, n), jnp.int32), jax.ShapeDtypeStruct((ne, 128), F32)],
        scratch_shapes=[pltpu.VMEM((ne, 128), F32)],
        compiler_params=_cparams("arbitrary"),
    )(logits_t, b_router.reshape(ne, 1))


def _row_tok_kernel(lo_ref, hi_ref, pos_ref, out_ref, *, tb):
    j = pl.program_id(0)

    @pl.when(j == 0)
    def _():
        last = out_ref.shape[0] - 8

        def per_range(e, carry):
            lo = lo_ref[e]

            def fill(q, c):
                base = jnp.minimum(lo + q * 8, last)
                for r in range(8):
                    out_ref[base + r] = 0
                return c
            lax.fori_loop(0, (hi_ref[e] - lo + 7) // 8, fill, 0)
            return carry
        lax.fori_loop(0, lo_ref.shape[0], per_range, 0)

    def body(n, carry):
        for k in range(TOP_K):
            out_ref[pos_ref[k, n]] = j * tb + n
        return carry
    lax.fori_loop(0, tb, body, 0, unroll=8)


def _dispatch_plan(ei8, rk8, cnt, n_blocks, tb):
    n = ei8.shape[1]
    counts = cnt[:, 0].astype(jnp.int32)
    padded = (counts + MOE_ROWS - 1) // MOE_ROWS * MOE_ROWS
    pad_end = jnp.cumsum(padded)
    pad_start = pad_end - padded
    onehot = ei8[..., None] == jnp.arange(N_EXPERTS, dtype=jnp.int32)
    pos8 = rk8 + jnp.sum(jnp.where(onehot, pad_start, 0), axis=-1)
    blk_e = jnp.minimum(jnp.sum(jnp.arange(n_blocks, dtype=jnp.int32)[:, None] * MOE_ROWS >= pad_end[None, :],
                                axis=1), N_EXPERTS - 1).astype(jnp.int32)
    n_used = (pad_end[-1] // MOE_ROWS).astype(jnp.int32).reshape(1)
    run_end = jnp.sum(jnp.where(blk_e[:, None] == jnp.arange(N_EXPERTS, dtype=jnp.int32), pad_end // MOE_ROWS, 0),
                      axis=1)
    nxt_e = jnp.where(run_end < n_used[0], blk_e[jnp.minimum(run_end, n_blocks - 1)], -1).astype(jnp.int32)
    grid_spec = pltpu.PrefetchScalarGridSpec(
        num_scalar_prefetch=2,
        grid=(n // tb,),
        in_specs=[pl.BlockSpec((TOP_K, tb), lambda j, lo, hi: (0, j), memory_space=pltpu.SMEM)],
        out_specs=pl.BlockSpec(memory_space=pltpu.SMEM))
    row_tok = pl.pallas_call(
        functools.partial(_row_tok_kernel, tb=tb),
        grid_spec=grid_spec,
        out_shape=jax.ShapeDtypeStruct((n_blocks * MOE_ROWS,), jnp.int32),
        compiler_params=_cparams("arbitrary"),
    )(jnp.concatenate([pad_start + counts, pad_end[-1:]]).astype(jnp.int32),
      jnp.concatenate([pad_end, jnp.full((1,), n_blocks * MOE_ROWS)]).astype(jnp.int32), pos8)
    return row_tok, pos8.reshape(-1), blk_e, nxt_e, n_used


def _expert_kernel(blk_e_ref, nxt_e_ref, nused_ref, tok_ref, h_ref, wg_hbm, wu_hbm, wd_hbm, y_ref,
                   xg, stage_g, stage_u, stage_d, sem, wgb, wub, wdb, *, layer):
    bm = MOE_ROWS
    i = pl.program_id(0)
    n_used = nused_ref[0]
    nt = xg.shape[0] // bm
    d = nt * 256

    def weight_copies(e):
        return (pltpu.make_async_copy(wg_hbm.at[layer, e], stage_g, sem.at[0]),
                pltpu.make_async_copy(wu_hbm.at[layer, e], stage_u, sem.at[1]),
                pltpu.make_async_copy(wd_hbm.at[layer, e], stage_d, sem.at[2]))

    @pl.when(i < n_used)
    def _():
        e = blk_e_ref[i]

        @pl.when(i == 0)
        def _():
            for cp in weight_copies(e):
                cp.start()

        def gather(r, carry):
            tok = tok_ref[i * bm + r]
            xg[pl.ds(pl.multiple_of(r * nt, nt), nt), :] = h_ref[pl.ds(pl.multiple_of(tok * nt, nt), nt), :]
            return carry
        lax.fori_loop(0, bm, gather, 0, unroll=8)

        new_expert = jnp.logical_or(i == 0, e != blk_e_ref[jnp.maximum(i - 1, 0)])

        @pl.when(new_expert)
        def _():
            for cp in weight_copies(e):
                cp.wait()
            wgb[...] = stage_g[...].astype(BF16)
            wub[...] = stage_u[...].astype(BF16)
            wdb[...] = stage_d[...].astype(BF16)
            nxt = nxt_e_ref[i]

            @pl.when(nxt >= 0)
            def _():
                for cp in weight_copies(nxt):
                    cp.start()

        x = _load_token_rows(xg, bm, d).astype(BF16)
        g = jnp.dot(x, wgb[...], preferred_element_type=F32)
        u = jnp.dot(x, wub[...], preferred_element_type=F32)
        _store_token_rows(y_ref, jnp.dot((_silu(g) * u).astype(BF16), wdb[...], preferred_element_type=F32))

    @pl.when(i >= n_used)
    def _():
        y_ref[...] = jnp.zeros_like(y_ref)


def _experts(h2c, row_tok, blk_e, nxt_e, n_used, w_gate, w_up, w_down, layer, n_blocks):
    d, de = w_gate.shape[-2:]
    nt = d // 256
    bm = MOE_ROWS
    hbm = pl.BlockSpec(memory_space=pl.ANY)
    grid_spec = pltpu.PrefetchScalarGridSpec(
        num_scalar_prefetch=4,
        grid=(n_blocks,),
        in_specs=[pl.BlockSpec(h2c.shape, lambda i, *_: (0, 0), pipeline_mode=pl.Buffered(1)), hbm, hbm, hbm],
        out_specs=pl.BlockSpec((bm * nt, 128), lambda i, *_: (i, 0)),
        scratch_shapes=[pltpu.VMEM((bm * nt, 128), U32),
                        pltpu.VMEM((d, de), F32), pltpu.VMEM((d, de), F32), pltpu.VMEM((de, d), F32),
                        pltpu.SemaphoreType.DMA((3,)),
                        pltpu.VMEM((d, de), BF16), pltpu.VMEM((d, de), BF16), pltpu.VMEM((de, d), BF16)])
    return pl.pallas_call(
        functools.partial(_expert_kernel, layer=layer),
        grid_spec=grid_spec,
        out_shape=jax.ShapeDtypeStruct((n_blocks * bm * nt, 128), U32),
        compiler_params=pltpu.CompilerParams(dimension_semantics=("arbitrary",),
                                             vmem_limit_bytes=EXPERT_VMEM_LIMIT),
    )(blk_e, nxt_e, n_used, row_tok, h2c, w_gate, w_up, w_down)


def _combine_kernel(pos_ref, ys_hbm, tw_ref, h2_ref, x_ref, mod_ref, wsg_ref, wsu_ref, wsd_ref, fg_ref,
                    o_ref, buf, sem, *, tmc, final):
    i = pl.program_id(0)
    n_tiles = pl.num_programs(0)
    n_tok = n_tiles * tmc
    slot = i % 2
    nt = buf.shape[2] // tmc
    d = nt * 256

    def row_copy(p, slot_, k, t):
        return pltpu.make_async_copy(ys_hbm.at[pl.ds(pl.multiple_of(p * nt, nt), nt), :],
                                     buf.at[slot_, k, pl.ds(pl.multiple_of(t * nt, nt), nt), :], sem.at[slot_])

    def issue(tile, slot_):
        def body(t, carry):
            for k in range(TOP_K):
                row_copy(pos_ref[k * n_tok + tile * tmc + t], slot_, k, t).start(priority=k % 2)
            return carry
        lax.fori_loop(0, tmc, body, 0)

    @pl.when(i == 0)
    def _():
        issue(0, 0)

    @pl.when(i + 1 < n_tiles)
    def _():
        issue(i + 1, 1 - slot)

    h = _load_token_rows(h2_ref, tmc, d).astype(BF16)
    sg = jnp.dot(h, wsg_ref[...], preferred_element_type=F32)
    su = jnp.dot(h, wsu_ref[...], preferred_element_type=F32)
    ff = jnp.dot((_silu(sg) * su).astype(BF16), wsd_ref[...], preferred_element_type=F32)

    def wait_body(t, carry):
        for k in range(TOP_K):
            row_copy(0, slot, k, t).wait()
        return carry
    lax.fori_loop(0, tmc, wait_body, 0)

    tw = tw_ref[...]
    routed = tw[:, 0:1] * _load_token_rows(buf.at[slot, 0], tmc, d)
    for k in range(1, TOP_K):
        routed = routed + tw[:, k:k + 1] * _load_token_rows(buf.at[slot, k], tmc, d)
    xn = x_ref[...] + mod_ref[5:6, :] * (routed + ff)
    if final:
        xn = _rms(xn) * fg_ref[...]
    o_ref[...] = xn


def _combine(ys, pos, top_w, h2c, xs_flat, mods, wsg, wsu, wsd, final_g, n_ctx, t_per_batch, final):
    d, de = wsg.shape
    nt = d // 256
    n = xs_flat.shape[0]
    tmc = COMBINE_TOKENS
    tiles = n // tmc
    bsz = n // t_per_batch
    tiles_per_batch = t_per_batch // tmc
    ctx_tiles = n_ctx // tmc
    const = lambda i, p: (0, 0)
    row = lambda i, p: (i, 0)

    def mod_map(i, p):
        return (jnp.where(i % tiles_per_batch < ctx_tiles, bsz, i // tiles_per_batch), 0, 0)

    grid_spec = pltpu.PrefetchScalarGridSpec(
        num_scalar_prefetch=1,
        grid=(tiles,),
        in_specs=[pl.BlockSpec(memory_space=pl.ANY),
                  pl.BlockSpec((tmc, TOP_K), row),
                  pl.BlockSpec((tmc * nt, 128), row),
                  pl.BlockSpec((tmc, d), row),
                  pl.BlockSpec((None, N_MOD, d), mod_map),
                  pl.BlockSpec((d, de), const), pl.BlockSpec((d, de), const), pl.BlockSpec((de, d), const),
                  pl.BlockSpec((1, d), const)],
        out_specs=pl.BlockSpec((tmc, d), row),
        scratch_shapes=[pltpu.VMEM((2, TOP_K, tmc * nt, 128), U32), pltpu.SemaphoreType.DMA((2,))])
    return pl.pallas_call(
        functools.partial(_combine_kernel, tmc=tmc, final=final),
        grid_spec=grid_spec,
        out_shape=jax.ShapeDtypeStruct(xs_flat.shape, F32),
        input_output_aliases={4: 0},
        compiler_params=_cparams("arbitrary"),
    )(pos, ys, top_w, h2c, xs_flat, mods, wsg, wsu, wsd, final_g.reshape(1, d))


def _pick_tile(n, candidates):
    for c in candidates:
        if n % c == 0:
            return c
    raise ValueError(f"no tile for {n}")


def kernel(x, c, ctx, c_ctx, w_mod, b_mod, norm1_g, norm2_g, final_g, w_in, s5_lam_re, s5_lam_im, s5_log_dt, s5_b_re, s5_b_im, s5_c_re, s5_c_im, s5_d, s5_w_glu, ret_log_decay, ret_norm_g, w_br_s5, w_br_ret, w_out, moe_router, moe_router_bias, moe_w_gate, moe_w_up, moe_w_down, sh_w_gate, sh_w_up, sh_w_down):
    bsz, n_lat, d = x.shape
    n_ctx = ctx.shape[1]
    depth = w_mod.shape[0]
    t = n_ctx + n_lat
    n_tok = bsz * t
    g_n = s5_b_re.shape[1]
    assert n_ctx % RET_CHUNK == 0 and n_lat % RET_CHUNK == 0 and bsz + 1 <= 8
    assert d % 1024 == 0 and n_tok % COMBINE_TOKENS == 0 and n_ctx % COMBINE_TOKENS == 0

    cin = jnp.zeros((8, d), F32).at[:bsz].set(c).at[bsz].set(c_ctx)
    mods_all = _modulation(cin, w_mod, b_mod).reshape(depth, 8, N_MOD, d)
    cos, sin = _rope_tables(n_ctx, n_lat)
    s5_w1, s5_wo, s5_ar, s5_ai = jax.vmap(_s5_tables)(s5_lam_re, s5_lam_im, s5_log_dt, s5_b_re, s5_b_im,
                                                      s5_c_re, s5_c_im, s5_d)
    xs = jnp.concatenate([ctx, x], axis=1)

    tm_in = _pick_tile(t, (544, 272, 256, 128))
    tm_merge = _pick_tile(n_ctx, (256, 128))
    gb = 128 // S5_GROUP

    for l in range(depth):
        mods = mods_all[l]
        proj = _in_proj(xs, norm1_g[l], mods, w_in[l].astype(BF16), n_ctx, tm_in)

        y_s5 = _s5_mix(proj, s5_w1, s5_wo, s5_ar, s5_ai, l, n_ctx, gb)

        lg_tab = jnp.zeros((RET_HEADS, 8, RET_HEAD_DIM), F32).at[:, 0:2, :].set(
            jnp.broadcast_to(ret_log_decay[l].T[:, :, None], (RET_HEADS, 2, RET_HEAD_DIM)))
        o_ret = _retention(proj, cos, sin, lg_tab, n_ctx)

        last = l == depth - 1
        t_l, ctx_l = (n_lat, 0) if last else (t, n_ctx)
        n_l = bsz * t_l
        xs, h2c, logits_t = _merge(y_s5, o_ret, proj, xs, mods, ret_norm_g[l], norm2_g[l],
                                   s5_w_glu[l].astype(BF16), w_br_s5[l].astype(BF16), w_br_ret[l].astype(BF16),
                                   w_out[l].astype(BF16), moe_router[l], n_ctx, tm_merge, latent_only=last)

        n_blocks = -(-(n_l * TOP_K + N_EXPERTS * (MOE_ROWS - 1)) // MOE_ROWS)
        tw8, ei8, rk8, cnt = _route(logits_t, moe_router_bias[l], _pick_tile(n_l, (256, 128)), 0, n_l)
        row_tok, pos, blk_e, nxt_e, n_used = _dispatch_plan(ei8, rk8, cnt, n_blocks,
                                                            _pick_tile(n_l, (2176, 2048, 1024, 512, 256, 128)))
        ys = _experts(h2c, row_tok, blk_e, nxt_e, n_used, moe_w_gate, moe_w_up, moe_w_down, l, n_blocks)
        xs = _combine(ys, pos, tw8.T, h2c, xs.reshape(n_l, d), mods, sh_w_gate[l].astype(BF16),
                      sh_w_up[l].astype(BF16), sh_w_down[l].astype(BF16), final_g, ctx_l, t_l,
                      final=last).reshape(bsz, t_l, d)

    return xs
```

```python
import functools

import jax
import jax.numpy as jnp
from jax import lax
from jax.experimental import pallas as pl
from jax.experimental.pallas import tpu as pltpu

F32 = jnp.float32
BF16 = jnp.bfloat16

NORM_EPS = 1e-6
N_MOD = 6
GRID_W = 64
ROPE_BASE = 10000.0

S5_GROUP = 16
S5_STATE = 64
S5_CHUNK = 16
S5_TILE = S5_CHUNK * S5_GROUP
S5_LANES = 2 * S5_STATE

RET_HEADS = 8
RET_HEAD_DIM = 128
RET_CHUNK = 128
BRANCH_WIDTH = RET_HEADS * RET_HEAD_DIM

N_EXPERTS = 64
TOP_K = 8
N_GROUPS = 8
TOPK_GROUPS = 4
ROUTED_SCALE = 2.5
MOE_ROWS = 256
COMBINE_TOKENS = 128

VMEM_LIMIT = 56 * 1024 * 1024
EXPERT_VMEM_LIMIT = 62 * 1024 * 1024


def _cparams(*sem):
    return pltpu.CompilerParams(dimension_semantics=sem, vmem_limit_bytes=VMEM_LIMIT)


def _silu(x):
    return x * jax.nn.sigmoid(x)


def _gelu_tanh(x):
    return 0.5 * x * (1.0 + jnp.tanh(0.7978845608028654 * (x + 0.044715 * x * x * x)))


def _rms(x):
    return x * lax.rsqrt(jnp.mean(x * x, axis=-1, keepdims=True) + NORM_EPS)


U32 = jnp.uint32
HIGH_HALF = 0xFFFF0000


def _bf16_bits(x):
    return lax.bitcast_convert_type(x.astype(BF16).astype(F32), U32)


def _store_token_rows(ref, val):
    rows, d = val.shape
    nw = d // 256
    for j in range(nw):
        lo = _bf16_bits(val[:, j * 128:(j + 1) * 128])
        hi = _bf16_bits(val[:, (j + nw) * 128:(j + nw + 1) * 128])
        ref[pl.ds(j, rows, stride=nw), :] = (hi & U32(HIGH_HALF)) | (lo >> 16)


def _load_token_rows(ref, rows, d):
    nw = d // 256
    words = [ref[pl.ds(j, rows, stride=nw), :] for j in range(nw)]
    lo = [lax.bitcast_convert_type(w << 16, F32) for w in words]
    hi = [lax.bitcast_convert_type(w & U32(HIGH_HALF), F32) for w in words]
    return jnp.concatenate(lo + hi, axis=1)


def _mod_kernel(c_ref, w_ref, b_ref, o_ref):
    a = _silu(c_ref[...]).astype(BF16)
    o_ref[...] = jnp.dot(a, w_ref[...].astype(BF16), preferred_element_type=F32) + b_ref[...]


def _modulation(cin, w_mod, b_mod):
    depth, d, n = w_mod.shape
    tn = 1024
    return pl.pallas_call(
        _mod_kernel,
        grid=(depth, n // tn),
        in_specs=[pl.BlockSpec((8, d), lambda l, j: (0, 0)),
                  pl.BlockSpec((None, d, tn), lambda l, j: (l, 0, j)),
                  pl.BlockSpec((None, 1, tn), lambda l, j: (l, 0, j))],
        out_specs=pl.BlockSpec((None, 8, tn), lambda l, j: (l, 0, j)),
        out_shape=jax.ShapeDtypeStruct((depth, 8, n), F32),
        compiler_params=_cparams("arbitrary", "arbitrary"),
    )(cin, w_mod, b_mod.reshape(depth, 1, n))


def _in_proj_kernel(x_ref, g_ref, ml_ref, mc_ref, w_ref, o_ref, h_scr, *, n_ctx, tm):
    i = pl.program_id(1)
    j = pl.program_id(2)

    @pl.when(j == 0)
    def _():
        y = _rms(x_ref[...]) * g_ref[...]
        row = i * tm + lax.broadcasted_iota(jnp.int32, (tm, 1), 0)
        is_ctx = row < n_ctx
        shift = jnp.where(is_ctx, mc_ref[0:1, :], ml_ref[0:1, :])
        scale = jnp.where(is_ctx, mc_ref[1:2, :], ml_ref[1:2, :])
        h_scr[...] = (y * (1.0 + scale) + shift).astype(BF16)

    o_ref[...] = jnp.dot(h_scr[...], w_ref[...], preferred_element_type=F32)


def _in_proj(xs, norm_g, mods, w_in_bf16, n_ctx, tm):
    bsz, t, d = xs.shape
    n = w_in_bf16.shape[1]
    tn = 1024
    return pl.pallas_call(
        functools.partial(_in_proj_kernel, n_ctx=n_ctx, tm=tm),
        grid=(bsz, t // tm, n // tn),
        in_specs=[pl.BlockSpec((None, tm, d), lambda b, i, j: (b, i, 0)),
                  pl.BlockSpec((1, d), lambda b, i, j: (0, 0)),
                  pl.BlockSpec((None, N_MOD, d), lambda b, i, j: (b, 0, 0)),
                  pl.BlockSpec((None, N_MOD, d), lambda b, i, j: (bsz, 0, 0)),
                  pl.BlockSpec((d, tn), lambda b, i, j: (0, j))],
        out_specs=pl.BlockSpec((None, tm, tn), lambda b, i, j: (b, i, j)),
        out_shape=jax.ShapeDtypeStruct((bsz, t, n), F32),
        scratch_shapes=[pltpu.VMEM((tm, d), BF16)],
        compiler_params=_cparams("arbitrary", "arbitrary", "arbitrary"),
    )(xs, norm_g.reshape(1, d), mods, mods, w_in_bf16)


def _s5_tables(lam_re, lam_im, log_dt, b_re, b_im, c_re, c_im, d_skip):
    hp = lax.Precision.HIGHEST
    g_n, p_n, h_n = b_re.shape
    c_n = S5_CHUNK
    dt = jnp.exp(log_dt)[..., None]
    lre = jnp.minimum(lam_re, -1e-4)
    steps = jnp.arange(c_n + 1, dtype=F32)[:, None, None, None]
    mag = jnp.exp(steps * (lre * dt))
    ang = steps * (lam_im * dt)
    pr, pi = mag * jnp.cos(ang), mag * jnp.sin(ang)
    a_re, a_im = pr[1], pi[1]
    den = lre * lre + lam_im * lam_im
    nr, ni = a_re - 1.0, a_im
    f_re = (nr * lre + ni * lam_im) / den
    f_im = (ni * lre - nr * lam_im) / den
    bb_re = f_re[..., None] * b_re - f_im[..., None] * b_im
    bb_im = f_re[..., None] * b_im + f_im[..., None] * b_re
    w_re = pr[..., None] * bb_re - pi[..., None] * bb_im
    w_im = pr[..., None] * bb_im + pi[..., None] * bb_re
    kl = (jnp.einsum('dghp,ndgpk->ndgkh', c_re, w_re[:c_n], precision=hp)
          - jnp.einsum('dghp,ndgpk->ndgkh', c_im, w_im[:c_n], precision=hp))
    lag0 = kl[0, 0] + kl[0, 1] + jnp.eye(h_n, dtype=F32)[None] * d_skip[:, None, :]
    lags = jnp.concatenate([kl[:0:-1, 1], lag0[None], kl[1:, 0]], axis=0)
    t_minus_s = jnp.arange(c_n)[None, :] - jnp.arange(c_n)[:, None] + (c_n - 1)
    ktoep = lags[t_minus_s].transpose(2, 0, 3, 1, 4).reshape(g_n, S5_TILE, S5_TILE)

    def st(w, idx, d):
        return w[idx, d].transpose(1, 0, 3, 2).reshape(g_n, S5_TILE, p_n)

    fwd_idx = c_n - 1 - jnp.arange(c_n)
    bwd_idx = jnp.arange(c_n)
    wst = jnp.concatenate([st(w_re, fwd_idx, 0), st(w_re, bwd_idx, 1),
                           st(w_im, fwd_idx, 0), st(w_im, bwd_idx, 1)], axis=-1)

    def out_rows(d, idx):
        cr = c_re[d][None]
        ci = c_im[d][None]
        er = pr[idx, d][:, :, None, :]
        ei = pi[idx, d][:, :, None, :]
        re = (cr * er - ci * ei).transpose(1, 3, 0, 2).reshape(g_n, p_n, S5_TILE)
        im = (cr * ei + ci * er).transpose(1, 3, 0, 2).reshape(g_n, p_n, S5_TILE)
        return re, im

    fo_re, fo_im = out_rows(0, jnp.arange(c_n) + 1)
    bo_re, bo_im = out_rows(1, c_n - jnp.arange(c_n))
    wout = jnp.concatenate([fo_re, bo_re, -fo_im, -bo_im], axis=1)
    w1 = jnp.concatenate([ktoep, wst], axis=-1).astype(BF16)
    ar = jnp.concatenate([pr[c_n, 0], pr[c_n, 1]], axis=-1)
    ai = jnp.concatenate([pi[c_n, 0], pi[c_n, 1]], axis=-1)
    return w1, wout.astype(BF16), ar, ai


def _s5_kernel(u_ref, w1_ref, wo_ref, ar_ref, ai_ref, y_ref, yg, sre, sim, xfr, xfi, xbr, xbi, *, gb, nc, ncc):
    ln = S5_LANES
    c_n = S5_CHUNK
    per_tile = 128 // S5_GROUP
    lane_grp = lax.broadcasted_iota(jnp.int32, (1, 128), 1) // S5_GROUP

    def regroup(pieces, src_off, dst_offs):
        acc = None
        for piece, dst in zip(pieces, dst_offs):
            shift = ((dst - src_off) * S5_GROUP) % 128
            rolled = pltpu.roll(piece, shift, 1) if shift else piece
            acc = rolled if acc is None else jnp.where(lane_grp == dst, rolled, acc)
        return acc

    xs = [u_ref[pl.ds(s, nc, stride=c_n), :] for s in range(c_n)]
    for g in range(gb):
        tile, off = divmod(g, per_tile)
        halves = []
        for half in range(c_n // per_tile):
            pieces = [xs[half * per_tile + i][:, tile * 128:(tile + 1) * 128] for i in range(per_tile)]
            halves.append(regroup(pieces, off, range(per_tile)))
        u_g = jnp.concatenate(halves, axis=1).astype(BF16)
        r = jnp.dot(u_g, w1_ref[g], preferred_element_type=F32)
        yg[g] = r[:, :S5_TILE]
        sre[:, g * ln:(g + 1) * ln] = r[:, S5_TILE:S5_TILE + ln]
        sim[:, g * ln:(g + 1) * ln] = r[:, S5_TILE + ln:]
    are = ar_ref[...]
    aim = ai_ref[...]
    is_f = (lax.broadcasted_iota(jnp.int32, (1, gb * ln), 1) % ln) < S5_STATE

    def step(j, carry):
        xre, xim = carry
        cf = j
        cb = jnp.where(j < ncc, ncc - 1 - j, nc - 1 - (j - ncc))
        s_re = jnp.where(is_f, sre[pl.ds(cf, 1), :], sre[pl.ds(cb, 1), :])
        s_im = jnp.where(is_f, sim[pl.ds(cf, 1), :], sim[pl.ds(cb, 1), :])
        xfr[pl.ds(cf, 1), :] = xre
        xfi[pl.ds(cf, 1), :] = xim
        xbr[pl.ds(cb, 1), :] = xre
        xbi[pl.ds(cb, 1), :] = xim
        return are * xre - aim * xim + s_re, are * xim + aim * xre + s_im

    zero = jnp.zeros((1, gb * ln), F32)
    lax.fori_loop(0, nc, step, (zero, zero))

    is_f1 = lax.broadcasted_iota(jnp.int32, (1, ln), 1) < S5_STATE
    for g in range(gb):
        sl = slice(g * ln, (g + 1) * ln)
        xp = jnp.concatenate([jnp.where(is_f1, xfr[:, sl], xbr[:, sl]),
                              jnp.where(is_f1, xfi[:, sl], xbi[:, sl])], axis=1).astype(BF16)
        yg[g] += jnp.dot(xp, wo_ref[g], preferred_element_type=F32)

    for t in range(c_n):
        t_tile, t_off = divmod(t, per_tile)
        for tile in range(gb // per_tile):
            pieces = [yg[tile * per_tile + i][:, t_tile * 128:(t_tile + 1) * 128] for i in range(per_tile)]
            y_ref[pl.ds(t, nc, stride=c_n), tile * 128:(tile + 1) * 128] = regroup(pieces, t_off, range(per_tile))


def _s5_mix(proj, w1, wout, ar, ai, layer, n_ctx, gb):
    bsz, t, _ = proj.shape
    depth, g_n = w1.shape[:2]
    nc = t // S5_CHUNK
    ncc = n_ctx // S5_CHUNK
    ln = S5_LANES
    scr = pltpu.VMEM((nc, gb * ln), F32)
    return pl.pallas_call(
        functools.partial(_s5_kernel, gb=gb, nc=nc, ncc=ncc),
        grid=(bsz, g_n // gb),
        in_specs=[pl.BlockSpec((None, t, gb * S5_GROUP), lambda b, j: (b, 0, j)),
                  pl.BlockSpec((None, gb, S5_TILE, 2 * S5_TILE), lambda b, j: (layer, j, 0, 0)),
                  pl.BlockSpec((None, gb, S5_TILE, S5_TILE), lambda b, j: (layer, j, 0, 0)),
                  pl.BlockSpec((None, None, 1, gb * ln), lambda b, j: (layer, j, 0, 0)),
                  pl.BlockSpec((None, None, 1, gb * ln), lambda b, j: (layer, j, 0, 0))],
        out_specs=pl.BlockSpec((None, t, gb * S5_GROUP), lambda b, j: (b, 0, j)),
        out_shape=jax.ShapeDtypeStruct((bsz, t, g_n * S5_GROUP), F32),
        scratch_shapes=[pltpu.VMEM((gb, nc, S5_TILE), F32), scr, scr, scr, scr, scr, scr],
        compiler_params=_cparams("arbitrary", "arbitrary"),
    )(proj, w1, wout, ar.reshape(depth, g_n // gb, 1, gb * ln), ai.reshape(depth, g_n // gb, 1, gb * ln))


def _rope_tables(n_ctx, n_lat):
    pos = jnp.arange(n_lat)
    rows = (pos // GRID_W).astype(F32)
    cols = (pos % GRID_W).astype(F32)
    nf = RET_HEAD_DIM // 4
    freqs = ROPE_BASE ** (-jnp.arange(nf, dtype=F32) / nf)
    ar, ac = rows[:, None] * freqs, cols[:, None] * freqs
    cos = jnp.concatenate([jnp.cos(ar), jnp.cos(ar), jnp.cos(ac), jnp.cos(ac)], axis=-1)
    sin = jnp.concatenate([-jnp.sin(ar), jnp.sin(ar), -jnp.sin(ac), jnp.sin(ac)], axis=-1)
    cos = jnp.concatenate([jnp.ones((n_ctx, RET_HEAD_DIM), F32), cos], axis=0)
    sin = jnp.concatenate([jnp.zeros((n_ctx, RET_HEAD_DIM), F32), sin], axis=0)
    return cos, sin


def _ret_kernel(q_ref, k_ref, v_ref, cos_ref, sin_ref, lg_ref, o_ref, qs, ks, vs, ob, *, nchunk, ncc):
    c_n = RET_CHUNK
    nf = RET_HEAD_DIM // 4
    lane = lax.broadcasted_iota(jnp.int32, (1, RET_HEAD_DIM), 1)
    first = (lane % (2 * nf)) < nf

    def rope(x):
        partner = jnp.where(first, pltpu.roll(x, RET_HEAD_DIM - nf, 1), pltpu.roll(x, nf, 1))
        return x * cos_ref[...] + partner * sin_ref[...]

    qs[...] = rope(q_ref[...]).astype(BF16)
    ks[...] = rope(k_ref[...]) * (RET_HEAD_DIM ** -0.5)
    vs[...] = v_ref[...].astype(BF16)

    lg = jnp.minimum(lg_ref[...], -1e-6)
    lgf = lg[0:1, :]
    lgb = lg[1:2, :]
    ri = lax.broadcasted_iota(jnp.int32, (c_n, c_n), 0)
    ci = lax.broadcasted_iota(jnp.int32, (c_n, c_n), 1)
    diff = (ri - ci).astype(F32)
    low = diff >= 0.0
    decay = jnp.where(low, jnp.exp(jnp.where(low, diff, 0.0) * lgf), jnp.exp(jnp.where(low, 0.0, -diff) * lgb))
    pos = ri.astype(F32)
    xi_f = jnp.exp((pos + 1.0) * lgf)
    zeta_f = jnp.exp((c_n - 1.0 - pos) * lgf)
    gch_f = jnp.exp(c_n * lgf)
    xi_b = jnp.exp((c_n - pos) * lgb)
    zeta_b = jnp.exp(pos * lgb)
    gch_b = jnp.exp(c_n * lgb)
    nt = (((1,), (1,)), ((), ()))

    def step(j, carry):
        rf, rb = carry
        sl = pl.ds(pl.multiple_of(j * c_n, c_n), c_n)
        qc, kf, vc = qs[sl, :], ks[sl, :], vs[sl, :]
        s = lax.dot_general(qc, kf.astype(BF16), nt, preferred_element_type=F32)
        o = jnp.dot((s * decay).astype(BF16), vc, preferred_element_type=F32)
        o_ref[sl, :] = o + jnp.dot(qc, rf.astype(BF16), preferred_element_type=F32) * xi_f
        rf = gch_f * rf + jnp.dot((kf * zeta_f).T.astype(BF16), vc, preferred_element_type=F32)

        cb = jnp.where(j < ncc, ncc - 1 - j, nchunk - 1 - (j - ncc))
        sb = pl.ds(pl.multiple_of(cb * c_n, c_n), c_n)
        qb, kb, vb = qs[sb, :], ks[sb, :], vs[sb, :]
        ob[sb, :] = jnp.dot(qb, rb.astype(BF16), preferred_element_type=F32) * xi_b
        rb = gch_b * rb + jnp.dot((kb * zeta_b).T.astype(BF16), vb, preferred_element_type=F32)
        return rf, rb

    zero = jnp.zeros((c_n, c_n), F32)
    lax.fori_loop(0, nchunk, step, (zero, zero), unroll=2)
    o_ref[...] += ob[...]


def _retention(proj, cos, sin, lg_tab, n_ctx):
    bsz, t, _ = proj.shape
    hd = RET_HEAD_DIM
    nchunk = t // RET_CHUNK
    ncc = n_ctx // RET_CHUNK
    q0 = BRANCH_WIDTH // hd
    return pl.pallas_call(
        functools.partial(_ret_kernel, nchunk=nchunk, ncc=ncc),
        grid=(bsz, RET_HEADS),
        in_specs=[pl.BlockSpec((None, t, hd), lambda b, h: (b, 0, q0 + h)),
                  pl.BlockSpec((None, t, hd), lambda b, h: (b, 0, 2 * q0 + h)),
                  pl.BlockSpec((None, t, hd), lambda b, h: (b, 0, 3 * q0 + h)),
                  pl.BlockSpec((t, hd), lambda b, h: (0, 0)),
                  pl.BlockSpec((t, hd), lambda b, h: (0, 0)),
                  pl.BlockSpec((None, 8, hd), lambda b, h: (h, 0, 0))],
        out_specs=pl.BlockSpec((None, t, hd), lambda b, h: (b, 0, h)),
        out_shape=jax.ShapeDtypeStruct((bsz, t, BRANCH_WIDTH), F32),
        scratch_shapes=[pltpu.VMEM((t, hd), BF16), pltpu.VMEM((t, hd), F32), pltpu.VMEM((t, hd), BF16),
                        pltpu.VMEM((t, hd), F32)],
        compiler_params=_cparams("arbitrary", "arbitrary"),
    )(proj, proj, proj, cos, sin, lg_tab)


def _merge_kernel(*refs, ngb):
    ys_ref, or_ref, gr_ref = refs[0:3]
    gs_refs = refs[3:3 + ngb]
    gt_refs = refs[3 + ngb:3 + 2 * ngb]
    (x_ref, mod_ref, rg_ref, n2_ref, wglu_ref, wbs_ref, wbr_ref, wout_ref, wr_ref,
     xo_ref, h2_ref, lg_ref) = refs[3 + 2 * ngb:]
    w = BRANCH_WIDTH
    tm, d = x_ref.shape
    sub = 128
    nw = d // 256
    for r0 in range(0, tm, sub):
        rows = slice(r0, r0 + sub)
        z = _gelu_tanh(ys_ref[rows, :]).astype(BF16)
        zz = jnp.dot(z, wglu_ref[...], preferred_element_type=F32)
        s5b = jnp.dot((zz[:, :w] * jax.nn.sigmoid(zz[:, w:])).astype(BF16), wbs_ref[...],
                      preferred_element_type=F32)
        o = or_ref[rows, :]
        on = jnp.concatenate([_rms(o[:, h * RET_HEAD_DIM:(h + 1) * RET_HEAD_DIM]) for h in range(RET_HEADS)],
                             axis=1)
        on = on * rg_ref[...]
        rb = jnp.dot((on * _silu(gr_ref[rows, :])).astype(BF16), wbr_ref[...], preferred_element_type=F32)
        gate_s = jax.nn.sigmoid(jnp.concatenate([r[rows, :] for r in gs_refs], axis=1))
        gate_r = jax.nn.sigmoid(jnp.concatenate([r[rows, :] for r in gt_refs], axis=1))
        mix = jnp.dot((gate_s * s5b + gate_r * rb).astype(BF16), wout_ref[...], preferred_element_type=F32)
        xn = x_ref[rows, :] + mod_ref[2:3, :] * mix
        xo_ref[rows, :] = xn
        h2 = (_rms(xn) * n2_ref[...]) * (1.0 + mod_ref[4:5, :]) + mod_ref[3:4, :]
        _store_token_rows(h2_ref.at[pl.ds(r0 * nw, sub * nw), :], h2)
        lg_ref[:, rows] = lax.dot_general(wr_ref[...], h2, (((1,), (1,)), ((), ())), preferred_element_type=F32,
                                          precision=lax.Precision.HIGHEST)


def _merge(y_s5, o_ret, proj, xs, mods, ret_norm_g, norm2_g, wglu, wbs, wbr, wout, w_router, n_ctx, tm,
           latent_only):
    bsz, t, d = xs.shape
    w = BRANCH_WIDTH
    gw = 1024
    ngb = d // gw
    gs0 = 5 * w // gw
    ctx_tiles = n_ctx // tm
    skip = ctx_tiles if latent_only else 0
    row = lambda b, i: (b, i + skip, 0)
    const = lambda b, i: (0, 0)
    one = pl.Buffered(1)

    def gate_spec(k):
        return pl.BlockSpec((None, tm, gw), lambda b, i: (b, i + skip, gs0 + k))

    in_specs = ([pl.BlockSpec((None, tm, w), row), pl.BlockSpec((None, tm, w), row),
                 pl.BlockSpec((None, tm, w), lambda b, i: (b, i + skip, 4))]
                + [gate_spec(k) for k in range(ngb)] + [gate_spec(ngb + k) for k in range(ngb)]
                + [pl.BlockSpec((None, tm, d), row),
                   pl.BlockSpec((None, N_MOD, d), lambda b, i: (jnp.where(i + skip < ctx_tiles, bsz, b), 0, 0)),
                   pl.BlockSpec((1, w), const), pl.BlockSpec((1, d), const),
                   pl.BlockSpec((w, 2 * w), const, pipeline_mode=one),
                   pl.BlockSpec((w, d), const, pipeline_mode=one),
                   pl.BlockSpec((w, d), const, pipeline_mode=one),
                   pl.BlockSpec((d, d), const, pipeline_mode=one),
                   pl.BlockSpec((N_EXPERTS, d), const, pipeline_mode=one)])
    nt = d // 256
    tiles = t // tm - skip
    t_out = tiles * tm
    return pl.pallas_call(
        functools.partial(_merge_kernel, ngb=ngb),
        grid=(bsz, tiles),
        in_specs=in_specs,
        out_specs=[pl.BlockSpec((None, tm, d), lambda b, i: (b, i, 0)),
                   pl.BlockSpec((tm * nt, 128), lambda b, i: (b * tiles + i, 0)),
                   pl.BlockSpec((N_EXPERTS, tm), lambda b, i: (0, b * tiles + i))],
        out_shape=[jax.ShapeDtypeStruct((bsz, t_out, d), F32), jax.ShapeDtypeStruct((bsz * t_out * nt, 128), U32),
                   jax.ShapeDtypeStruct((N_EXPERTS, bsz * t_out), F32)],
        compiler_params=_cparams("arbitrary", "arbitrary"),
    )(y_s5, o_ret, proj, *([proj] * (2 * ngb)), xs, mods, ret_norm_g.reshape(1, w), norm2_g.reshape(1, d),
      wglu, wbs, wbr, wout, w_router.T)


def _route_kernel(lg_ref, bias_ref, tw_ref, ei_ref, rk_ref, cnt_ref, carry, *, tm):
    i = pl.program_id(0)
    ne = N_EXPERTS
    per_group = ne // N_GROUPS
    neg = -jnp.inf

    @pl.when(i == 0)
    def _():
        carry[...] = jnp.zeros_like(carry)

    scores = jax.nn.sigmoid(lg_ref[...])
    biased = scores + bias_ref[...]
    sub = lax.broadcasted_iota(jnp.int32, (per_group, tm), 0)
    gscore = []
    for g in range(N_GROUPS):
        blk = biased[g * per_group:(g + 1) * per_group, :]
        m1 = jnp.max(blk, axis=0, keepdims=True)
        first = jnp.min(jnp.where(blk == m1, sub, per_group), axis=0, keepdims=True)
        m2 = jnp.max(jnp.where(sub == first, neg, blk), axis=0, keepdims=True)
        gscore.append(m1 + m2)
    masked = []
    for g in range(N_GROUPS):
        beaten = jnp.zeros((1, tm), jnp.int32)
        for j in range(N_GROUPS):
            if j != g:
                wins = (gscore[j] >= gscore[g]) if j < g else (gscore[j] > gscore[g])
                beaten = beaten + wins.astype(jnp.int32)
        keep = beaten < TOPK_GROUPS
        masked.append(jnp.where(keep, biased[g * per_group:(g + 1) * per_group, :], neg))
    mv = jnp.concatenate(masked, axis=0)
    eidx = lax.broadcasted_iota(jnp.int32, (ne, tm), 0)
    beaten = jnp.zeros((ne, tm), jnp.int32)
    for j in range(ne):
        vj = mv[j:j + 1, :]
        wins = (vj > mv) | ((vj == mv) & (j < eidx))
        beaten = beaten + wins.astype(jnp.int32)
    sel = beaten < TOP_K
    sel_w = jnp.where(sel, scores, 0.0)
    wd = sel_w / jnp.sum(sel_w, axis=0, keepdims=True) * ROUTED_SCALE
    sel_b = sel.astype(BF16)
    r_i = lax.broadcasted_iota(jnp.int32, (ne, ne), 0)
    c_i = lax.broadcasted_iota(jnp.int32, (ne, ne), 1)
    slot = jnp.dot((c_i < r_i).astype(BF16), sel_b, preferred_element_type=F32)
    t_r = lax.broadcasted_iota(jnp.int32, (tm, tm), 0)
    t_c = lax.broadcasted_iota(jnp.int32, (tm, tm), 1)
    rank = jnp.dot(sel_b, (t_r < t_c).astype(BF16), preferred_element_type=F32) + carry[:, 0:1]
    carry[...] = carry[...] + jnp.sum(sel.astype(F32), axis=1, keepdims=True)
    cnt_ref[...] = carry[...]
    eidx_f = eidx.astype(F32)
    for k in range(TOP_K):
        mk = sel & (slot == float(k))
        tw_ref[k:k + 1, :] = jnp.sum(jnp.where(mk, wd, 0.0), axis=0, keepdims=True)
        ei_ref[k:k + 1, :] = jnp.sum(jnp.where(mk, eidx_f, 0.0), axis=0, keepdims=True).astype(jnp.int32)
        rk_ref[k:k + 1, :] = jnp.sum(jnp.where(mk, rank, 0.0), axis=0, keepdims=True).astype(jnp.int32)


def _route(logits_t, b_router, tm, batch, n):
    ne = logits_t.shape[0]
    col = lambda i: (0, i)
    tiles = n // tm
    return pl.pallas_call(
        functools.partial(_route_kernel, tm=tm),
        grid=(tiles,),
        in_specs=[pl.BlockSpec((ne, tm), lambda i: (0, batch * tiles + i)), pl.BlockSpec((ne, 1), lambda i: (0, 0))],
        out_specs=[pl.BlockSpec((TOP_K, tm), col), pl.BlockSpec((TOP_K, tm), col), pl.BlockSpec((TOP_K, tm), col),
                   pl.BlockSpec((ne, 128), lambda i: (0, 0))],
        out_shape=[jax.ShapeDtypeStruct((TOP_K, n), F32), jax.ShapeDtypeStruct((TOP_K, n), jnp.int32),
                   jax.ShapeDtypeStruct((TOP_K, n), jnp.int32), jax.ShapeDtypeStruct((ne, 128), F32)],
        scratch_shapes=[pltpu.VMEM((ne, 128), F32)],
        compiler_params=_cparams("arbitrary"),
    )(logits_t, b_router.reshape(ne, 1))


def _row_tok_kernel(lo_ref, hi_ref, pos_ref, out_ref, *, tb):
    j = pl.program_id(0)

    @pl.when(j == 0)
    def _():
        last = out_ref.shape[0] - 8

        def per_range(e, carry):
            lo = lo_ref[e]

            def fill(q, c):
                base = jnp.minimum(lo + q * 8, last)
                for r in range(8):
                    out_ref[base + r] = 0
                return c
            lax.fori_loop(0, (hi_ref[e] - lo + 7) // 8, fill, 0)
            return carry
        lax.fori_loop(0, lo_ref.shape[0], per_range, 0)

    def body(n, carry):
        for k in range(TOP_K):
            out_ref[pos_ref[k, n]] = j * tb + n
        return carry
    lax.fori_loop(0, tb, body, 0, unroll=8)


def _dispatch_plan(ei8, rk8, cnt, n_blocks, tb):
    n = ei8.shape[1]
    counts = cnt[:, 0].astype(jnp.int32)
    padded = (counts + MOE_ROWS - 1) // MOE_ROWS * MOE_ROWS
    pad_end = jnp.cumsum(padded)
    pad_start = pad_end - padded
    onehot = ei8[..., None] == jnp.arange(N_EXPERTS, dtype=jnp.int32)
    pos8 = rk8 + jnp.sum(jnp.where(onehot, pad_start, 0), axis=-1)
    blk_e = jnp.minimum(jnp.sum(jnp.arange(n_blocks, dtype=jnp.int32)[:, None] * MOE_ROWS >= pad_end[None, :],
                                axis=1), N_EXPERTS - 1).astype(jnp.int32)
    n_used = (pad_end[-1] // MOE_ROWS).astype(jnp.int32).reshape(1)
    run_end = jnp.sum(jnp.where(blk_e[:, None] == jnp.arange(N_EXPERTS, dtype=jnp.int32), pad_end // MOE_ROWS, 0),
                      axis=1)
    blk_at_run_end = jnp.sum(jnp.where(run_end[:, None] == jnp.arange(n_blocks, dtype=jnp.int32), blk_e[None, :], 0),
                             axis=1)
    nxt_e = jnp.where(run_end < n_used[0], blk_at_run_end, -1).astype(jnp.int32)
    grid_spec = pltpu.PrefetchScalarGridSpec(
        num_scalar_prefetch=2,
        grid=(n // tb,),
        in_specs=[pl.BlockSpec((TOP_K, tb), lambda j, lo, hi: (0, j), memory_space=pltpu.SMEM)],
        out_specs=pl.BlockSpec(memory_space=pltpu.SMEM))
    row_tok = pl.pallas_call(
        functools.partial(_row_tok_kernel, tb=tb),
        grid_spec=grid_spec,
        out_shape=jax.ShapeDtypeStruct((n_blocks * MOE_ROWS,), jnp.int32),
        compiler_params=_cparams("arbitrary"),
    )(jnp.concatenate([pad_start + counts, pad_end[-1:]]).astype(jnp.int32),
      jnp.concatenate([pad_end, jnp.full((1,), n_blocks * MOE_ROWS)]).astype(jnp.int32), pos8)
    return row_tok, pos8.reshape(-1), blk_e, nxt_e, n_used


def _expert_kernel(blk_e_ref, nxt_e_ref, nused_ref, tok_ref, h_ref, wg_hbm, wu_hbm, wd_hbm, y_ref,
                   xg, stage_g, stage_u, stage_d, sem, wgb, wub, wdb, *, layer):
    bm = MOE_ROWS
    i = pl.program_id(0)
    n_used = nused_ref[0]
    nt = xg.shape[1] // bm
    d = nt * 256
    slot = i % 2

    def weight_copies(e):
        return (pltpu.make_async_copy(wg_hbm.at[layer, e], stage_g, sem.at[0]),
                pltpu.make_async_copy(wu_hbm.at[layer, e], stage_u, sem.at[1]),
                pltpu.make_async_copy(wd_hbm.at[layer, e], stage_d, sem.at[2]))

    def gather_row(blk, slot_, r):
        tok = tok_ref[blk * bm + r]
        xg[slot_, pl.ds(r * nt, nt), :] = h_ref[pl.ds(pl.multiple_of(tok * nt, nt), nt), :]

    @pl.when(i < n_used)
    def _():
        e = blk_e_ref[i]

        @pl.when(i == 0)
        def _():
            for cp in weight_copies(e):
                cp.start()

            def first(r, carry):
                gather_row(0, 0, r)
                return carry
            lax.fori_loop(0, bm, first, 0, unroll=8)

        new_expert = jnp.logical_or(i == 0, e != blk_e_ref[jnp.maximum(i - 1, 0)])

        @pl.when(new_expert)
        def _():
            for cp in weight_copies(e):
                cp.wait()
            wgb[...] = stage_g[...].astype(BF16)
            wub[...] = stage_u[...].astype(BF16)
            wdb[...] = stage_d[...].astype(BF16)
            nxt = nxt_e_ref[i]

            @pl.when(nxt >= 0)
            def _():
                for cp in weight_copies(nxt):
                    cp.start()

        x = _load_token_rows(xg.at[slot], bm, d).astype(BF16)
        g = jnp.dot(x, wgb[...], preferred_element_type=F32)
        u = jnp.dot(x, wub[...], preferred_element_type=F32)
        _store_token_rows(y_ref, jnp.dot((_silu(g) * u).astype(BF16), wdb[...], preferred_element_type=F32))
        nxt_blk = jnp.minimum(i + 1, n_used - 1)
        for r in range(bm):
            gather_row(nxt_blk, 1 - slot, r)

    @pl.when(i >= n_used)
    def _():
        y_ref[...] = jnp.zeros_like(y_ref)


def _experts(h2c, row_tok, blk_e, nxt_e, n_used, w_gate, w_up, w_down, layer, n_blocks):
    d, de = w_gate.shape[-2:]
    nt = d // 256
    bm = MOE_ROWS
    hbm = pl.BlockSpec(memory_space=pl.ANY)
    grid_spec = pltpu.PrefetchScalarGridSpec(
        num_scalar_prefetch=4,
        grid=(n_blocks,),
        in_specs=[pl.BlockSpec(h2c.shape, lambda i, *_: (0, 0), pipeline_mode=pl.Buffered(1)), hbm, hbm, hbm],
        out_specs=pl.BlockSpec((bm * nt, 128), lambda i, *_: (i, 0)),
        scratch_shapes=[pltpu.VMEM((2, bm * nt, 128), U32),
                        pltpu.VMEM((d, de), F32), pltpu.VMEM((d, de), F32), pltpu.VMEM((de, d), F32),
                        pltpu.SemaphoreType.DMA((3,)),
                        pltpu.VMEM((d, de), BF16), pltpu.VMEM((d, de), BF16), pltpu.VMEM((de, d), BF16)])
    return pl.pallas_call(
        functools.partial(_expert_kernel, layer=layer),
        grid_spec=grid_spec,
        out_shape=jax.ShapeDtypeStruct((n_blocks * bm * nt, 128), U32),
        compiler_params=pltpu.CompilerParams(dimension_semantics=("arbitrary",),
                                             vmem_limit_bytes=EXPERT_VMEM_LIMIT),
    )(blk_e, nxt_e, n_used, row_tok, h2c, w_gate, w_up, w_down)


def _combine_kernel(pos_ref, ys_hbm, tw_ref, h2_ref, x_ref, mod_ref, wsg_ref, wsu_ref, wsd_ref, fg_ref,
                    o_ref, buf, sem, *, tmc, final):
    i = pl.program_id(0)
    n_tiles = pl.num_programs(0)
    n_tok = n_tiles * tmc
    slot = i % 2
    nt = buf.shape[2] // tmc
    d = nt * 256

    def row_copy(p, slot_, k, t):
        return pltpu.make_async_copy(ys_hbm.at[pl.ds(pl.multiple_of(p * nt, nt), nt), :],
                                     buf.at[slot_, k, pl.ds(pl.multiple_of(t * nt, nt), nt), :], sem.at[slot_])

    def issue(tile, slot_):
        def body(t, carry):
            for k in range(TOP_K):
                row_copy(pos_ref[k * n_tok + tile * tmc + t], slot_, k, t).start(priority=k % 2)
            return carry
        lax.fori_loop(0, tmc, body, 0)

    @pl.when(i == 0)
    def _():
        issue(0, 0)

    @pl.when(i + 1 < n_tiles)
    def _():
        issue(i + 1, 1 - slot)

    h = _load_token_rows(h2_ref, tmc, d).astype(BF16)
    sg = jnp.dot(h, wsg_ref[...], preferred_element_type=F32)
    su = jnp.dot(h, wsu_ref[...], preferred_element_type=F32)
    ff = jnp.dot((_silu(sg) * su).astype(BF16), wsd_ref[...], preferred_element_type=F32)

    def wait_body(t, carry):
        for k in range(TOP_K):
            row_copy(0, slot, k, t).wait()
        return carry
    lax.fori_loop(0, tmc, wait_body, 0)

    tw = tw_ref[...]
    routed = tw[:, 0:1] * _load_token_rows(buf.at[slot, 0], tmc, d)
    for k in range(1, TOP_K):
        routed = routed + tw[:, k:k + 1] * _load_token_rows(buf.at[slot, k], tmc, d)
    xn = x_ref[...] + mod_ref[5:6, :] * (routed + ff)
    if final:
        xn = _rms(xn) * fg_ref[...]
    o_ref[...] = xn


def _combine(ys, pos, top_w, h2c, xs_flat, mods, wsg, wsu, wsd, final_g, n_ctx, t_per_batch, final):
    d, de = wsg.shape
    nt = d // 256
    n = xs_flat.shape[0]
    tmc = COMBINE_TOKENS
    tiles = n // tmc
    bsz = n // t_per_batch
    tiles_per_batch = t_per_batch // tmc
    ctx_tiles = n_ctx // tmc
    const = lambda i, p: (0, 0)
    row = lambda i, p: (i, 0)

    def mod_map(i, p):
        return (jnp.where(i % tiles_per_batch < ctx_tiles, bsz, i // tiles_per_batch), 0, 0)

    grid_spec = pltpu.PrefetchScalarGridSpec(
        num_scalar_prefetch=1,
        grid=(tiles,),
        in_specs=[pl.BlockSpec(memory_space=pl.ANY),
                  pl.BlockSpec((tmc, TOP_K), row),
                  pl.BlockSpec((tmc * nt, 128), row),
                  pl.BlockSpec((tmc, d), row),
                  pl.BlockSpec((None, N_MOD, d), mod_map),
                  pl.BlockSpec((d, de), const), pl.BlockSpec((d, de), const), pl.BlockSpec((de, d), const),
                  pl.BlockSpec((1, d), const)],
        out_specs=pl.BlockSpec((tmc, d), row),
        scratch_shapes=[pltpu.VMEM((2, TOP_K, tmc * nt, 128), U32), pltpu.SemaphoreType.DMA((2,))])
    return pl.pallas_call(
        functools.partial(_combine_kernel, tmc=tmc, final=final),
        grid_spec=grid_spec,
        out_shape=jax.ShapeDtypeStruct(xs_flat.shape, F32),
        input_output_aliases={4: 0},
        compiler_params=_cparams("arbitrary"),
    )(pos, ys, top_w, h2c, xs_flat, mods, wsg, wsu, wsd, final_g.reshape(1, d))


def _pick_tile(n, candidates):
    for c in candidates:
        if n % c == 0:
            return c
    raise ValueError(f"no tile for {n}")


def kernel(x, c, ctx, c_ctx, w_mod, b_mod, norm1_g, norm2_g, final_g, w_in, s5_lam_re, s5_lam_im, s5_log_dt, s5_b_re, s5_b_im, s5_c_re, s5_c_im, s5_d, s5_w_glu, ret_log_decay, ret_norm_g, w_br_s5, w_br_ret, w_out, moe_router, moe_router_bias, moe_w_gate, moe_w_up, moe_w_down, sh_w_gate, sh_w_up, sh_w_down):
    bsz, n_lat, d = x.shape
    n_ctx = ctx.shape[1]
    depth = w_mod.shape[0]
    t = n_ctx + n_lat
    n_tok = bsz * t
    g_n = s5_b_re.shape[1]
    assert n_ctx % RET_CHUNK == 0 and n_lat % RET_CHUNK == 0 and bsz + 1 <= 8
    assert d % 1024 == 0 and n_tok % COMBINE_TOKENS == 0 and n_ctx % COMBINE_TOKENS == 0

    cin = jnp.concatenate([c, c_ctx[None], jnp.zeros((8 - bsz - 1, d), F32)], axis=0)
    mods_all = _modulation(cin, w_mod, b_mod).reshape(depth, 8, N_MOD, d)
    cos, sin = _rope_tables(n_ctx, n_lat)
    s5_w1, s5_wo, s5_ar, s5_ai = jax.vmap(_s5_tables)(s5_lam_re, s5_lam_im, s5_log_dt, s5_b_re, s5_b_im,
                                                      s5_c_re, s5_c_im, s5_d)
    xs = jnp.concatenate([ctx, x], axis=1)

    tm_in = _pick_tile(t, (544, 272, 256, 128))
    tm_merge = _pick_tile(n_ctx, (256, 128))
    gb = 128 // S5_GROUP

    for l in range(depth):
        mods = mods_all[l]
        proj = _in_proj(xs, norm1_g[l], mods, w_in[l].astype(BF16), n_ctx, tm_in)

        y_s5 = _s5_mix(proj, s5_w1, s5_wo, s5_ar, s5_ai, l, n_ctx, gb)

        lg_tab = jnp.broadcast_to(jnp.pad(ret_log_decay[l].T, ((0, 0), (0, 6)))[:, :, None],
                                  (RET_HEADS, 8, RET_HEAD_DIM))
        o_ret = _retention(proj, cos, sin, lg_tab, n_ctx)

        last = l == depth - 1
        t_l, ctx_l = (n_lat, 0) if last else (t, n_ctx)
        n_l = bsz * t_l
        xs, h2c, logits_t = _merge(y_s5, o_ret, proj, xs, mods, ret_norm_g[l], norm2_g[l],
                                   s5_w_glu[l].astype(BF16), w_br_s5[l].astype(BF16), w_br_ret[l].astype(BF16),
                                   w_out[l].astype(BF16), moe_router[l], n_ctx, tm_merge, latent_only=last)

        n_blocks = -(-(n_l * TOP_K + N_EXPERTS * (MOE_ROWS - 1)) // MOE_ROWS)
        tw8, ei8, rk8, cnt = _route(logits_t, moe_router_bias[l], _pick_tile(n_l, (256, 128)), 0, n_l)
        row_tok, pos, blk_e, nxt_e, n_used = _dispatch_plan(ei8, rk8, cnt, n_blocks,
                                                            _pick_tile(n_l, (2176, 2048, 1024, 512, 256, 128)))
        ys = _experts(h2c, row_tok, blk_e, nxt_e, n_used, moe_w_gate, moe_w_up, moe_w_down, l, n_blocks)
        xs = _combine(ys, pos, tw8.T, h2c, xs.reshape(n_l, d), mods, sh_w_gate[l].astype(BF16),
                      sh_w_up[l].astype(BF16), sh_w_down[l].astype(BF16), final_g, ctx_l, t_l,
                      final=last).reshape(bsz, t_l, d)

    return xs
```

```python
import functools

import jax
import jax.numpy as jnp
from jax import lax
from jax.experimental import pallas as pl
from jax.experimental.pallas import tpu as pltpu

F32 = jnp.float32
BF16 = jnp.bfloat16

NORM_EPS = 1e-6
N_MOD = 6
GRID_W = 64
ROPE_BASE = 10000.0

S5_GROUP = 16
S5_STATE = 64
S5_CHUNK = 16
S5_TILE = S5_CHUNK * S5_GROUP
S5_LANES = 2 * S5_STATE

RET_HEADS = 8
RET_HEAD_DIM = 128
RET_CHUNK = 128
BRANCH_WIDTH = RET_HEADS * RET_HEAD_DIM

N_EXPERTS = 64
TOP_K = 8
N_GROUPS = 8
TOPK_GROUPS = 4
ROUTED_SCALE = 2.5
MOE_ROWS = 256
COMBINE_TOKENS = 128

VMEM_LIMIT = 56 * 1024 * 1024
EXPERT_VMEM_LIMIT = 62 * 1024 * 1024


def _cparams(*sem):
    return pltpu.CompilerParams(dimension_semantics=sem, vmem_limit_bytes=VMEM_LIMIT)


def _silu(x):
    return x * jax.nn.sigmoid(x)


def _gelu_tanh(x):
    return 0.5 * x * (1.0 + jnp.tanh(0.7978845608028654 * (x + 0.044715 * x * x * x)))


def _rms(x):
    return x * lax.rsqrt(jnp.mean(x * x, axis=-1, keepdims=True) + NORM_EPS)


U32 = jnp.uint32
HIGH_HALF = 0xFFFF0000


def _bf16_bits(x):
    return lax.bitcast_convert_type(x.astype(BF16).astype(F32), U32)


def _store_token_rows(ref, val):
    rows, d = val.shape
    nw = d // 256
    for j in range(nw):
        lo = _bf16_bits(val[:, j * 128:(j + 1) * 128])
        hi = _bf16_bits(val[:, (j + nw) * 128:(j + nw + 1) * 128])
        ref[pl.ds(j, rows, stride=nw), :] = (hi & U32(HIGH_HALF)) | (lo >> 16)


def _load_token_rows(ref, rows, d):
    nw = d // 256
    words = [ref[pl.ds(j, rows, stride=nw), :] for j in range(nw)]
    lo = [lax.bitcast_convert_type(w << 16, F32) for w in words]
    hi = [lax.bitcast_convert_type(w & U32(HIGH_HALF), F32) for w in words]
    return jnp.concatenate(lo + hi, axis=1)


def _mod_kernel(c_ref, w_ref, b_ref, o_ref):
    a = _silu(c_ref[...]).astype(BF16)
    o_ref[...] = jnp.dot(a, w_ref[...].astype(BF16), preferred_element_type=F32) + b_ref[...]


def _modulation(cin, w_mod, b_mod):
    depth, d, n = w_mod.shape
    tn = 1024
    return pl.pallas_call(
        _mod_kernel,
        grid=(depth, n // tn),
        in_specs=[pl.BlockSpec((8, d), lambda l, j: (0, 0)),
                  pl.BlockSpec((None, d, tn), lambda l, j: (l, 0, j)),
                  pl.BlockSpec((None, 1, tn), lambda l, j: (l, 0, j))],
        out_specs=pl.BlockSpec((None, 8, tn), lambda l, j: (l, 0, j)),
        out_shape=jax.ShapeDtypeStruct((depth, 8, n), F32),
        compiler_params=_cparams("arbitrary", "arbitrary"),
    )(cin, w_mod, b_mod.reshape(depth, 1, n))


def _in_proj_kernel(x_ref, g_ref, ml_ref, mc_ref, w_ref, o_ref, h_scr, *, n_ctx, tm):
    i = pl.program_id(1)
    j = pl.program_id(2)
    sub = tm // 4 if tm % 64 == 0 else tm

    @pl.when(j == 0)
    def _():
        for r0 in range(0, tm, sub):
            rows = slice(r0, r0 + sub)
            y = _rms(x_ref[rows, :]) * g_ref[...]
            row = i * tm + r0 + lax.broadcasted_iota(jnp.int32, (sub, 1), 0)
            is_ctx = row < n_ctx
            shift = jnp.where(is_ctx, mc_ref[0:1, :], ml_ref[0:1, :])
            scale = jnp.where(is_ctx, mc_ref[1:2, :], ml_ref[1:2, :])
            h_scr[rows, :] = (y * (1.0 + scale) + shift).astype(BF16)

    o_ref[...] = jnp.dot(h_scr[...], w_ref[...], preferred_element_type=F32)


def _in_proj(xs, norm_g, mods, w_in_bf16, n_ctx, tm):
    bsz, t, d = xs.shape
    n = w_in_bf16.shape[1]
    tn = 1024
    return pl.pallas_call(
        functools.partial(_in_proj_kernel, n_ctx=n_ctx, tm=tm),
        grid=(bsz, t // tm, n // tn),
        in_specs=[pl.BlockSpec((None, tm, d), lambda b, i, j: (b, i, 0)),
                  pl.BlockSpec((1, d), lambda b, i, j: (0, 0)),
                  pl.BlockSpec((None, N_MOD, d), lambda b, i, j: (b, 0, 0)),
                  pl.BlockSpec((None, N_MOD, d), lambda b, i, j: (bsz, 0, 0)),
                  pl.BlockSpec((d, tn), lambda b, i, j: (0, j))],
        out_specs=pl.BlockSpec((None, tm, tn), lambda b, i, j: (b, i, j)),
        out_shape=jax.ShapeDtypeStruct((bsz, t, n), F32),
        scratch_shapes=[pltpu.VMEM((tm, d), BF16)],
        compiler_params=_cparams("arbitrary", "arbitrary", "arbitrary"),
    )(xs, norm_g.reshape(1, d), mods, mods, w_in_bf16)


def _s5_tables(lam_re, lam_im, log_dt, b_re, b_im, c_re, c_im, d_skip):
    hp = lax.Precision.HIGHEST
    g_n, p_n, h_n = b_re.shape
    c_n = S5_CHUNK
    dt = jnp.exp(log_dt)[..., None]
    lre = jnp.minimum(lam_re, -1e-4)
    steps = jnp.arange(c_n + 1, dtype=F32)[:, None, None, None]
    mag = jnp.exp(steps * (lre * dt))
    ang = steps * (lam_im * dt)
    pr, pi = mag * jnp.cos(ang), mag * jnp.sin(ang)
    a_re, a_im = pr[1], pi[1]
    den = lre * lre + lam_im * lam_im
    nr, ni = a_re - 1.0, a_im
    f_re = (nr * lre + ni * lam_im) / den
    f_im = (ni * lre - nr * lam_im) / den
    bb_re = f_re[..., None] * b_re - f_im[..., None] * b_im
    bb_im = f_re[..., None] * b_im + f_im[..., None] * b_re
    w_re = pr[..., None] * bb_re - pi[..., None] * bb_im
    w_im = pr[..., None] * bb_im + pi[..., None] * bb_re
    kl = (jnp.einsum('dghp,ndgpk->ndgkh', c_re, w_re[:c_n], precision=hp)
          - jnp.einsum('dghp,ndgpk->ndgkh', c_im, w_im[:c_n], precision=hp))
    lag0 = kl[0, 0] + kl[0, 1] + jnp.eye(h_n, dtype=F32)[None] * d_skip[:, None, :]
    lags = jnp.concatenate([kl[:0:-1, 1], lag0[None], kl[1:, 0]], axis=0)
    t_minus_s = jnp.arange(c_n)[None, :] - jnp.arange(c_n)[:, None] + (c_n - 1)
    ktoep = lags[t_minus_s].transpose(2, 0, 3, 1, 4).reshape(g_n, S5_TILE, S5_TILE)

    def st(w, idx, d):
        return w[idx, d].transpose(1, 0, 3, 2).reshape(g_n, S5_TILE, p_n)

    fwd_idx = c_n - 1 - jnp.arange(c_n)
    bwd_idx = jnp.arange(c_n)
    wst = jnp.concatenate([st(w_re, fwd_idx, 0), st(w_re, bwd_idx, 1),
                           st(w_im, fwd_idx, 0), st(w_im, bwd_idx, 1)], axis=-1)

    def out_rows(d, idx):
        cr = c_re[d][None]
        ci = c_im[d][None]
        er = pr[idx, d][:, :, None, :]
        ei = pi[idx, d][:, :, None, :]
        re = (cr * er - ci * ei).transpose(1, 3, 0, 2).reshape(g_n, p_n, S5_TILE)
        im = (cr * ei + ci * er).transpose(1, 3, 0, 2).reshape(g_n, p_n, S5_TILE)
        return re, im

    fo_re, fo_im = out_rows(0, jnp.arange(c_n) + 1)
    bo_re, bo_im = out_rows(1, c_n - jnp.arange(c_n))
    wout = jnp.concatenate([fo_re, bo_re, -fo_im, -bo_im], axis=1)
    w1 = jnp.concatenate([ktoep, wst], axis=-1).astype(BF16)
    ar = jnp.concatenate([pr[c_n, 0], pr[c_n, 1]], axis=-1)
    ai = jnp.concatenate([pi[c_n, 0], pi[c_n, 1]], axis=-1)
    return w1, wout.astype(BF16), ar, ai


def _s5_kernel(u_ref, w1_ref, wo_ref, ar_ref, ai_ref, y_ref, yg, sre, sim, xfr, xfi, xbr, xbi, *, gb, nc, ncc):
    ln = S5_LANES
    c_n = S5_CHUNK
    per_tile = 128 // S5_GROUP
    lane_grp = lax.broadcasted_iota(jnp.int32, (1, 128), 1) // S5_GROUP

    def regroup(pieces, src_off, dst_offs):
        acc = None
        for piece, dst in zip(pieces, dst_offs):
            shift = ((dst - src_off) * S5_GROUP) % 128
            rolled = pltpu.roll(piece, shift, 1) if shift else piece
            acc = rolled if acc is None else jnp.where(lane_grp == dst, rolled, acc)
        return acc

    xs = [u_ref[pl.ds(s, nc, stride=c_n), :] for s in range(c_n)]
    for g in range(gb):
        tile, off = divmod(g, per_tile)
        halves = []
        for half in range(c_n // per_tile):
            pieces = [xs[half * per_tile + i][:, tile * 128:(tile + 1) * 128] for i in range(per_tile)]
            halves.append(regroup(pieces, off, range(per_tile)))
        u_g = jnp.concatenate(halves, axis=1).astype(BF16)
        r = jnp.dot(u_g, w1_ref[g], preferred_element_type=F32)
        yg[g] = r[:, :S5_TILE]
        sre[:, g * ln:(g + 1) * ln] = r[:, S5_TILE:S5_TILE + ln]
        sim[:, g * ln:(g + 1) * ln] = r[:, S5_TILE + ln:]
    are = ar_ref[...]
    aim = ai_ref[...]
    is_f = (lax.broadcasted_iota(jnp.int32, (1, gb * ln), 1) % ln) < S5_STATE

    def step(j, carry):
        xre, xim = carry
        cf = j
        cb = jnp.where(j < ncc, ncc - 1 - j, nc - 1 - (j - ncc))
        s_re = jnp.where(is_f, sre[pl.ds(cf, 1), :], sre[pl.ds(cb, 1), :])
        s_im = jnp.where(is_f, sim[pl.ds(cf, 1), :], sim[pl.ds(cb, 1), :])
        xfr[pl.ds(cf, 1), :] = xre
        xfi[pl.ds(cf, 1), :] = xim
        xbr[pl.ds(cb, 1), :] = xre
        xbi[pl.ds(cb, 1), :] = xim
        return are * xre - aim * xim + s_re, are * xim + aim * xre + s_im

    zero = jnp.zeros((1, gb * ln), F32)
    lax.fori_loop(0, nc, step, (zero, zero), unroll=2)

    is_f1 = lax.broadcasted_iota(jnp.int32, (1, ln), 1) < S5_STATE
    for g in range(gb):
        sl = slice(g * ln, (g + 1) * ln)
        xp = jnp.concatenate([jnp.where(is_f1, xfr[:, sl], xbr[:, sl]),
                              jnp.where(is_f1, xfi[:, sl], xbi[:, sl])], axis=1).astype(BF16)
        yg[g] += jnp.dot(xp, wo_ref[g], preferred_element_type=F32)

    for t in range(c_n):
        t_tile, t_off = divmod(t, per_tile)
        for tile in range(gb // per_tile):
            pieces = [yg[tile * per_tile + i][:, t_tile * 128:(t_tile + 1) * 128] for i in range(per_tile)]
            y_ref[pl.ds(t, nc, stride=c_n), tile * 128:(tile + 1) * 128] = regroup(pieces, t_off, range(per_tile))


def _s5_mix(proj, w1, wout, ar, ai, layer, n_ctx, gb):
    bsz, t, _ = proj.shape
    depth, g_n = w1.shape[:2]
    nc = t // S5_CHUNK
    ncc = n_ctx // S5_CHUNK
    ln = S5_LANES
    scr = pltpu.VMEM((nc, gb * ln), F32)
    return pl.pallas_call(
        functools.partial(_s5_kernel, gb=gb, nc=nc, ncc=ncc),
        grid=(bsz, g_n // gb),
        in_specs=[pl.BlockSpec((None, t, gb * S5_GROUP), lambda b, j: (b, 0, j)),
                  pl.BlockSpec((None, gb, S5_TILE, 2 * S5_TILE), lambda b, j: (layer, j, 0, 0)),
                  pl.BlockSpec((None, gb, S5_TILE, S5_TILE), lambda b, j: (layer, j, 0, 0)),
                  pl.BlockSpec((None, None, 1, gb * ln), lambda b, j: (layer, j, 0, 0)),
                  pl.BlockSpec((None, None, 1, gb * ln), lambda b, j: (layer, j, 0, 0))],
        out_specs=pl.BlockSpec((None, t, gb * S5_GROUP), lambda b, j: (b, 0, j)),
        out_shape=jax.ShapeDtypeStruct((bsz, t, g_n * S5_GROUP), F32),
        scratch_shapes=[pltpu.VMEM((gb, nc, S5_TILE), F32), scr, scr, scr, scr, scr, scr],
        compiler_params=_cparams("arbitrary", "arbitrary"),
    )(proj, w1, wout, ar.reshape(depth, g_n // gb, 1, gb * ln), ai.reshape(depth, g_n // gb, 1, gb * ln))


def _rope_tables(n_ctx, n_lat):
    pos = jnp.arange(n_lat)
    rows = (pos // GRID_W).astype(F32)
    cols = (pos % GRID_W).astype(F32)
    nf = RET_HEAD_DIM // 4
    freqs = ROPE_BASE ** (-jnp.arange(nf, dtype=F32) / nf)
    ar, ac = rows[:, None] * freqs, cols[:, None] * freqs
    cos = jnp.concatenate([jnp.cos(ar), jnp.cos(ar), jnp.cos(ac), jnp.cos(ac)], axis=-1)
    sin = jnp.concatenate([-jnp.sin(ar), jnp.sin(ar), -jnp.sin(ac), jnp.sin(ac)], axis=-1)
    cos = jnp.concatenate([jnp.ones((n_ctx, RET_HEAD_DIM), F32), cos], axis=0)
    sin = jnp.concatenate([jnp.zeros((n_ctx, RET_HEAD_DIM), F32), sin], axis=0)
    return cos, sin


def _ret_kernel(q_ref, k_ref, v_ref, cos_ref, sin_ref, lg_ref, o_ref, qs, ks, vs, ob, *, nchunk, ncc):
    c_n = RET_CHUNK
    nf = RET_HEAD_DIM // 4
    lane = lax.broadcasted_iota(jnp.int32, (1, RET_HEAD_DIM), 1)
    first = (lane % (2 * nf)) < nf

    def rope(x):
        partner = jnp.where(first, pltpu.roll(x, RET_HEAD_DIM - nf, 1), pltpu.roll(x, nf, 1))
        return x * cos_ref[...] + partner * sin_ref[...]

    qs[...] = rope(q_ref[...]).astype(BF16)
    ks[...] = rope(k_ref[...]) * (RET_HEAD_DIM ** -0.5)
    vs[...] = v_ref[...].astype(BF16)

    lg = jnp.minimum(lg_ref[...], -1e-6)
    lgf = lg[0:1, :]
    lgb = lg[1:2, :]
    ri = lax.broadcasted_iota(jnp.int32, (c_n, c_n), 0)
    ci = lax.broadcasted_iota(jnp.int32, (c_n, c_n), 1)
    diff = (ri - ci).astype(F32)
    low = diff >= 0.0
    decay = jnp.where(low, jnp.exp(jnp.where(low, diff, 0.0) * lgf), jnp.exp(jnp.where(low, 0.0, -diff) * lgb))
    pos = ri.astype(F32)
    xi_f = jnp.exp((pos + 1.0) * lgf)
    zeta_f = jnp.exp((c_n - 1.0 - pos) * lgf)
    gch_f = jnp.exp(c_n * lgf)
    xi_b = jnp.exp((c_n - pos) * lgb)
    zeta_b = jnp.exp(pos * lgb)
    gch_b = jnp.exp(c_n * lgb)
    nt = (((1,), (1,)), ((), ()))

    def step(j, carry):
        rf, rb = carry
        sl = pl.ds(pl.multiple_of(j * c_n, c_n), c_n)
        qc, kf, vc = qs[sl, :], ks[sl, :], vs[sl, :]
        s = lax.dot_general(qc, kf.astype(BF16), nt, preferred_element_type=F32)
        o = jnp.dot((s * decay).astype(BF16), vc, preferred_element_type=F32)
        o_ref[sl, :] = o + jnp.dot(qc, rf.astype(BF16), preferred_element_type=F32) * xi_f
        rf = gch_f * rf + jnp.dot((kf * zeta_f).T.astype(BF16), vc, preferred_element_type=F32)

        cb = jnp.where(j < ncc, ncc - 1 - j, nchunk - 1 - (j - ncc))
        sb = pl.ds(pl.multiple_of(cb * c_n, c_n), c_n)
        qb, kb, vb = qs[sb, :], ks[sb, :], vs[sb, :]
        ob[sb, :] = jnp.dot(qb, rb.astype(BF16), preferred_element_type=F32) * xi_b
        rb = gch_b * rb + jnp.dot((kb * zeta_b).T.astype(BF16), vb, preferred_element_type=F32)
        return rf, rb

    zero = jnp.zeros((c_n, c_n), F32)
    lax.fori_loop(0, nchunk, step, (zero, zero), unroll=2)
    o_ref[...] += ob[...]


def _retention(proj, cos, sin, lg_tab, n_ctx):
    bsz, t, _ = proj.shape
    hd = RET_HEAD_DIM
    nchunk = t // RET_CHUNK
    ncc = n_ctx // RET_CHUNK
    q0 = BRANCH_WIDTH // hd
    return pl.pallas_call(
        functools.partial(_ret_kernel, nchunk=nchunk, ncc=ncc),
        grid=(bsz, RET_HEADS),
        in_specs=[pl.BlockSpec((None, t, hd), lambda b, h: (b, 0, q0 + h)),
                  pl.BlockSpec((None, t, hd), lambda b, h: (b, 0, 2 * q0 + h)),
                  pl.BlockSpec((None, t, hd), lambda b, h: (b, 0, 3 * q0 + h)),
                  pl.BlockSpec((t, hd), lambda b, h: (0, 0)),
                  pl.BlockSpec((t, hd), lambda b, h: (0, 0)),
                  pl.BlockSpec((None, 8, hd), lambda b, h: (h, 0, 0))],
        out_specs=pl.BlockSpec((None, t, hd), lambda b, h: (b, 0, h)),
        out_shape=jax.ShapeDtypeStruct((bsz, t, BRANCH_WIDTH), F32),
        scratch_shapes=[pltpu.VMEM((t, hd), BF16), pltpu.VMEM((t, hd), F32), pltpu.VMEM((t, hd), BF16),
                        pltpu.VMEM((t, hd), F32)],
        compiler_params=_cparams("arbitrary", "arbitrary"),
    )(proj, proj, proj, cos, sin, lg_tab)


def _merge_kernel(*refs, ngb):
    ys_ref, or_ref, gr_ref = refs[0:3]
    gs_refs = refs[3:3 + ngb]
    gt_refs = refs[3 + ngb:3 + 2 * ngb]
    (x_ref, mod_ref, rg_ref, n2_ref, wglu_ref, wbs_ref, wbr_ref, wout_ref, wr_ref,
     xo_ref, h2_ref, lg_ref) = refs[3 + 2 * ngb:]
    w = BRANCH_WIDTH
    tm, d = x_ref.shape
    sub = 128
    nw = d // 256
    for r0 in range(0, tm, sub):
        rows = slice(r0, r0 + sub)
        z = _gelu_tanh(ys_ref[rows, :]).astype(BF16)
        zz = jnp.dot(z, wglu_ref[...], preferred_element_type=F32)
        s5b = jnp.dot((zz[:, :w] * jax.nn.sigmoid(zz[:, w:])).astype(BF16), wbs_ref[...],
                      preferred_element_type=F32)
        o = or_ref[rows, :]
        on = jnp.concatenate([_rms(o[:, h * RET_HEAD_DIM:(h + 1) * RET_HEAD_DIM]) for h in range(RET_HEADS)],
                             axis=1)
        on = on * rg_ref[...]
        rb = jnp.dot((on * _silu(gr_ref[rows, :])).astype(BF16), wbr_ref[...], preferred_element_type=F32)
        gate_s = jax.nn.sigmoid(jnp.concatenate([r[rows, :] for r in gs_refs], axis=1))
        gate_r = jax.nn.sigmoid(jnp.concatenate([r[rows, :] for r in gt_refs], axis=1))
        mix = jnp.dot((gate_s * s5b + gate_r * rb).astype(BF16), wout_ref[...], preferred_element_type=F32)
        xn = x_ref[rows, :] + mod_ref[2:3, :] * mix
        xo_ref[rows, :] = xn
        h2 = (_rms(xn) * n2_ref[...]) * (1.0 + mod_ref[4:5, :]) + mod_ref[3:4, :]
        _store_token_rows(h2_ref.at[pl.ds(r0 * nw, sub * nw), :], h2)
        lg_ref[:, rows] = lax.dot_general(wr_ref[...], h2, (((1,), (1,)), ((), ())), preferred_element_type=F32,
                                          precision=lax.Precision.HIGHEST)


def _merge(y_s5, o_ret, proj, xs, mods, ret_norm_g, norm2_g, wglu, wbs, wbr, wout, w_router, n_ctx, tm,
           latent_only):
    bsz, t, d = xs.shape
    w = BRANCH_WIDTH
    gw = 1024
    ngb = d // gw
    gs0 = 5 * w // gw
    ctx_tiles = n_ctx // tm
    skip = ctx_tiles if latent_only else 0
    row = lambda b, i: (b, i + skip, 0)
    const = lambda b, i: (0, 0)
    one = pl.Buffered(1)

    def gate_spec(k):
        return pl.BlockSpec((None, tm, gw), lambda b, i: (b, i + skip, gs0 + k))

    in_specs = ([pl.BlockSpec((None, tm, w), row), pl.BlockSpec((None, tm, w), row),
                 pl.BlockSpec((None, tm, w), lambda b, i: (b, i + skip, 4))]
                + [gate_spec(k) for k in range(ngb)] + [gate_spec(ngb + k) for k in range(ngb)]
                + [pl.BlockSpec((None, tm, d), row),
                   pl.BlockSpec((None, N_MOD, d), lambda b, i: (jnp.where(i + skip < ctx_tiles, bsz, b), 0, 0)),
                   pl.BlockSpec((1, w), const), pl.BlockSpec((1, d), const),
                   pl.BlockSpec((w, 2 * w), const, pipeline_mode=one),
                   pl.BlockSpec((w, d), const, pipeline_mode=one),
                   pl.BlockSpec((w, d), const, pipeline_mode=one),
                   pl.BlockSpec((d, d), const, pipeline_mode=one),
                   pl.BlockSpec((N_EXPERTS, d), const, pipeline_mode=one)])
    nt = d // 256
    tiles = t // tm - skip
    t_out = tiles * tm
    return pl.pallas_call(
        functools.partial(_merge_kernel, ngb=ngb),
        grid=(bsz, tiles),
        in_specs=in_specs,
        out_specs=[pl.BlockSpec((None, tm, d), lambda b, i: (b, i, 0)),
                   pl.BlockSpec((tm * nt, 128), lambda b, i: (b * tiles + i, 0)),
                   pl.BlockSpec((N_EXPERTS, tm), lambda b, i: (0, b * tiles + i))],
        out_shape=[jax.ShapeDtypeStruct((bsz, t_out, d), F32), jax.ShapeDtypeStruct((bsz * t_out * nt, 128), U32),
                   jax.ShapeDtypeStruct((N_EXPERTS, bsz * t_out), F32)],
        compiler_params=_cparams("arbitrary", "arbitrary"),
    )(y_s5, o_ret, proj, *([proj] * (2 * ngb)), xs, mods, ret_norm_g.reshape(1, w), norm2_g.reshape(1, d),
      wglu, wbs, wbr, wout, w_router.T)


def _route_kernel(lg_ref, bias_ref, tw_ref, ei_ref, rk_ref, cnt_ref, carry, *, tm):
    i = pl.program_id(0)
    ne = N_EXPERTS
    per_group = ne // N_GROUPS
    neg = -jnp.inf

    @pl.when(i == 0)
    def _():
        carry[...] = jnp.zeros_like(carry)

    scores = jax.nn.sigmoid(lg_ref[...])
    biased = scores + bias_ref[...]
    sub = lax.broadcasted_iota(jnp.int32, (per_group, tm), 0)
    gscore = []
    for g in range(N_GROUPS):
        blk = biased[g * per_group:(g + 1) * per_group, :]
        m1 = jnp.max(blk, axis=0, keepdims=True)
        first = jnp.min(jnp.where(blk == m1, sub, per_group), axis=0, keepdims=True)
        m2 = jnp.max(jnp.where(sub == first, neg, blk), axis=0, keepdims=True)
        gscore.append(m1 + m2)
    masked = []
    for g in range(N_GROUPS):
        beaten = jnp.zeros((1, tm), jnp.int32)
        for j in range(N_GROUPS):
            if j != g:
                wins = (gscore[j] >= gscore[g]) if j < g else (gscore[j] > gscore[g])
                beaten = beaten + wins.astype(jnp.int32)
        keep = beaten < TOPK_GROUPS
        masked.append(jnp.where(keep, biased[g * per_group:(g + 1) * per_group, :], neg))
    mv = jnp.concatenate(masked, axis=0)
    eidx = lax.broadcasted_iota(jnp.int32, (ne, tm), 0)
    beaten = jnp.zeros((ne, tm), jnp.int32)
    for j in range(ne):
        vj = mv[j:j + 1, :]
        wins = (vj > mv) | ((vj == mv) & (j < eidx))
        beaten = beaten + wins.astype(jnp.int32)
    sel = beaten < TOP_K
    sel_w = jnp.where(sel, scores, 0.0)
    wd = sel_w / jnp.sum(sel_w, axis=0, keepdims=True) * ROUTED_SCALE
    sel_b = sel.astype(BF16)
    r_i = lax.broadcasted_iota(jnp.int32, (ne, ne), 0)
    c_i = lax.broadcasted_iota(jnp.int32, (ne, ne), 1)
    slot = jnp.dot((c_i < r_i).astype(BF16), sel_b, preferred_element_type=F32)
    t_r = lax.broadcasted_iota(jnp.int32, (tm, tm), 0)
    t_c = lax.broadcasted_iota(jnp.int32, (tm, tm), 1)
    rank = jnp.dot(sel_b, (t_r < t_c).astype(BF16), preferred_element_type=F32) + carry[:, 0:1]
    carry[...] = carry[...] + jnp.sum(sel.astype(F32), axis=1, keepdims=True)
    cnt_ref[...] = carry[...]
    eidx_f = eidx.astype(F32)
    for k in range(TOP_K):
        mk = sel & (slot == float(k))
        tw_ref[k:k + 1, :] = jnp.sum(jnp.where(mk, wd, 0.0), axis=0, keepdims=True)
        ei_ref[k:k + 1, :] = jnp.sum(jnp.where(mk, eidx_f, 0.0), axis=0, keepdims=True).astype(jnp.int32)
        rk_ref[k:k + 1, :] = jnp.sum(jnp.where(mk, rank, 0.0), axis=0, keepdims=True).astype(jnp.int32)


def _route(logits_t, b_router, tm, batch, n):
    ne = logits_t.shape[0]
    col = lambda i: (0, i)
    tiles = n // tm
    return pl.pallas_call(
        functools.partial(_route_kernel, tm=tm),
        grid=(tiles,),
        in_specs=[pl.BlockSpec((ne, tm), lambda i: (0, batch * tiles + i)), pl.BlockSpec((ne, 1), lambda i: (0, 0))],
        out_specs=[pl.BlockSpec((TOP_K, tm), col), pl.BlockSpec((TOP_K, tm), col), pl.BlockSpec((TOP_K, tm), col),
                   pl.BlockSpec((ne, 128), lambda i: (0, 0))],
        out_shape=[jax.ShapeDtypeStruct((TOP_K, n), F32), jax.ShapeDtypeStruct((TOP_K, n), jnp.int32),
                   jax.ShapeDtypeStruct((TOP_K, n), jnp.int32), jax.ShapeDtypeStruct((ne, 128), F32)],
        scratch_shapes=[pltpu.VMEM((ne, 128), F32)],
        compiler_params=_cparams("arbitrary"),
    )(logits_t, b_router.reshape(ne, 1))


def _row_tok_kernel(lo_ref, hi_ref, pos_ref, out_ref, *, tb):
    j = pl.program_id(0)

    @pl.when(j == 0)
    def _():
        last = out_ref.shape[0] - 8

        def per_range(e, carry):
            lo = lo_ref[e]

            def fill(q, c):
                base = jnp.minimum(lo + q * 8, last)
                for r in range(8):
                    out_ref[base + r] = 0
                return c
            lax.fori_loop(0, (hi_ref[e] - lo + 7) // 8, fill, 0)
            return carry
        lax.fori_loop(0, lo_ref.shape[0], per_range, 0)

    def body(n, carry):
        for k in range(TOP_K):
            out_ref[pos_ref[k, n]] = j * tb + n
        return carry
    lax.fori_loop(0, tb, body, 0, unroll=8)


def _dispatch_plan(ei8, rk8, cnt, n_blocks, tb):
    n = ei8.shape[1]
    counts = cnt[:, 0].astype(jnp.int32)
    padded = (counts + MOE_ROWS - 1) // MOE_ROWS * MOE_ROWS
    pad_end = jnp.cumsum(padded)
    pad_start = pad_end - padded
    onehot = ei8[..., None] == jnp.arange(N_EXPERTS, dtype=jnp.int32)
    pos8 = rk8 + jnp.sum(jnp.where(onehot, pad_start, 0), axis=-1)
    blk_e = jnp.minimum(jnp.sum(jnp.arange(n_blocks, dtype=jnp.int32)[:, None] * MOE_ROWS >= pad_end[None, :],
                                axis=1), N_EXPERTS - 1).astype(jnp.int32)
    n_used = (pad_end[-1] // MOE_ROWS).astype(jnp.int32).reshape(1)
    run_end = jnp.sum(jnp.where(blk_e[:, None] == jnp.arange(N_EXPERTS, dtype=jnp.int32), pad_end // MOE_ROWS, 0),
                      axis=1)
    blk_at_run_end = jnp.sum(jnp.where(run_end[:, None] == jnp.arange(n_blocks, dtype=jnp.int32), blk_e[None, :], 0),
                             axis=1)
    nxt_e = jnp.where(run_end < n_used[0], blk_at_run_end, -1).astype(jnp.int32)
    grid_spec = pltpu.PrefetchScalarGridSpec(
        num_scalar_prefetch=2,
        grid=(n // tb,),
        in_specs=[pl.BlockSpec((TOP_K, tb), lambda j, lo, hi: (0, j), memory_space=pltpu.SMEM)],
        out_specs=pl.BlockSpec(memory_space=pltpu.SMEM))
    row_tok = pl.pallas_call(
        functools.partial(_row_tok_kernel, tb=tb),
        grid_spec=grid_spec,
        out_shape=jax.ShapeDtypeStruct((n_blocks * MOE_ROWS,), jnp.int32),
        compiler_params=_cparams("arbitrary"),
    )(jnp.concatenate([pad_start + counts, pad_end[-1:]]).astype(jnp.int32),
      jnp.concatenate([pad_end, jnp.full((1,), n_blocks * MOE_ROWS)]).astype(jnp.int32), pos8)
    return row_tok, pos8.reshape(-1), blk_e, nxt_e, n_used


def _expert_kernel(blk_e_ref, nxt_e_ref, nused_ref, tok_ref, h_ref, wg_hbm, wu_hbm, wd_hbm, y_ref,
                   xg, stage_g, stage_u, stage_d, sem, wgb, wub, wdb, *, layer):
    bm = MOE_ROWS
    i = pl.program_id(0)
    n_used = nused_ref[0]
    nt = xg.shape[1] // bm
    d = nt * 256
    slot = i % 2

    def weight_copies(e):
        return (pltpu.make_async_copy(wg_hbm.at[layer, e], stage_g, sem.at[0]),
                pltpu.make_async_copy(wu_hbm.at[layer, e], stage_u, sem.at[1]),
                pltpu.make_async_copy(wd_hbm.at[layer, e], stage_d, sem.at[2]))

    def gather_row(blk, slot_, r):
        tok = tok_ref[blk * bm + r]
        xg[slot_, pl.ds(r * nt, nt), :] = h_ref[pl.ds(pl.multiple_of(tok * nt, nt), nt), :]

    @pl.when(i < n_used)
    def _():
        e = blk_e_ref[i]

        @pl.when(i == 0)
        def _():
            for cp in weight_copies(e):
                cp.start()

            def first(r, carry):
                gather_row(0, 0, r)
                return carry
            lax.fori_loop(0, bm, first, 0, unroll=8)

        new_expert = jnp.logical_or(i == 0, e != blk_e_ref[jnp.maximum(i - 1, 0)])

        @pl.when(new_expert)
        def _():
            for cp in weight_copies(e):
                cp.wait()
            wgb[...] = stage_g[...].astype(BF16)
            wub[...] = stage_u[...].astype(BF16)
            wdb[...] = stage_d[...].astype(BF16)
            nxt = nxt_e_ref[i]

            @pl.when(nxt >= 0)
            def _():
                for cp in weight_copies(nxt):
                    cp.start()

        x = _load_token_rows(xg.at[slot], bm, d).astype(BF16)
        g = jnp.dot(x, wgb[...], preferred_element_type=F32)
        u = jnp.dot(x, wub[...], preferred_element_type=F32)
        _store_token_rows(y_ref, jnp.dot((_silu(g) * u).astype(BF16), wdb[...], preferred_element_type=F32))
        nxt_blk = jnp.minimum(i + 1, n_used - 1)
        for r in range(bm):
            gather_row(nxt_blk, 1 - slot, r)

    @pl.when(i >= n_used)
    def _():
        y_ref[...] = jnp.zeros_like(y_ref)


def _experts(h2c, row_tok, blk_e, nxt_e, n_used, w_gate, w_up, w_down, layer, n_blocks):
    d, de = w_gate.shape[-2:]
    nt = d // 256
    bm = MOE_ROWS
    hbm = pl.BlockSpec(memory_space=pl.ANY)
    grid_spec = pltpu.PrefetchScalarGridSpec(
        num_scalar_prefetch=4,
        grid=(n_blocks,),
        in_specs=[pl.BlockSpec(h2c.shape, lambda i, *_: (0, 0), pipeline_mode=pl.Buffered(1)), hbm, hbm, hbm],
        out_specs=pl.BlockSpec((bm * nt, 128), lambda i, *_: (i, 0)),
        scratch_shapes=[pltpu.VMEM((2, bm * nt, 128), U32),
                        pltpu.VMEM((d, de), F32), pltpu.VMEM((d, de), F32), pltpu.VMEM((de, d), F32),
                        pltpu.SemaphoreType.DMA((3,)),
                        pltpu.VMEM((d, de), BF16), pltpu.VMEM((d, de), BF16), pltpu.VMEM((de, d), BF16)])
    return pl.pallas_call(
        functools.partial(_expert_kernel, layer=layer),
        grid_spec=grid_spec,
        out_shape=jax.ShapeDtypeStruct((n_blocks * bm * nt, 128), U32),
        compiler_params=pltpu.CompilerParams(dimension_semantics=("arbitrary",),
                                             vmem_limit_bytes=EXPERT_VMEM_LIMIT),
    )(blk_e, nxt_e, n_used, row_tok, h2c, w_gate, w_up, w_down)


def _combine_kernel(pos_ref, ys_hbm, tw_ref, h2_ref, x_ref, mod_ref, wsg_ref, wsu_ref, wsd_ref, fg_ref,
                    o_ref, buf, sem, *, tmc, final):
    i = pl.program_id(0)
    n_tiles = pl.num_programs(0)
    n_tok = n_tiles * tmc
    slot = i % 2
    nt = buf.shape[2] // tmc
    d = nt * 256

    def row_copy(p, slot_, k, t):
        return pltpu.make_async_copy(ys_hbm.at[pl.ds(pl.multiple_of(p * nt, nt), nt), :],
                                     buf.at[slot_, k, pl.ds(pl.multiple_of(t * nt, nt), nt), :], sem.at[slot_])

    def issue(tile, slot_):
        def body(t, carry):
            for k in range(TOP_K):
                row_copy(pos_ref[k * n_tok + tile * tmc + t], slot_, k, t).start(priority=k % 2)
            return carry
        lax.fori_loop(0, tmc, body, 0)

    @pl.when(i == 0)
    def _():
        issue(0, 0)

    @pl.when(i + 1 < n_tiles)
    def _():
        issue(i + 1, 1 - slot)

    h = _load_token_rows(h2_ref, tmc, d).astype(BF16)
    sg = jnp.dot(h, wsg_ref[...], preferred_element_type=F32)
    su = jnp.dot(h, wsu_ref[...], preferred_element_type=F32)
    ff = jnp.dot((_silu(sg) * su).astype(BF16), wsd_ref[...], preferred_element_type=F32)

    def wait_body(t, carry):
        for k in range(TOP_K):
            row_copy(0, slot, k, t).wait()
        return carry
    lax.fori_loop(0, tmc, wait_body, 0)

    tw = tw_ref[...]
    routed = tw[:, 0:1] * _load_token_rows(buf.at[slot, 0], tmc, d)
    for k in range(1, TOP_K):
        routed = routed + tw[:, k:k + 1] * _load_token_rows(buf.at[slot, k], tmc, d)
    xn = x_ref[...] + mod_ref[5:6, :] * (routed + ff)
    if final:
        xn = _rms(xn) * fg_ref[...]
    o_ref[...] = xn


def _combine(ys, pos, top_w, h2c, xs_flat, mods, wsg, wsu, wsd, final_g, n_ctx, t_per_batch, final):
    d, de = wsg.shape
    nt = d // 256
    n = xs_flat.shape[0]
    tmc = COMBINE_TOKENS
    tiles = n // tmc
    bsz = n // t_per_batch
    tiles_per_batch = t_per_batch // tmc
    ctx_tiles = n_ctx // tmc
    const = lambda i, p: (0, 0)
    row = lambda i, p: (i, 0)

    def mod_map(i, p):
        return (jnp.where(i % tiles_per_batch < ctx_tiles, bsz, i // tiles_per_batch), 0, 0)

    grid_spec = pltpu.PrefetchScalarGridSpec(
        num_scalar_prefetch=1,
        grid=(tiles,),
        in_specs=[pl.BlockSpec(memory_space=pl.ANY),
                  pl.BlockSpec((tmc, TOP_K), row),
                  pl.BlockSpec((tmc * nt, 128), row),
                  pl.BlockSpec((tmc, d), row),
                  pl.BlockSpec((None, N_MOD, d), mod_map),
                  pl.BlockSpec((d, de), const), pl.BlockSpec((d, de), const), pl.BlockSpec((de, d), const),
                  pl.BlockSpec((1, d), const)],
        out_specs=pl.BlockSpec((tmc, d), row),
        scratch_shapes=[pltpu.VMEM((2, TOP_K, tmc * nt, 128), U32), pltpu.SemaphoreType.DMA((2,))])
    return pl.pallas_call(
        functools.partial(_combine_kernel, tmc=tmc, final=final),
        grid_spec=grid_spec,
        out_shape=jax.ShapeDtypeStruct(xs_flat.shape, F32),
        input_output_aliases={4: 0},
        compiler_params=_cparams("arbitrary"),
    )(pos, ys, top_w, h2c, xs_flat, mods, wsg, wsu, wsd, final_g.reshape(1, d))


def _pick_tile(n, candidates):
    for c in candidates:
        if n % c == 0:
            return c
    raise ValueError(f"no tile for {n}")


def kernel(x, c, ctx, c_ctx, w_mod, b_mod, norm1_g, norm2_g, final_g, w_in, s5_lam_re, s5_lam_im, s5_log_dt, s5_b_re, s5_b_im, s5_c_re, s5_c_im, s5_d, s5_w_glu, ret_log_decay, ret_norm_g, w_br_s5, w_br_ret, w_out, moe_router, moe_router_bias, moe_w_gate, moe_w_up, moe_w_down, sh_w_gate, sh_w_up, sh_w_down):
    bsz, n_lat, d = x.shape
    n_ctx = ctx.shape[1]
    depth = w_mod.shape[0]
    t = n_ctx + n_lat
    n_tok = bsz * t
    g_n = s5_b_re.shape[1]
    assert n_ctx % RET_CHUNK == 0 and n_lat % RET_CHUNK == 0 and bsz + 1 <= 8
    assert d % 1024 == 0 and n_tok % COMBINE_TOKENS == 0 and n_ctx % COMBINE_TOKENS == 0

    cin = jnp.concatenate([c, c_ctx[None], jnp.zeros((8 - bsz - 1, d), F32)], axis=0)
    mods_all = _modulation(cin, w_mod, b_mod).reshape(depth, 8, N_MOD, d)
    cos, sin = _rope_tables(n_ctx, n_lat)
    s5_w1, s5_wo, s5_ar, s5_ai = jax.vmap(_s5_tables)(s5_lam_re, s5_lam_im, s5_log_dt, s5_b_re, s5_b_im,
                                                      s5_c_re, s5_c_im, s5_d)
    xs = jnp.concatenate([ctx, x], axis=1)

    tm_in = _pick_tile(t, (1088, 544, 272, 256, 128))
    tm_merge = _pick_tile(n_ctx, (256, 128))
    gb = 128 // S5_GROUP

    for l in range(depth):
        mods = mods_all[l]
        proj = _in_proj(xs, norm1_g[l], mods, w_in[l].astype(BF16), n_ctx, tm_in)

        y_s5 = _s5_mix(proj, s5_w1, s5_wo, s5_ar, s5_ai, l, n_ctx, gb)

        lg_tab = jnp.broadcast_to(jnp.pad(ret_log_decay[l].T, ((0, 0), (0, 6)))[:, :, None],
                                  (RET_HEADS, 8, RET_HEAD_DIM))
        o_ret = _retention(proj, cos, sin, lg_tab, n_ctx)

        last = l == depth - 1
        t_l, ctx_l = (n_lat, 0) if last else (t, n_ctx)
        n_l = bsz * t_l
        xs, h2c, logits_t = _merge(y_s5, o_ret, proj, xs, mods, ret_norm_g[l], norm2_g[l],
                                   s5_w_glu[l].astype(BF16), w_br_s5[l].astype(BF16), w_br_ret[l].astype(BF16),
                                   w_out[l].astype(BF16), moe_router[l], n_ctx, tm_merge, latent_only=last)

        n_blocks = -(-(n_l * TOP_K + N_EXPERTS * (MOE_ROWS - 1)) // MOE_ROWS)
        tw8, ei8, rk8, cnt = _route(logits_t, moe_router_bias[l], _pick_tile(n_l, (256, 128)), 0, n_l)
        row_tok, pos, blk_e, nxt_e, n_used = _dispatch_plan(ei8, rk8, cnt, n_blocks,
                                                            _pick_tile(n_l, (2176, 2048, 1024, 512, 256, 128)))
        ys = _experts(h2c, row_tok, blk_e, nxt_e, n_used, moe_w_gate, moe_w_up, moe_w_down, l, n_blocks)
        xs = _combine(ys, pos, tw8.T, h2c, xs.reshape(n_l, d), mods, sh_w_gate[l].astype(BF16),
                      sh_w_up[l].astype(BF16), sh_w_down[l].astype(BF16), final_g, ctx_l, t_l,
                      final=last).reshape(bsz, t_l, d)

    return xs
```

```python
import functools

import jax
import jax.numpy as jnp
from jax import lax
from jax.experimental import pallas as pl
from jax.experimental.pallas import tpu as pltpu

F32 = jnp.float32
BF16 = jnp.bfloat16

NORM_EPS = 1e-6
N_MOD = 6
GRID_W = 64
ROPE_BASE = 10000.0

S5_GROUP = 16
S5_STATE = 64
S5_CHUNK = 16
S5_TILE = S5_CHUNK * S5_GROUP
S5_LANES = 2 * S5_STATE

RET_HEADS = 8
RET_HEAD_DIM = 128
RET_CHUNK = 128
BRANCH_WIDTH = RET_HEADS * RET_HEAD_DIM

N_EXPERTS = 64
TOP_K = 8
N_GROUPS = 8
TOPK_GROUPS = 4
ROUTED_SCALE = 2.5
MOE_ROWS = 256
COMBINE_TOKENS = 128

VMEM_LIMIT = 56 * 1024 * 1024
EXPERT_VMEM_LIMIT = 62 * 1024 * 1024


def _cparams(*sem):
    return pltpu.CompilerParams(dimension_semantics=sem, vmem_limit_bytes=VMEM_LIMIT)


def _silu(x):
    return x * jax.nn.sigmoid(x)


def _gelu_tanh(x):
    return 0.5 * x * (1.0 + jnp.tanh(0.7978845608028654 * (x + 0.044715 * x * x * x)))


def _rms(x):
    return x * lax.rsqrt(jnp.mean(x * x, axis=-1, keepdims=True) + NORM_EPS)


U32 = jnp.uint32
HIGH_HALF = 0xFFFF0000


def _bf16_bits(x):
    return lax.bitcast_convert_type(x.astype(BF16).astype(F32), U32)


def _store_token_rows(ref, val):
    rows, d = val.shape
    nw = d // 256
    for j in range(nw):
        lo = _bf16_bits(val[:, j * 128:(j + 1) * 128])
        hi = _bf16_bits(val[:, (j + nw) * 128:(j + nw + 1) * 128])
        ref[pl.ds(j, rows, stride=nw), :] = (hi & U32(HIGH_HALF)) | (lo >> 16)


def _load_token_rows(ref, rows, d):
    nw = d // 256
    words = [ref[pl.ds(j, rows, stride=nw), :] for j in range(nw)]
    lo = [lax.bitcast_convert_type(w << 16, F32) for w in words]
    hi = [lax.bitcast_convert_type(w & U32(HIGH_HALF), F32) for w in words]
    return jnp.concatenate(lo + hi, axis=1)


def _mod_kernel(c_ref, w_ref, b_ref, o_ref):
    a = _silu(c_ref[...]).astype(BF16)
    o_ref[...] = jnp.dot(a, w_ref[...].astype(BF16), preferred_element_type=F32) + b_ref[...]


def _modulation(cin, w_mod, b_mod):
    depth, d, n = w_mod.shape
    tn = 1024
    return pl.pallas_call(
        _mod_kernel,
        grid=(depth, n // tn),
        in_specs=[pl.BlockSpec((8, d), lambda l, j: (0, 0)),
                  pl.BlockSpec((None, d, tn), lambda l, j: (l, 0, j)),
                  pl.BlockSpec((None, 1, tn), lambda l, j: (l, 0, j))],
        out_specs=pl.BlockSpec((None, 8, tn), lambda l, j: (l, 0, j)),
        out_shape=jax.ShapeDtypeStruct((depth, 8, n), F32),
        compiler_params=_cparams("arbitrary", "arbitrary"),
    )(cin, w_mod, b_mod.reshape(depth, 1, n))


def _in_proj_kernel(x_ref, g_ref, ml_ref, mc_ref, w_ref, o_ref, h_scr, *, n_ctx, tm):
    i = pl.program_id(1)
    j = pl.program_id(2)
    sub = tm // 4 if tm % 64 == 0 else tm

    @pl.when(j == 0)
    def _():
        for r0 in range(0, tm, sub):
            rows = slice(r0, r0 + sub)
            y = _rms(x_ref[rows, :]) * g_ref[...]
            row = i * tm + r0 + lax.broadcasted_iota(jnp.int32, (sub, 1), 0)
            is_ctx = row < n_ctx
            shift = jnp.where(is_ctx, mc_ref[0:1, :], ml_ref[0:1, :])
            scale = jnp.where(is_ctx, mc_ref[1:2, :], ml_ref[1:2, :])
            h_scr[rows, :] = (y * (1.0 + scale) + shift).astype(BF16)

    o_ref[...] = jnp.dot(h_scr[...], w_ref[...], preferred_element_type=F32)


def _in_proj(xs, norm_g, mods, w_in_bf16, n_ctx, tm):
    bsz, t, d = xs.shape
    n = w_in_bf16.shape[1]
    tn = 1024
    return pl.pallas_call(
        functools.partial(_in_proj_kernel, n_ctx=n_ctx, tm=tm),
        grid=(bsz, t // tm, n // tn),
        in_specs=[pl.BlockSpec((None, tm, d), lambda b, i, j: (b, i, 0)),
                  pl.BlockSpec((1, d), lambda b, i, j: (0, 0)),
                  pl.BlockSpec((None, N_MOD, d), lambda b, i, j: (b, 0, 0)),
                  pl.BlockSpec((None, N_MOD, d), lambda b, i, j: (bsz, 0, 0)),
                  pl.BlockSpec((d, tn), lambda b, i, j: (0, j))],
        out_specs=pl.BlockSpec((None, tm, tn), lambda b, i, j: (b, i, j)),
        out_shape=jax.ShapeDtypeStruct((bsz, t, n), F32),
        scratch_shapes=[pltpu.VMEM((tm, d), BF16)],
        compiler_params=_cparams("arbitrary", "arbitrary", "arbitrary"),
    )(xs, norm_g.reshape(1, d), mods, mods, w_in_bf16)


def _s5_tables(lam_re, lam_im, log_dt, b_re, b_im, c_re, c_im, d_skip):
    hp = lax.Precision.HIGHEST
    g_n, p_n, h_n = b_re.shape
    c_n = S5_CHUNK
    dt = jnp.exp(log_dt)[..., None]
    lre = jnp.minimum(lam_re, -1e-4)
    steps = jnp.arange(c_n + 1, dtype=F32)[:, None, None, None]
    mag = jnp.exp(steps * (lre * dt))
    ang = steps * (lam_im * dt)
    pr, pi = mag * jnp.cos(ang), mag * jnp.sin(ang)
    a_re, a_im = pr[1], pi[1]
    den = lre * lre + lam_im * lam_im
    nr, ni = a_re - 1.0, a_im
    f_re = (nr * lre + ni * lam_im) / den
    f_im = (ni * lre - nr * lam_im) / den
    bb_re = f_re[..., None] * b_re - f_im[..., None] * b_im
    bb_im = f_re[..., None] * b_im + f_im[..., None] * b_re
    w_re = pr[..., None] * bb_re - pi[..., None] * bb_im
    w_im = pr[..., None] * bb_im + pi[..., None] * bb_re
    kl = (jnp.einsum('dghp,ndgpk->ndgkh', c_re, w_re[:c_n], precision=hp)
          - jnp.einsum('dghp,ndgpk->ndgkh', c_im, w_im[:c_n], precision=hp))
    lag0 = kl[0, 0] + kl[0, 1] + jnp.eye(h_n, dtype=F32)[None] * d_skip[:, None, :]
    lags = jnp.concatenate([kl[:0:-1, 1], lag0[None], kl[1:, 0]], axis=0)
    t_minus_s = jnp.arange(c_n)[None, :] - jnp.arange(c_n)[:, None] + (c_n - 1)
    ktoep = lags[t_minus_s].transpose(2, 0, 3, 1, 4).reshape(g_n, S5_TILE, S5_TILE)

    def st(w, idx, d):
        return w[idx, d].transpose(1, 0, 3, 2).reshape(g_n, S5_TILE, p_n)

    fwd_idx = c_n - 1 - jnp.arange(c_n)
    bwd_idx = jnp.arange(c_n)
    wst = jnp.concatenate([st(w_re, fwd_idx, 0), st(w_re, bwd_idx, 1),
                           st(w_im, fwd_idx, 0), st(w_im, bwd_idx, 1)], axis=-1)

    def out_rows(d, idx):
        cr = c_re[d][None]
        ci = c_im[d][None]
        er = pr[idx, d][:, :, None, :]
        ei = pi[idx, d][:, :, None, :]
        re = (cr * er - ci * ei).transpose(1, 3, 0, 2).reshape(g_n, p_n, S5_TILE)
        im = (cr * ei + ci * er).transpose(1, 3, 0, 2).reshape(g_n, p_n, S5_TILE)
        return re, im

    fo_re, fo_im = out_rows(0, jnp.arange(c_n) + 1)
    bo_re, bo_im = out_rows(1, c_n - jnp.arange(c_n))
    wout = jnp.concatenate([fo_re, bo_re, -fo_im, -bo_im], axis=1)
    w1 = jnp.concatenate([ktoep, wst], axis=-1).astype(BF16)
    ar = jnp.concatenate([pr[c_n, 0], pr[c_n, 1]], axis=-1)
    ai = jnp.concatenate([pi[c_n, 0], pi[c_n, 1]], axis=-1)
    return w1, wout.astype(BF16), ar, ai


def _s5_kernel(u_ref, w1_ref, wo_ref, ar_ref, ai_ref, y_ref, yg, sre, sim, xfr, xfi, xbr, xbi, *, gb, nc, ncc):
    ln = S5_LANES
    c_n = S5_CHUNK
    per_tile = 128 // S5_GROUP
    lane_grp = lax.broadcasted_iota(jnp.int32, (1, 128), 1) // S5_GROUP

    def regroup(pieces, src_off, dst_offs):
        acc = None
        for piece, dst in zip(pieces, dst_offs):
            shift = ((dst - src_off) * S5_GROUP) % 128
            rolled = pltpu.roll(piece, shift, 1) if shift else piece
            acc = rolled if acc is None else jnp.where(lane_grp == dst, rolled, acc)
        return acc

    xs = [u_ref[pl.ds(s, nc, stride=c_n), :] for s in range(c_n)]
    for g in range(gb):
        tile, off = divmod(g, per_tile)
        halves = []
        for half in range(c_n // per_tile):
            pieces = [xs[half * per_tile + i][:, tile * 128:(tile + 1) * 128] for i in range(per_tile)]
            halves.append(regroup(pieces, off, range(per_tile)))
        u_g = jnp.concatenate(halves, axis=1).astype(BF16)
        r = jnp.dot(u_g, w1_ref[g], preferred_element_type=F32)
        yg[g] = r[:, :S5_TILE]
        sre[:, g * ln:(g + 1) * ln] = r[:, S5_TILE:S5_TILE + ln]
        sim[:, g * ln:(g + 1) * ln] = r[:, S5_TILE + ln:]
    are = ar_ref[...]
    aim = ai_ref[...]
    is_f = (lax.broadcasted_iota(jnp.int32, (1, gb * ln), 1) % ln) < S5_STATE

    def step(j, carry):
        xre, xim = carry
        cf = j
        cb = jnp.where(j < ncc, ncc - 1 - j, nc - 1 - (j - ncc))
        s_re = jnp.where(is_f, sre[pl.ds(cf, 1), :], sre[pl.ds(cb, 1), :])
        s_im = jnp.where(is_f, sim[pl.ds(cf, 1), :], sim[pl.ds(cb, 1), :])
        xfr[pl.ds(cf, 1), :] = xre
        xfi[pl.ds(cf, 1), :] = xim
        xbr[pl.ds(cb, 1), :] = xre
        xbi[pl.ds(cb, 1), :] = xim
        return are * xre - aim * xim + s_re, are * xim + aim * xre + s_im

    zero = jnp.zeros((1, gb * ln), F32)
    lax.fori_loop(0, nc, step, (zero, zero), unroll=2)

    is_f1 = lax.broadcasted_iota(jnp.int32, (1, ln), 1) < S5_STATE
    for g in range(gb):
        sl = slice(g * ln, (g + 1) * ln)
        xp = jnp.concatenate([jnp.where(is_f1, xfr[:, sl], xbr[:, sl]),
                              jnp.where(is_f1, xfi[:, sl], xbi[:, sl])], axis=1).astype(BF16)
        yg[g] += jnp.dot(xp, wo_ref[g], preferred_element_type=F32)

    for t in range(c_n):
        t_tile, t_off = divmod(t, per_tile)
        for tile in range(gb // per_tile):
            pieces = [yg[tile * per_tile + i][:, t_tile * 128:(t_tile + 1) * 128] for i in range(per_tile)]
            y_ref[pl.ds(t, nc, stride=c_n), tile * 128:(tile + 1) * 128] = regroup(pieces, t_off, range(per_tile))


def _s5_mix(proj, w1, wout, ar, ai, layer, n_ctx, gb):
    bsz, t, _ = proj.shape
    depth, g_n = w1.shape[:2]
    nc = t // S5_CHUNK
    ncc = n_ctx // S5_CHUNK
    ln = S5_LANES
    scr = pltpu.VMEM((nc, gb * ln), F32)
    return pl.pallas_call(
        functools.partial(_s5_kernel, gb=gb, nc=nc, ncc=ncc),
        grid=(bsz, g_n // gb),
        in_specs=[pl.BlockSpec((None, t, gb * S5_GROUP), lambda b, j: (b, 0, j)),
                  pl.BlockSpec((None, gb, S5_TILE, 2 * S5_TILE), lambda b, j: (layer, j, 0, 0)),
                  pl.BlockSpec((None, gb, S5_TILE, S5_TILE), lambda b, j: (layer, j, 0, 0)),
                  pl.BlockSpec((None, None, 1, gb * ln), lambda b, j: (layer, j, 0, 0)),
                  pl.BlockSpec((None, None, 1, gb * ln), lambda b, j: (layer, j, 0, 0))],
        out_specs=pl.BlockSpec((None, t, gb * S5_GROUP), lambda b, j: (b, 0, j)),
        out_shape=jax.ShapeDtypeStruct((bsz, t, g_n * S5_GROUP), F32),
        scratch_shapes=[pltpu.VMEM((gb, nc, S5_TILE), F32), scr, scr, scr, scr, scr, scr],
        compiler_params=_cparams("arbitrary", "arbitrary"),
    )(proj, w1, wout, ar.reshape(depth, g_n // gb, 1, gb * ln), ai.reshape(depth, g_n // gb, 1, gb * ln))


def _rope_tables(n_ctx, n_lat):
    pos = jnp.arange(n_lat)
    rows = (pos // GRID_W).astype(F32)
    cols = (pos % GRID_W).astype(F32)
    nf = RET_HEAD_DIM // 4
    freqs = ROPE_BASE ** (-jnp.arange(nf, dtype=F32) / nf)
    ar, ac = rows[:, None] * freqs, cols[:, None] * freqs
    cos = jnp.concatenate([jnp.cos(ar), jnp.cos(ar), jnp.cos(ac), jnp.cos(ac)], axis=-1)
    sin = jnp.concatenate([-jnp.sin(ar), jnp.sin(ar), -jnp.sin(ac), jnp.sin(ac)], axis=-1)
    cos = jnp.concatenate([jnp.ones((n_ctx, RET_HEAD_DIM), F32), cos], axis=0)
    sin = jnp.concatenate([jnp.zeros((n_ctx, RET_HEAD_DIM), F32), sin], axis=0)
    return cos, sin


def _ret_kernel(q_ref, k_ref, v_ref, cos_ref, sin_ref, lg_ref, o_ref, qs, ks, vs, ob, *, nchunk, ncc):
    c_n = RET_CHUNK
    nf = RET_HEAD_DIM // 4
    lane = lax.broadcasted_iota(jnp.int32, (1, RET_HEAD_DIM), 1)
    first = (lane % (2 * nf)) < nf

    def rope(x):
        partner = jnp.where(first, pltpu.roll(x, RET_HEAD_DIM - nf, 1), pltpu.roll(x, nf, 1))
        return x * cos_ref[...] + partner * sin_ref[...]

    qs[...] = rope(q_ref[...]).astype(BF16)
    ks[...] = rope(k_ref[...]) * (RET_HEAD_DIM ** -0.5)
    vs[...] = v_ref[...].astype(BF16)

    lg = jnp.minimum(lg_ref[...], -1e-6)
    lgf = lg[0:1, :]
    lgb = lg[1:2, :]
    ri = lax.broadcasted_iota(jnp.int32, (c_n, c_n), 0)
    ci = lax.broadcasted_iota(jnp.int32, (c_n, c_n), 1)
    diff = (ri - ci).astype(F32)
    low = diff >= 0.0
    decay = jnp.where(low, jnp.exp(jnp.where(low, diff, 0.0) * lgf), jnp.exp(jnp.where(low, 0.0, -diff) * lgb))
    pos = ri.astype(F32)
    xi_f = jnp.exp((pos + 1.0) * lgf)
    zeta_f = jnp.exp((c_n - 1.0 - pos) * lgf)
    gch_f = jnp.exp(c_n * lgf)
    xi_b = jnp.exp((c_n - pos) * lgb)
    zeta_b = jnp.exp(pos * lgb)
    gch_b = jnp.exp(c_n * lgb)
    nt = (((1,), (1,)), ((), ()))

    def step(j, carry):
        rf, rb = carry
        sl = pl.ds(pl.multiple_of(j * c_n, c_n), c_n)
        qc, kf, vc = qs[sl, :], ks[sl, :], vs[sl, :]
        s = lax.dot_general(qc, kf.astype(BF16), nt, preferred_element_type=F32)
        o = jnp.dot((s * decay).astype(BF16), vc, preferred_element_type=F32)
        o_ref[sl, :] = o + jnp.dot(qc, rf.astype(BF16), preferred_element_type=F32) * xi_f
        rf = gch_f * rf + jnp.dot((kf * zeta_f).T.astype(BF16), vc, preferred_element_type=F32)

        cb = jnp.where(j < ncc, ncc - 1 - j, nchunk - 1 - (j - ncc))
        sb = pl.ds(pl.multiple_of(cb * c_n, c_n), c_n)
        qb, kb, vb = qs[sb, :], ks[sb, :], vs[sb, :]
        ob[sb, :] = jnp.dot(qb, rb.astype(BF16), preferred_element_type=F32) * xi_b
        rb = gch_b * rb + jnp.dot((kb * zeta_b).T.astype(BF16), vb, preferred_element_type=F32)
        return rf, rb

    zero = jnp.zeros((c_n, c_n), F32)
    lax.fori_loop(0, nchunk, step, (zero, zero), unroll=2)
    o_ref[...] += ob[...]


def _retention(proj, cos, sin, lg_tab, n_ctx):
    bsz, t, _ = proj.shape
    hd = RET_HEAD_DIM
    nchunk = t // RET_CHUNK
    ncc = n_ctx // RET_CHUNK
    q0 = BRANCH_WIDTH // hd
    return pl.pallas_call(
        functools.partial(_ret_kernel, nchunk=nchunk, ncc=ncc),
        grid=(bsz, RET_HEADS),
        in_specs=[pl.BlockSpec((None, t, hd), lambda b, h: (b, 0, q0 + h)),
                  pl.BlockSpec((None, t, hd), lambda b, h: (b, 0, 2 * q0 + h)),
                  pl.BlockSpec((None, t, hd), lambda b, h: (b, 0, 3 * q0 + h)),
                  pl.BlockSpec((t, hd), lambda b, h: (0, 0)),
                  pl.BlockSpec((t, hd), lambda b, h: (0, 0)),
                  pl.BlockSpec((None, 8, hd), lambda b, h: (h, 0, 0))],
        out_specs=pl.BlockSpec((None, t, hd), lambda b, h: (b, 0, h)),
        out_shape=jax.ShapeDtypeStruct((bsz, t, BRANCH_WIDTH), F32),
        scratch_shapes=[pltpu.VMEM((t, hd), BF16), pltpu.VMEM((t, hd), F32), pltpu.VMEM((t, hd), BF16),
                        pltpu.VMEM((t, hd), F32)],
        compiler_params=_cparams("arbitrary", "arbitrary"),
    )(proj, proj, proj, cos, sin, lg_tab)


def _merge_kernel(*refs, ngb):
    ys_ref, or_ref, gr_ref = refs[0:3]
    gs_refs = refs[3:3 + ngb]
    gt_refs = refs[3 + ngb:3 + 2 * ngb]
    (x_ref, mod_ref, rg_ref, n2_ref, wglu_ref, wbs_ref, wbr_ref, wout_ref, wr_ref,
     xo_ref, h2_ref, lg_ref) = refs[3 + 2 * ngb:]
    w = BRANCH_WIDTH
    tm, d = x_ref.shape
    sub = 128
    nw = d // 256
    for r0 in range(0, tm, sub):
        rows = slice(r0, r0 + sub)
        z = _gelu_tanh(ys_ref[rows, :]).astype(BF16)
        zz = jnp.dot(z, wglu_ref[...], preferred_element_type=F32)
        s5b = jnp.dot((zz[:, :w] * jax.nn.sigmoid(zz[:, w:])).astype(BF16), wbs_ref[...],
                      preferred_element_type=F32)
        o = or_ref[rows, :]
        on = jnp.concatenate([_rms(o[:, h * RET_HEAD_DIM:(h + 1) * RET_HEAD_DIM]) for h in range(RET_HEADS)],
                             axis=1)
        on = on * rg_ref[...]
        rb = jnp.dot((on * _silu(gr_ref[rows, :])).astype(BF16), wbr_ref[...], preferred_element_type=F32)
        gate_s = jax.nn.sigmoid(jnp.concatenate([r[rows, :] for r in gs_refs], axis=1))
        gate_r = jax.nn.sigmoid(jnp.concatenate([r[rows, :] for r in gt_refs], axis=1))
        mix = jnp.dot((gate_s * s5b + gate_r * rb).astype(BF16), wout_ref[...], preferred_element_type=F32)
        xn = x_ref[rows, :] + mod_ref[2:3, :] * mix
        xo_ref[rows, :] = xn
        h2 = (_rms(xn) * n2_ref[...]) * (1.0 + mod_ref[4:5, :]) + mod_ref[3:4, :]
        _store_token_rows(h2_ref.at[pl.ds(r0 * nw, sub * nw), :], h2)
        lg_ref[:, rows] = lax.dot_general(wr_ref[...], h2, (((1,), (1,)), ((), ())), preferred_element_type=F32,
                                          precision=lax.Precision.HIGHEST)


def _merge(y_s5, o_ret, proj, xs, mods, ret_norm_g, norm2_g, wglu, wbs, wbr, wout, w_router, n_ctx, tm,
           latent_only):
    bsz, t, d = xs.shape
    w = BRANCH_WIDTH
    gw = 1024
    ngb = d // gw
    gs0 = 5 * w // gw
    ctx_tiles = n_ctx // tm
    skip = ctx_tiles if latent_only else 0
    row = lambda b, i: (b, i + skip, 0)
    const = lambda b, i: (0, 0)
    one = pl.Buffered(1)

    def gate_spec(k):
        return pl.BlockSpec((None, tm, gw), lambda b, i: (b, i + skip, gs0 + k))

    in_specs = ([pl.BlockSpec((None, tm, w), row), pl.BlockSpec((None, tm, w), row),
                 pl.BlockSpec((None, tm, w), lambda b, i: (b, i + skip, 4))]
                + [gate_spec(k) for k in range(ngb)] + [gate_spec(ngb + k) for k in range(ngb)]
                + [pl.BlockSpec((None, tm, d), row),
                   pl.BlockSpec((None, N_MOD, d), lambda b, i: (jnp.where(i + skip < ctx_tiles, bsz, b), 0, 0)),
                   pl.BlockSpec((1, w), const), pl.BlockSpec((1, d), const),
                   pl.BlockSpec((w, 2 * w), const, pipeline_mode=one),
                   pl.BlockSpec((w, d), const, pipeline_mode=one),
                   pl.BlockSpec((w, d), const, pipeline_mode=one),
                   pl.BlockSpec((d, d), const, pipeline_mode=one),
                   pl.BlockSpec((N_EXPERTS, d), const, pipeline_mode=one)])
    nt = d // 256
    tiles = t // tm - skip
    t_out = tiles * tm
    return pl.pallas_call(
        functools.partial(_merge_kernel, ngb=ngb),
        grid=(bsz, tiles),
        in_specs=in_specs,
        out_specs=[pl.BlockSpec((None, tm, d), lambda b, i: (b, i, 0)),
                   pl.BlockSpec((tm * nt, 128), lambda b, i: (b * tiles + i, 0)),
                   pl.BlockSpec((N_EXPERTS, tm), lambda b, i: (0, b * tiles + i))],
        out_shape=[jax.ShapeDtypeStruct((bsz, t_out, d), F32), jax.ShapeDtypeStruct((bsz * t_out * nt, 128), U32),
                   jax.ShapeDtypeStruct((N_EXPERTS, bsz * t_out), F32)],
        compiler_params=_cparams("arbitrary", "arbitrary"),
    )(y_s5, o_ret, proj, *([proj] * (2 * ngb)), xs, mods, ret_norm_g.reshape(1, w), norm2_g.reshape(1, d),
      wglu, wbs, wbr, wout, w_router.T)


def _route_kernel(lg_ref, bias_ref, tw_ref, ei_ref, rk_ref, cnt_ref, carry, *, tm):
    i = pl.program_id(0)
    ne = N_EXPERTS
    per_group = ne // N_GROUPS
    neg = -jnp.inf

    @pl.when(i == 0)
    def _():
        carry[...] = jnp.zeros_like(carry)

    scores = jax.nn.sigmoid(lg_ref[...])
    biased = scores + bias_ref[...]
    sub = lax.broadcasted_iota(jnp.int32, (per_group, tm), 0)
    gscore = []
    for g in range(N_GROUPS):
        blk = biased[g * per_group:(g + 1) * per_group, :]
        m1 = jnp.max(blk, axis=0, keepdims=True)
        first = jnp.min(jnp.where(blk == m1, sub, per_group), axis=0, keepdims=True)
        m2 = jnp.max(jnp.where(sub == first, neg, blk), axis=0, keepdims=True)
        gscore.append(m1 + m2)
    masked = []
    for g in range(N_GROUPS):
        beaten = jnp.zeros((1, tm), jnp.int32)
        for j in range(N_GROUPS):
            if j != g:
                wins = (gscore[j] >= gscore[g]) if j < g else (gscore[j] > gscore[g])
                beaten = beaten + wins.astype(jnp.int32)
        keep = beaten < TOPK_GROUPS
        masked.append(jnp.where(keep, biased[g * per_group:(g + 1) * per_group, :], neg))
    mv = jnp.concatenate(masked, axis=0)
    eidx = lax.broadcasted_iota(jnp.int32, (ne, tm), 0)
    beaten = jnp.zeros((ne, tm), jnp.int32)
    for j in range(ne):
        vj = mv[j:j + 1, :]
        wins = (vj > mv) | ((vj == mv) & (j < eidx))
        beaten = beaten + wins.astype(jnp.int32)
    sel = beaten < TOP_K
    sel_w = jnp.where(sel, scores, 0.0)
    wd = sel_w / jnp.sum(sel_w, axis=0, keepdims=True) * ROUTED_SCALE
    sel_b = sel.astype(BF16)
    r_i = lax.broadcasted_iota(jnp.int32, (ne, ne), 0)
    c_i = lax.broadcasted_iota(jnp.int32, (ne, ne), 1)
    slot = jnp.dot((c_i < r_i).astype(BF16), sel_b, preferred_element_type=F32)
    t_r = lax.broadcasted_iota(jnp.int32, (tm, tm), 0)
    t_c = lax.broadcasted_iota(jnp.int32, (tm, tm), 1)
    rank = jnp.dot(sel_b, (t_r < t_c).astype(BF16), preferred_element_type=F32) + carry[:, 0:1]
    carry[...] = carry[...] + jnp.sum(sel.astype(F32), axis=1, keepdims=True)
    cnt_ref[...] = carry[...]
    eidx_f = eidx.astype(F32)
    for k in range(TOP_K):
        mk = sel & (slot == float(k))
        tw_ref[k:k + 1, :] = jnp.sum(jnp.where(mk, wd, 0.0), axis=0, keepdims=True)
        ei_ref[k:k + 1, :] = jnp.sum(jnp.where(mk, eidx_f, 0.0), axis=0, keepdims=True).astype(jnp.int32)
        rk_ref[k:k + 1, :] = jnp.sum(jnp.where(mk, rank, 0.0), axis=0, keepdims=True).astype(jnp.int32)


def _route(logits_t, b_router, tm, batch, n):
    ne = logits_t.shape[0]
    col = lambda i: (0, i)
    tiles = n // tm
    return pl.pallas_call(
        functools.partial(_route_kernel, tm=tm),
        grid=(tiles,),
        in_specs=[pl.BlockSpec((ne, tm), lambda i: (0, batch * tiles + i)), pl.BlockSpec((ne, 1), lambda i: (0, 0))],
        out_specs=[pl.BlockSpec((TOP_K, tm), col), pl.BlockSpec((TOP_K, tm), col), pl.BlockSpec((TOP_K, tm), col),
                   pl.BlockSpec((ne, 128), lambda i: (0, 0))],
        out_shape=[jax.ShapeDtypeStruct((TOP_K, n), F32), jax.ShapeDtypeStruct((TOP_K, n), jnp.int32),
                   jax.ShapeDtypeStruct((TOP_K, n), jnp.int32), jax.ShapeDtypeStruct((ne, 128), F32)],
        scratch_shapes=[pltpu.VMEM((ne, 128), F32)],
        compiler_params=_cparams("arbitrary"),
    )(logits_t, b_router.reshape(ne, 1))


def _row_tok_kernel(lo_ref, hi_ref, pos_ref, out_ref, *, tb):
    j = pl.program_id(0)

    @pl.when(j == 0)
    def _():
        last = out_ref.shape[0] - 8

        def per_range(e, carry):
            lo = lo_ref[e]

            def fill(q, c):
                base = jnp.minimum(lo + q * 8, last)
                for r in range(8):
                    out_ref[base + r] = 0
                return c
            lax.fori_loop(0, (hi_ref[e] - lo + 7) // 8, fill, 0)
            return carry
        lax.fori_loop(0, lo_ref.shape[0], per_range, 0)

    def body(n, carry):
        for k in range(TOP_K):
            out_ref[pos_ref[k, n]] = j * tb + n
        return carry
    lax.fori_loop(0, tb, body, 0, unroll=8)


def _dispatch_plan(ei8, rk8, cnt, n_blocks, tb):
    n = ei8.shape[1]
    counts = cnt[:, 0].astype(jnp.int32)
    padded = (counts + MOE_ROWS - 1) // MOE_ROWS * MOE_ROWS
    pad_end = jnp.cumsum(padded)
    pad_start = pad_end - padded
    onehot = ei8[..., None] == jnp.arange(N_EXPERTS, dtype=jnp.int32)
    pos8 = rk8 + jnp.sum(jnp.where(onehot, pad_start, 0), axis=-1)
    blk_e = jnp.minimum(jnp.sum(jnp.arange(n_blocks, dtype=jnp.int32)[:, None] * MOE_ROWS >= pad_end[None, :],
                                axis=1), N_EXPERTS - 1).astype(jnp.int32)
    n_used = (pad_end[-1] // MOE_ROWS).astype(jnp.int32).reshape(1)
    run_end = jnp.sum(jnp.where(blk_e[:, None] == jnp.arange(N_EXPERTS, dtype=jnp.int32), pad_end // MOE_ROWS, 0),
                      axis=1)
    blk_at_run_end = jnp.sum(jnp.where(run_end[:, None] == jnp.arange(n_blocks, dtype=jnp.int32), blk_e[None, :], 0),
                             axis=1)
    nxt_e = jnp.where(run_end < n_used[0], blk_at_run_end, -1).astype(jnp.int32)
    grid_spec = pltpu.PrefetchScalarGridSpec(
        num_scalar_prefetch=2,
        grid=(n // tb,),
        in_specs=[pl.BlockSpec((TOP_K, tb), lambda j, lo, hi: (0, j), memory_space=pltpu.SMEM)],
        out_specs=pl.BlockSpec(memory_space=pltpu.SMEM))
    row_tok = pl.pallas_call(
        functools.partial(_row_tok_kernel, tb=tb),
        grid_spec=grid_spec,
        out_shape=jax.ShapeDtypeStruct((n_blocks * MOE_ROWS,), jnp.int32),
        compiler_params=_cparams("arbitrary"),
    )(jnp.concatenate([pad_start + counts, pad_end[-1:]]).astype(jnp.int32),
      jnp.concatenate([pad_end, jnp.full((1,), n_blocks * MOE_ROWS)]).astype(jnp.int32), pos8)
    return row_tok, pos8.reshape(-1), blk_e, nxt_e, n_used


def _expert_kernel(blk_e_ref, nxt_e_ref, nused_ref, tok_ref, h_ref, wg_hbm, wu_hbm, wd_hbm, y_ref,
                   xg, stage_g, stage_u, stage_d, sem, wgb, wub, wdb, *, layer):
    bm = MOE_ROWS
    i = pl.program_id(0)
    n_used = nused_ref[0]
    nt = xg.shape[1] // bm
    d = nt * 256
    slot = i % 2

    def weight_copies(e):
        return (pltpu.make_async_copy(wg_hbm.at[layer, e], stage_g, sem.at[0]),
                pltpu.make_async_copy(wu_hbm.at[layer, e], stage_u, sem.at[1]),
                pltpu.make_async_copy(wd_hbm.at[layer, e], stage_d, sem.at[2]))

    def gather_row(blk, slot_, r):
        tok = tok_ref[blk * bm + r]
        xg[slot_, pl.ds(r * nt, nt), :] = h_ref[pl.ds(pl.multiple_of(tok * nt, nt), nt), :]

    @pl.when(i < n_used)
    def _():
        e = blk_e_ref[i]

        @pl.when(i == 0)
        def _():
            for cp in weight_copies(e):
                cp.start()

            def first(r, carry):
                gather_row(0, 0, r)
                return carry
            lax.fori_loop(0, bm, first, 0, unroll=8)

        new_expert = jnp.logical_or(i == 0, e != blk_e_ref[jnp.maximum(i - 1, 0)])

        @pl.when(new_expert)
        def _():
            for cp in weight_copies(e):
                cp.wait()
            wgb[...] = stage_g[...].astype(BF16)
            wub[...] = stage_u[...].astype(BF16)
            wdb[...] = stage_d[...].astype(BF16)
            nxt = nxt_e_ref[i]

            @pl.when(nxt >= 0)
            def _():
                for cp in weight_copies(nxt):
                    cp.start()

        x = _load_token_rows(xg.at[slot], bm, d).astype(BF16)
        g = jnp.dot(x, wgb[...], preferred_element_type=F32)
        u = jnp.dot(x, wub[...], preferred_element_type=F32)
        _store_token_rows(y_ref, jnp.dot((_silu(g) * u).astype(BF16), wdb[...], preferred_element_type=F32))
        nxt_blk = jnp.minimum(i + 1, n_used - 1)
        for r in range(bm):
            gather_row(nxt_blk, 1 - slot, r)

    @pl.when(i >= n_used)
    def _():
        y_ref[...] = jnp.zeros_like(y_ref)


def _experts(h2c, row_tok, blk_e, nxt_e, n_used, w_gate, w_up, w_down, layer, n_blocks):
    d, de = w_gate.shape[-2:]
    nt = d // 256
    bm = MOE_ROWS
    hbm = pl.BlockSpec(memory_space=pl.ANY)
    grid_spec = pltpu.PrefetchScalarGridSpec(
        num_scalar_prefetch=4,
        grid=(n_blocks,),
        in_specs=[pl.BlockSpec(h2c.shape, lambda i, *_: (0, 0), pipeline_mode=pl.Buffered(1)), hbm, hbm, hbm],
        out_specs=pl.BlockSpec((bm * nt, 128), lambda i, *_: (i, 0)),
        scratch_shapes=[pltpu.VMEM((2, bm * nt, 128), U32),
                        pltpu.VMEM((d, de), F32), pltpu.VMEM((d, de), F32), pltpu.VMEM((de, d), F32),
                        pltpu.SemaphoreType.DMA((3,)),
                        pltpu.VMEM((d, de), BF16), pltpu.VMEM((d, de), BF16), pltpu.VMEM((de, d), BF16)])
    return pl.pallas_call(
        functools.partial(_expert_kernel, layer=layer),
        grid_spec=grid_spec,
        out_shape=jax.ShapeDtypeStruct((n_blocks * bm * nt, 128), U32),
        compiler_params=pltpu.CompilerParams(dimension_semantics=("arbitrary",),
                                             vmem_limit_bytes=EXPERT_VMEM_LIMIT),
    )(blk_e, nxt_e, n_used, row_tok, h2c, w_gate, w_up, w_down)


def _combine_kernel(pos_ref, ys_hbm, tw_ref, h2_ref, x_ref, mod_ref, wsg_ref, wsu_ref, wsd_ref, fg_ref,
                    o_ref, buf, sem, *, tmc, final):
    i = pl.program_id(0)
    n_tiles = pl.num_programs(0)
    n_tok = n_tiles * tmc
    slot = i % 2
    nt = buf.shape[2] // tmc
    d = nt * 256

    def row_copy(p, slot_, k, t):
        return pltpu.make_async_copy(ys_hbm.at[pl.ds(pl.multiple_of(p, nt), nt), :],
                                     buf.at[slot_, k, pl.ds(pl.multiple_of(t * nt, nt), nt), :], sem.at[slot_])

    def issue(tile, slot_):
        def body(t, carry):
            for k in range(TOP_K):
                row_copy(pos_ref[k * n_tok + tile * tmc + t], slot_, k, t).start(priority=k % 2)
            return carry
        lax.fori_loop(0, tmc, body, 0, unroll=4)

    @pl.when(i == 0)
    def _():
        issue(0, 0)

    @pl.when(i + 1 < n_tiles)
    def _():
        issue(i + 1, 1 - slot)

    h = _load_token_rows(h2_ref, tmc, d).astype(BF16)
    sg = jnp.dot(h, wsg_ref[...], preferred_element_type=F32)
    su = jnp.dot(h, wsu_ref[...], preferred_element_type=F32)
    ff = jnp.dot((_silu(sg) * su).astype(BF16), wsd_ref[...], preferred_element_type=F32)

    def wait_body(t, carry):
        for k in range(TOP_K):
            row_copy(0, slot, k, t).wait()
        return carry
    lax.fori_loop(0, tmc, wait_body, 0, unroll=4)

    tw = tw_ref[...]
    routed = tw[:, 0:1] * _load_token_rows(buf.at[slot, 0], tmc, d)
    for k in range(1, TOP_K):
        routed = routed + tw[:, k:k + 1] * _load_token_rows(buf.at[slot, k], tmc, d)
    xn = x_ref[...] + mod_ref[5:6, :] * (routed + ff)
    if final:
        xn = _rms(xn) * fg_ref[...]
    o_ref[...] = xn


def _combine(ys, pos, top_w, h2c, xs_flat, mods, wsg, wsu, wsd, final_g, n_ctx, t_per_batch, final):
    d, de = wsg.shape
    nt = d // 256
    n = xs_flat.shape[0]
    tmc = COMBINE_TOKENS
    tiles = n // tmc
    bsz = n // t_per_batch
    tiles_per_batch = t_per_batch // tmc
    ctx_tiles = n_ctx // tmc
    const = lambda i, p: (0, 0)
    row = lambda i, p: (i, 0)

    def mod_map(i, p):
        return (jnp.where(i % tiles_per_batch < ctx_tiles, bsz, i // tiles_per_batch), 0, 0)

    grid_spec = pltpu.PrefetchScalarGridSpec(
        num_scalar_prefetch=1,
        grid=(tiles,),
        in_specs=[pl.BlockSpec(memory_space=pl.ANY),
                  pl.BlockSpec((tmc, TOP_K), row),
                  pl.BlockSpec((tmc * nt, 128), row),
                  pl.BlockSpec((tmc, d), row),
                  pl.BlockSpec((None, N_MOD, d), mod_map),
                  pl.BlockSpec((d, de), const), pl.BlockSpec((d, de), const), pl.BlockSpec((de, d), const),
                  pl.BlockSpec((1, d), const)],
        out_specs=pl.BlockSpec((tmc, d), row),
        scratch_shapes=[pltpu.VMEM((2, TOP_K, tmc * nt, 128), U32), pltpu.SemaphoreType.DMA((2,))])
    return pl.pallas_call(
        functools.partial(_combine_kernel, tmc=tmc, final=final),
        grid_spec=grid_spec,
        out_shape=jax.ShapeDtypeStruct(xs_flat.shape, F32),
        input_output_aliases={4: 0},
        compiler_params=_cparams("arbitrary"),
    )(pos, ys, top_w, h2c, xs_flat, mods, wsg, wsu, wsd, final_g.reshape(1, d))


def _pick_tile(n, candidates):
    for c in candidates:
        if n % c == 0:
            return c
    raise ValueError(f"no tile for {n}")


def kernel(x, c, ctx, c_ctx, w_mod, b_mod, norm1_g, norm2_g, final_g, w_in, s5_lam_re, s5_lam_im, s5_log_dt, s5_b_re, s5_b_im, s5_c_re, s5_c_im, s5_d, s5_w_glu, ret_log_decay, ret_norm_g, w_br_s5, w_br_ret, w_out, moe_router, moe_router_bias, moe_w_gate, moe_w_up, moe_w_down, sh_w_gate, sh_w_up, sh_w_down):
    bsz, n_lat, d = x.shape
    n_ctx = ctx.shape[1]
    depth = w_mod.shape[0]
    t = n_ctx + n_lat
    n_tok = bsz * t
    g_n = s5_b_re.shape[1]
    assert n_ctx % RET_CHUNK == 0 and n_lat % RET_CHUNK == 0 and bsz + 1 <= 8
    assert d % 1024 == 0 and n_tok % COMBINE_TOKENS == 0 and n_ctx % COMBINE_TOKENS == 0

    cin = jnp.concatenate([c, c_ctx[None], jnp.zeros((8 - bsz - 1, d), F32)], axis=0)
    mods_all = _modulation(cin, w_mod, b_mod).reshape(depth, 8, N_MOD, d)
    cos, sin = _rope_tables(n_ctx, n_lat)
    s5_w1, s5_wo, s5_ar, s5_ai = jax.vmap(_s5_tables)(s5_lam_re, s5_lam_im, s5_log_dt, s5_b_re, s5_b_im,
                                                      s5_c_re, s5_c_im, s5_d)
    xs = jnp.concatenate([ctx, x], axis=1)

    tm_in = _pick_tile(t, (1088, 544, 272, 256, 128))
    tm_merge = _pick_tile(n_ctx, (256, 128))
    gb = 128 // S5_GROUP

    for l in range(depth):
        mods = mods_all[l]
        proj = _in_proj(xs, norm1_g[l], mods, w_in[l].astype(BF16), n_ctx, tm_in)

        y_s5 = _s5_mix(proj, s5_w1, s5_wo, s5_ar, s5_ai, l, n_ctx, gb)

        lg_tab = jnp.broadcast_to(jnp.pad(ret_log_decay[l].T, ((0, 0), (0, 6)))[:, :, None],
                                  (RET_HEADS, 8, RET_HEAD_DIM))
        o_ret = _retention(proj, cos, sin, lg_tab, n_ctx)

        last = l == depth - 1
        t_l, ctx_l = (n_lat, 0) if last else (t, n_ctx)
        n_l = bsz * t_l
        xs, h2c, logits_t = _merge(y_s5, o_ret, proj, xs, mods, ret_norm_g[l], norm2_g[l],
                                   s5_w_glu[l].astype(BF16), w_br_s5[l].astype(BF16), w_br_ret[l].astype(BF16),
                                   w_out[l].astype(BF16), moe_router[l], n_ctx, tm_merge, latent_only=last)

        n_blocks = -(-(n_l * TOP_K + N_EXPERTS * (MOE_ROWS - 1)) // MOE_ROWS)
        tw8, ei8, rk8, cnt = _route(logits_t, moe_router_bias[l], _pick_tile(n_l, (256, 128)), 0, n_l)
        row_tok, pos, blk_e, nxt_e, n_used = _dispatch_plan(ei8, rk8, cnt, n_blocks,
                                                            _pick_tile(n_l, (2176, 2048, 1024, 512, 256, 128)))
        ys = _experts(h2c, row_tok, blk_e, nxt_e, n_used, moe_w_gate, moe_w_up, moe_w_down, l, n_blocks)
        xs = _combine(ys, pos * (d // 256), tw8.T, h2c, xs.reshape(n_l, d), mods, sh_w_gate[l].astype(BF16),
                      sh_w_up[l].astype(BF16), sh_w_down[l].astype(BF16), final_g, ctx_l, t_l,
                      final=last).reshape(bsz, t_l, d)

    return xs
```

```python
import functools

import jax
import jax.numpy as jnp
from jax import lax
from jax.experimental import pallas as pl
from jax.experimental.pallas import tpu as pltpu

F32 = jnp.float32
BF16 = jnp.bfloat16

NORM_EPS = 1e-6
N_MOD = 6
GRID_W = 64
ROPE_BASE = 10000.0

S5_GROUP = 16
S5_STATE = 64
S5_CHUNK = 16
S5_TILE = S5_CHUNK * S5_GROUP
S5_LANES = 2 * S5_STATE

RET_HEADS = 8
RET_HEAD_DIM = 128
RET_CHUNK = 128
BRANCH_WIDTH = RET_HEADS * RET_HEAD_DIM

N_EXPERTS = 64
TOP_K = 8
N_GROUPS = 8
TOPK_GROUPS = 4
ROUTED_SCALE = 2.5
MOE_ROWS = 256
COMBINE_TOKENS = 128

VMEM_LIMIT = 56 * 1024 * 1024
EXPERT_VMEM_LIMIT = 62 * 1024 * 1024


def _cparams(*sem):
    return pltpu.CompilerParams(dimension_semantics=sem, vmem_limit_bytes=VMEM_LIMIT)


def _silu(x):
    return x * jax.nn.sigmoid(x)


def _gelu_tanh(x):
    return 0.5 * x * (1.0 + jnp.tanh(0.7978845608028654 * (x + 0.044715 * x * x * x)))


def _rms(x):
    return x * lax.rsqrt(jnp.mean(x * x, axis=-1, keepdims=True) + NORM_EPS)


U32 = jnp.uint32
HIGH_HALF = 0xFFFF0000


def _bf16_bits(x):
    return lax.bitcast_convert_type(x.astype(BF16).astype(F32), U32)


def _store_token_rows(ref, val):
    rows, d = val.shape
    nw = d // 256
    for j in range(nw):
        lo = _bf16_bits(val[:, j * 128:(j + 1) * 128])
        hi = _bf16_bits(val[:, (j + nw) * 128:(j + nw + 1) * 128])
        ref[pl.ds(j, rows, stride=nw), :] = (hi & U32(HIGH_HALF)) | (lo >> 16)


def _load_token_rows(ref, rows, d):
    nw = d // 256
    words = [ref[pl.ds(j, rows, stride=nw), :] for j in range(nw)]
    lo = [lax.bitcast_convert_type(w << 16, F32) for w in words]
    hi = [lax.bitcast_convert_type(w & U32(HIGH_HALF), F32) for w in words]
    return jnp.concatenate(lo + hi, axis=1)


def _mod_kernel(c_ref, w_ref, b_ref, o_ref):
    a = _silu(c_ref[...]).astype(BF16)
    o_ref[...] = jnp.dot(a, w_ref[...].astype(BF16), preferred_element_type=F32) + b_ref[...]


def _modulation(cin, w_mod, b_mod):
    depth, d, n = w_mod.shape
    tn = 1024
    return pl.pallas_call(
        _mod_kernel,
        grid=(depth, n // tn),
        in_specs=[pl.BlockSpec((8, d), lambda l, j: (0, 0)),
                  pl.BlockSpec((None, d, tn), lambda l, j: (l, 0, j)),
                  pl.BlockSpec((None, 1, tn), lambda l, j: (l, 0, j))],
        out_specs=pl.BlockSpec((None, 8, tn), lambda l, j: (l, 0, j)),
        out_shape=jax.ShapeDtypeStruct((depth, 8, n), F32),
        compiler_params=_cparams("arbitrary", "arbitrary"),
    )(cin, w_mod, b_mod.reshape(depth, 1, n))


def _in_proj_kernel(x_ref, g_ref, ml_ref, mc_ref, w_ref, o_ref, h_scr, *, n_ctx, tm):
    i = pl.program_id(1)
    j = pl.program_id(2)
    sub = tm // 4 if tm % 64 == 0 else tm

    @pl.when(j == 0)
    def _():
        for r0 in range(0, tm, sub):
            rows = slice(r0, r0 + sub)
            y = _rms(x_ref[rows, :]) * g_ref[...]
            row = i * tm + r0 + lax.broadcasted_iota(jnp.int32, (sub, 1), 0)
            is_ctx = row < n_ctx
            shift = jnp.where(is_ctx, mc_ref[0:1, :], ml_ref[0:1, :])
            scale = jnp.where(is_ctx, mc_ref[1:2, :], ml_ref[1:2, :])
            h_scr[rows, :] = (y * (1.0 + scale) + shift).astype(BF16)

    o_ref[...] = jnp.dot(h_scr[...], w_ref[...], preferred_element_type=F32)


def _in_proj(xs, norm_g, mods, w_in_bf16, n_ctx, tm):
    bsz, t, d = xs.shape
    n = w_in_bf16.shape[1]
    tn = 1024
    return pl.pallas_call(
        functools.partial(_in_proj_kernel, n_ctx=n_ctx, tm=tm),
        grid=(bsz, t // tm, n // tn),
        in_specs=[pl.BlockSpec((None, tm, d), lambda b, i, j: (b, i, 0)),
                  pl.BlockSpec((1, d), lambda b, i, j: (0, 0)),
                  pl.BlockSpec((None, N_MOD, d), lambda b, i, j: (b, 0, 0)),
                  pl.BlockSpec((None, N_MOD, d), lambda b, i, j: (bsz, 0, 0)),
                  pl.BlockSpec((d, tn), lambda b, i, j: (0, j))],
        out_specs=pl.BlockSpec((None, tm, tn), lambda b, i, j: (b, i, j)),
        out_shape=jax.ShapeDtypeStruct((bsz, t, n), F32),
        scratch_shapes=[pltpu.VMEM((tm, d), BF16)],
        compiler_params=_cparams("arbitrary", "arbitrary", "arbitrary"),
    )(xs, norm_g.reshape(1, d), mods, mods, w_in_bf16)


def _s5_tables(lam_re, lam_im, log_dt, b_re, b_im, c_re, c_im, d_skip):
    hp = lax.Precision.HIGHEST
    g_n, p_n, h_n = b_re.shape
    c_n = S5_CHUNK
    dt = jnp.exp(log_dt)[..., None]
    lre = jnp.minimum(lam_re, -1e-4)
    steps = jnp.arange(c_n + 1, dtype=F32)[:, None, None, None]
    mag = jnp.exp(steps * (lre * dt))
    ang = steps * (lam_im * dt)
    pr, pi = mag * jnp.cos(ang), mag * jnp.sin(ang)
    a_re, a_im = pr[1], pi[1]
    den = lre * lre + lam_im * lam_im
    nr, ni = a_re - 1.0, a_im
    f_re = (nr * lre + ni * lam_im) / den
    f_im = (ni * lre - nr * lam_im) / den
    bb_re = f_re[..., None] * b_re - f_im[..., None] * b_im
    bb_im = f_re[..., None] * b_im + f_im[..., None] * b_re
    w_re = pr[..., None] * bb_re - pi[..., None] * bb_im
    w_im = pr[..., None] * bb_im + pi[..., None] * bb_re
    kl = (jnp.einsum('dghp,ndgpk->ndgkh', c_re, w_re[:c_n], precision=hp)
          - jnp.einsum('dghp,ndgpk->ndgkh', c_im, w_im[:c_n], precision=hp))
    lag0 = kl[0, 0] + kl[0, 1] + jnp.eye(h_n, dtype=F32)[None] * d_skip[:, None, :]
    lags = jnp.concatenate([kl[:0:-1, 1], lag0[None], kl[1:, 0]], axis=0)
    t_minus_s = jnp.arange(c_n)[None, :] - jnp.arange(c_n)[:, None] + (c_n - 1)
    ktoep = lags[t_minus_s].transpose(2, 0, 3, 1, 4).reshape(g_n, S5_TILE, S5_TILE)

    def st(w, idx, d):
        return w[idx, d].transpose(1, 0, 3, 2).reshape(g_n, S5_TILE, p_n)

    fwd_idx = c_n - 1 - jnp.arange(c_n)
    bwd_idx = jnp.arange(c_n)
    wst = jnp.concatenate([st(w_re, fwd_idx, 0), st(w_re, bwd_idx, 1),
                           st(w_im, fwd_idx, 0), st(w_im, bwd_idx, 1)], axis=-1)

    def out_rows(d, idx):
        cr = c_re[d][None]
        ci = c_im[d][None]
        er = pr[idx, d][:, :, None, :]
        ei = pi[idx, d][:, :, None, :]
        re = (cr * er - ci * ei).transpose(1, 3, 0, 2).reshape(g_n, p_n, S5_TILE)
        im = (cr * ei + ci * er).transpose(1, 3, 0, 2).reshape(g_n, p_n, S5_TILE)
        return re, im

    fo_re, fo_im = out_rows(0, jnp.arange(c_n) + 1)
    bo_re, bo_im = out_rows(1, c_n - jnp.arange(c_n))
    wout = jnp.concatenate([fo_re, bo_re, -fo_im, -bo_im], axis=1)
    w1 = jnp.concatenate([ktoep, wst], axis=-1).astype(BF16)
    ar = jnp.concatenate([pr[c_n, 0], pr[c_n, 1]], axis=-1)
    ai = jnp.concatenate([pi[c_n, 0], pi[c_n, 1]], axis=-1)
    return w1, wout.astype(BF16), ar, ai


def _s5_kernel(u_ref, w1_ref, wo_ref, ar_ref, ai_ref, y_ref, yg, sre, sim, xfr, xfi, xbr, xbi, *, gb, nc, ncc):
    ln = S5_LANES
    c_n = S5_CHUNK
    per_tile = 128 // S5_GROUP
    lane_grp = lax.broadcasted_iota(jnp.int32, (1, 128), 1) // S5_GROUP

    def regroup(pieces, src_off, dst_offs):
        acc = None
        for piece, dst in zip(pieces, dst_offs):
            shift = ((dst - src_off) * S5_GROUP) % 128
            rolled = pltpu.roll(piece, shift, 1) if shift else piece
            acc = rolled if acc is None else jnp.where(lane_grp == dst, rolled, acc)
        return acc

    xs = [u_ref[pl.ds(s, nc, stride=c_n), :] for s in range(c_n)]
    for g in range(gb):
        tile, off = divmod(g, per_tile)
        halves = []
        for half in range(c_n // per_tile):
            pieces = [xs[half * per_tile + i][:, tile * 128:(tile + 1) * 128] for i in range(per_tile)]
            halves.append(regroup(pieces, off, range(per_tile)))
        u_g = jnp.concatenate(halves, axis=1).astype(BF16)
        r = jnp.dot(u_g, w1_ref[g], preferred_element_type=F32)
        yg[g] = r[:, :S5_TILE]
        sre[:, g * ln:(g + 1) * ln] = r[:, S5_TILE:S5_TILE + ln]
        sim[:, g * ln:(g + 1) * ln] = r[:, S5_TILE + ln:]
    are = ar_ref[...]
    aim = ai_ref[...]
    is_f = (lax.broadcasted_iota(jnp.int32, (1, gb * ln), 1) % ln) < S5_STATE

    def step(j, carry):
        xre, xim = carry
        cf = j
        cb = jnp.where(j < ncc, ncc - 1 - j, nc - 1 - (j - ncc))
        s_re = jnp.where(is_f, sre[pl.ds(cf, 1), :], sre[pl.ds(cb, 1), :])
        s_im = jnp.where(is_f, sim[pl.ds(cf, 1), :], sim[pl.ds(cb, 1), :])
        xfr[pl.ds(cf, 1), :] = xre
        xfi[pl.ds(cf, 1), :] = xim
        xbr[pl.ds(cb, 1), :] = xre
        xbi[pl.ds(cb, 1), :] = xim
        return are * xre - aim * xim + s_re, are * xim + aim * xre + s_im

    zero = jnp.zeros((1, gb * ln), F32)
    lax.fori_loop(0, nc, step, (zero, zero), unroll=2)

    is_f1 = lax.broadcasted_iota(jnp.int32, (1, ln), 1) < S5_STATE
    for g in range(gb):
        sl = slice(g * ln, (g + 1) * ln)
        xp = jnp.concatenate([jnp.where(is_f1, xfr[:, sl], xbr[:, sl]),
                              jnp.where(is_f1, xfi[:, sl], xbi[:, sl])], axis=1).astype(BF16)
        yg[g] += jnp.dot(xp, wo_ref[g], preferred_element_type=F32)

    for t in range(c_n):
        t_tile, t_off = divmod(t, per_tile)
        for tile in range(gb // per_tile):
            pieces = [yg[tile * per_tile + i][:, t_tile * 128:(t_tile + 1) * 128] for i in range(per_tile)]
            y_ref[pl.ds(t, nc, stride=c_n), tile * 128:(tile + 1) * 128] = regroup(pieces, t_off, range(per_tile))


def _s5_mix(proj, w1, wout, ar, ai, layer, n_ctx, gb):
    bsz, t, _ = proj.shape
    depth, g_n = w1.shape[:2]
    nc = t // S5_CHUNK
    ncc = n_ctx // S5_CHUNK
    ln = S5_LANES
    scr = pltpu.VMEM((nc, gb * ln), F32)
    return pl.pallas_call(
        functools.partial(_s5_kernel, gb=gb, nc=nc, ncc=ncc),
        grid=(bsz, g_n // gb),
        in_specs=[pl.BlockSpec((None, t, gb * S5_GROUP), lambda b, j: (b, 0, j)),
                  pl.BlockSpec((None, gb, S5_TILE, 2 * S5_TILE), lambda b, j: (layer, j, 0, 0)),
                  pl.BlockSpec((None, gb, S5_TILE, S5_TILE), lambda b, j: (layer, j, 0, 0)),
                  pl.BlockSpec((None, None, 1, gb * ln), lambda b, j: (layer, j, 0, 0)),
                  pl.BlockSpec((None, None, 1, gb * ln), lambda b, j: (layer, j, 0, 0))],
        out_specs=pl.BlockSpec((None, t, gb * S5_GROUP), lambda b, j: (b, 0, j)),
        out_shape=jax.ShapeDtypeStruct((bsz, t, g_n * S5_GROUP), F32),
        scratch_shapes=[pltpu.VMEM((gb, nc, S5_TILE), F32), scr, scr, scr, scr, scr, scr],
        compiler_params=_cparams("arbitrary", "arbitrary"),
    )(proj, w1, wout, ar.reshape(depth, g_n // gb, 1, gb * ln), ai.reshape(depth, g_n // gb, 1, gb * ln))


def _rope_tables(n_ctx, n_lat):
    pos = jnp.arange(n_lat)
    rows = (pos // GRID_W).astype(F32)
    cols = (pos % GRID_W).astype(F32)
    nf = RET_HEAD_DIM // 4
    freqs = ROPE_BASE ** (-jnp.arange(nf, dtype=F32) / nf)
    ar, ac = rows[:, None] * freqs, cols[:, None] * freqs
    cos = jnp.concatenate([jnp.cos(ar), jnp.cos(ar), jnp.cos(ac), jnp.cos(ac)], axis=-1)
    sin = jnp.concatenate([-jnp.sin(ar), jnp.sin(ar), -jnp.sin(ac), jnp.sin(ac)], axis=-1)
    cos = jnp.concatenate([jnp.ones((n_ctx, RET_HEAD_DIM), F32), cos], axis=0)
    sin = jnp.concatenate([jnp.zeros((n_ctx, RET_HEAD_DIM), F32), sin], axis=0)
    return cos, sin


def _ret_kernel(q_ref, k_ref, v_ref, cos_ref, sin_ref, lg_ref, o_ref, qs, ks, vs, ob, *, nchunk, ncc):
    c_n = RET_CHUNK
    nf = RET_HEAD_DIM // 4
    lane = lax.broadcasted_iota(jnp.int32, (1, RET_HEAD_DIM), 1)
    first = (lane % (2 * nf)) < nf

    def rope(x):
        partner = jnp.where(first, pltpu.roll(x, RET_HEAD_DIM - nf, 1), pltpu.roll(x, nf, 1))
        return x * cos_ref[...] + partner * sin_ref[...]

    qs[...] = rope(q_ref[...]).astype(BF16)
    ks[...] = rope(k_ref[...]) * (RET_HEAD_DIM ** -0.5)
    vs[...] = v_ref[...].astype(BF16)

    lg = jnp.minimum(lg_ref[...], -1e-6)
    lgf = lg[0:1, :]
    lgb = lg[1:2, :]
    ri = lax.broadcasted_iota(jnp.int32, (c_n, c_n), 0)
    ci = lax.broadcasted_iota(jnp.int32, (c_n, c_n), 1)
    diff = (ri - ci).astype(F32)
    low = diff >= 0.0
    decay = jnp.where(low, jnp.exp(jnp.where(low, diff, 0.0) * lgf), jnp.exp(jnp.where(low, 0.0, -diff) * lgb))
    pos = ri.astype(F32)
    xi_f = jnp.exp((pos + 1.0) * lgf)
    zeta_f = jnp.exp((c_n - 1.0 - pos) * lgf)
    gch_f = jnp.exp(c_n * lgf)
    xi_b = jnp.exp((c_n - pos) * lgb)
    zeta_b = jnp.exp(pos * lgb)
    gch_b = jnp.exp(c_n * lgb)
    nt = (((1,), (1,)), ((), ()))

    def step(j, carry):
        rf, rb = carry
        sl = pl.ds(pl.multiple_of(j * c_n, c_n), c_n)
        qc, kf, vc = qs[sl, :], ks[sl, :], vs[sl, :]
        s = lax.dot_general(qc, kf.astype(BF16), nt, preferred_element_type=F32)
        o = jnp.dot((s * decay).astype(BF16), vc, preferred_element_type=F32)
        o_ref[sl, :] = o + jnp.dot(qc, rf.astype(BF16), preferred_element_type=F32) * xi_f
        rf = gch_f * rf + jnp.dot((kf * zeta_f).T.astype(BF16), vc, preferred_element_type=F32)

        cb = jnp.where(j < ncc, ncc - 1 - j, nchunk - 1 - (j - ncc))
        sb = pl.ds(pl.multiple_of(cb * c_n, c_n), c_n)
        qb, kb, vb = qs[sb, :], ks[sb, :], vs[sb, :]
        ob[sb, :] = jnp.dot(qb, rb.astype(BF16), preferred_element_type=F32) * xi_b
        rb = gch_b * rb + jnp.dot((kb * zeta_b).T.astype(BF16), vb, preferred_element_type=F32)
        return rf, rb

    zero = jnp.zeros((c_n, c_n), F32)
    lax.fori_loop(0, nchunk, step, (zero, zero), unroll=2)
    o_ref[...] += ob[...]


def _retention(proj, cos, sin, lg_tab, n_ctx):
    bsz, t, _ = proj.shape
    hd = RET_HEAD_DIM
    nchunk = t // RET_CHUNK
    ncc = n_ctx // RET_CHUNK
    q0 = BRANCH_WIDTH // hd
    return pl.pallas_call(
        functools.partial(_ret_kernel, nchunk=nchunk, ncc=ncc),
        grid=(bsz, RET_HEADS),
        in_specs=[pl.BlockSpec((None, t, hd), lambda b, h: (b, 0, q0 + h)),
                  pl.BlockSpec((None, t, hd), lambda b, h: (b, 0, 2 * q0 + h)),
                  pl.BlockSpec((None, t, hd), lambda b, h: (b, 0, 3 * q0 + h)),
                  pl.BlockSpec((t, hd), lambda b, h: (0, 0)),
                  pl.BlockSpec((t, hd), lambda b, h: (0, 0)),
                  pl.BlockSpec((None, 8, hd), lambda b, h: (h, 0, 0))],
        out_specs=pl.BlockSpec((None, t, hd), lambda b, h: (b, 0, h)),
        out_shape=jax.ShapeDtypeStruct((bsz, t, BRANCH_WIDTH), F32),
        scratch_shapes=[pltpu.VMEM((t, hd), BF16), pltpu.VMEM((t, hd), F32), pltpu.VMEM((t, hd), BF16),
                        pltpu.VMEM((t, hd), F32)],
        compiler_params=_cparams("arbitrary", "arbitrary"),
    )(proj, proj, proj, cos, sin, lg_tab)


def _merge_kernel(*refs, ngb):
    ys_ref, or_ref, gr_ref = refs[0:3]
    gs_refs = refs[3:3 + ngb]
    gt_refs = refs[3 + ngb:3 + 2 * ngb]
    (x_ref, mod_ref, rg_ref, n2_ref, wglu_ref, wbs_ref, wbr_ref, wout_ref, wr_ref,
     xo_ref, h2_ref, lg_ref) = refs[3 + 2 * ngb:]
    w = BRANCH_WIDTH
    tm, d = x_ref.shape
    sub = 128
    nw = d // 256
    for r0 in range(0, tm, sub):
        rows = slice(r0, r0 + sub)
        z = _gelu_tanh(ys_ref[rows, :]).astype(BF16)
        zz = jnp.dot(z, wglu_ref[...], preferred_element_type=F32)
        s5b = jnp.dot((zz[:, :w] * jax.nn.sigmoid(zz[:, w:])).astype(BF16), wbs_ref[...],
                      preferred_element_type=F32)
        o = or_ref[rows, :]
        on = jnp.concatenate([_rms(o[:, h * RET_HEAD_DIM:(h + 1) * RET_HEAD_DIM]) for h in range(RET_HEADS)],
                             axis=1)
        on = on * rg_ref[...]
        rb = jnp.dot((on * _silu(gr_ref[rows, :])).astype(BF16), wbr_ref[...], preferred_element_type=F32)
        gate_s = jax.nn.sigmoid(jnp.concatenate([r[rows, :] for r in gs_refs], axis=1))
        gate_r = jax.nn.sigmoid(jnp.concatenate([r[rows, :] for r in gt_refs], axis=1))
        mix = jnp.dot((gate_s * s5b + gate_r * rb).astype(BF16), wout_ref[...], preferred_element_type=F32)
        xn = x_ref[rows, :] + mod_ref[2:3, :] * mix
        xo_ref[rows, :] = xn
        h2 = (_rms(xn) * n2_ref[...]) * (1.0 + mod_ref[4:5, :]) + mod_ref[3:4, :]
        _store_token_rows(h2_ref.at[pl.ds(r0 * nw, sub * nw), :], h2)
        lg_ref[:, rows] = lax.dot_general(wr_ref[...], h2, (((1,), (1,)), ((), ())), preferred_element_type=F32,
                                          precision=lax.Precision.HIGHEST)


def _merge(y_s5, o_ret, proj, xs, mods, ret_norm_g, norm2_g, wglu, wbs, wbr, wout, w_router, n_ctx, tm,
           latent_only):
    bsz, t, d = xs.shape
    w = BRANCH_WIDTH
    gw = 1024
    ngb = d // gw
    gs0 = 5 * w // gw
    ctx_tiles = n_ctx // tm
    skip = ctx_tiles if latent_only else 0
    row = lambda b, i: (b, i + skip, 0)
    const = lambda b, i: (0, 0)
    one = pl.Buffered(1)

    def gate_spec(k):
        return pl.BlockSpec((None, tm, gw), lambda b, i: (b, i + skip, gs0 + k))

    in_specs = ([pl.BlockSpec((None, tm, w), row), pl.BlockSpec((None, tm, w), row),
                 pl.BlockSpec((None, tm, w), lambda b, i: (b, i + skip, 4))]
                + [gate_spec(k) for k in range(ngb)] + [gate_spec(ngb + k) for k in range(ngb)]
                + [pl.BlockSpec((None, tm, d), row),
                   pl.BlockSpec((None, N_MOD, d), lambda b, i: (jnp.where(i + skip < ctx_tiles, bsz, b), 0, 0)),
                   pl.BlockSpec((1, w), const), pl.BlockSpec((1, d), const),
                   pl.BlockSpec((w, 2 * w), const, pipeline_mode=one),
                   pl.BlockSpec((w, d), const, pipeline_mode=one),
                   pl.BlockSpec((w, d), const, pipeline_mode=one),
                   pl.BlockSpec((d, d), const, pipeline_mode=one),
                   pl.BlockSpec((N_EXPERTS, d), const, pipeline_mode=one)])
    nt = d // 256
    tiles = t // tm - skip
    t_out = tiles * tm
    return pl.pallas_call(
        functools.partial(_merge_kernel, ngb=ngb),
        grid=(bsz, tiles),
        in_specs=in_specs,
        out_specs=[pl.BlockSpec((None, tm, d), lambda b, i: (b, i, 0)),
                   pl.BlockSpec((tm * nt, 128), lambda b, i: (b * tiles + i, 0)),
                   pl.BlockSpec((N_EXPERTS, tm), lambda b, i: (0, b * tiles + i))],
        out_shape=[jax.ShapeDtypeStruct((bsz, t_out, d), F32), jax.ShapeDtypeStruct((bsz * t_out * nt, 128), U32),
                   jax.ShapeDtypeStruct((N_EXPERTS, bsz * t_out), F32)],
        compiler_params=_cparams("arbitrary", "arbitrary"),
    )(y_s5, o_ret, proj, *([proj] * (2 * ngb)), xs, mods, ret_norm_g.reshape(1, w), norm2_g.reshape(1, d),
      wglu, wbs, wbr, wout, w_router.T)


def _route_kernel(lg_ref, bias_ref, tw_ref, ei_ref, rk_ref, cnt_ref, carry, *, tm):
    i = pl.program_id(0)
    ne = N_EXPERTS
    per_group = ne // N_GROUPS
    neg = -jnp.inf

    @pl.when(i == 0)
    def _():
        carry[...] = jnp.zeros_like(carry)

    scores = jax.nn.sigmoid(lg_ref[...])
    biased = scores + bias_ref[...]
    sub = lax.broadcasted_iota(jnp.int32, (per_group, tm), 0)
    gscore = []
    for g in range(N_GROUPS):
        blk = biased[g * per_group:(g + 1) * per_group, :]
        m1 = jnp.max(blk, axis=0, keepdims=True)
        first = jnp.min(jnp.where(blk == m1, sub, per_group), axis=0, keepdims=True)
        m2 = jnp.max(jnp.where(sub == first, neg, blk), axis=0, keepdims=True)
        gscore.append(m1 + m2)
    masked = []
    for g in range(N_GROUPS):
        beaten = jnp.zeros((1, tm), jnp.int32)
        for j in range(N_GROUPS):
            if j != g:
                wins = (gscore[j] >= gscore[g]) if j < g else (gscore[j] > gscore[g])
                beaten = beaten + wins.astype(jnp.int32)
        keep = beaten < TOPK_GROUPS
        masked.append(jnp.where(keep, biased[g * per_group:(g + 1) * per_group, :], neg))
    mv = jnp.concatenate(masked, axis=0)
    eidx = lax.broadcasted_iota(jnp.int32, (ne, tm), 0)
    beaten = jnp.zeros((ne, tm), jnp.int32)
    for j in range(ne):
        vj = mv[j:j + 1, :]
        wins = (vj > mv) | ((vj == mv) & (j < eidx))
        beaten = beaten + wins.astype(jnp.int32)
    sel = beaten < TOP_K
    sel_w = jnp.where(sel, scores, 0.0)
    wd = sel_w / jnp.sum(sel_w, axis=0, keepdims=True) * ROUTED_SCALE
    sel_b = sel.astype(BF16)
    r_i = lax.broadcasted_iota(jnp.int32, (ne, ne), 0)
    c_i = lax.broadcasted_iota(jnp.int32, (ne, ne), 1)
    slot = jnp.dot((c_i < r_i).astype(BF16), sel_b, preferred_element_type=F32)
    t_r = lax.broadcasted_iota(jnp.int32, (tm, tm), 0)
    t_c = lax.broadcasted_iota(jnp.int32, (tm, tm), 1)
    rank = jnp.dot(sel_b, (t_r < t_c).astype(BF16), preferred_element_type=F32) + carry[:, 0:1]
    carry[...] = carry[...] + jnp.sum(sel.astype(F32), axis=1, keepdims=True)
    cnt_ref[...] = carry[...]
    eidx_f = eidx.astype(F32)
    for k in range(TOP_K):
        mk = sel & (slot == float(k))
        tw_ref[k:k + 1, :] = jnp.sum(jnp.where(mk, wd, 0.0), axis=0, keepdims=True)
        ei_ref[k:k + 1, :] = jnp.sum(jnp.where(mk, eidx_f, 0.0), axis=0, keepdims=True).astype(jnp.int32)
        rk_ref[k:k + 1, :] = jnp.sum(jnp.where(mk, rank, 0.0), axis=0, keepdims=True).astype(jnp.int32)


def _route(logits_t, b_router, tm, batch, n):
    ne = logits_t.shape[0]
    col = lambda i: (0, i)
    tiles = n // tm
    return pl.pallas_call(
        functools.partial(_route_kernel, tm=tm),
        grid=(tiles,),
        in_specs=[pl.BlockSpec((ne, tm), lambda i: (0, batch * tiles + i)), pl.BlockSpec((ne, 1), lambda i: (0, 0))],
        out_specs=[pl.BlockSpec((TOP_K, tm), col), pl.BlockSpec((TOP_K, tm), col), pl.BlockSpec((TOP_K, tm), col),
                   pl.BlockSpec((ne, 128), lambda i: (0, 0))],
        out_shape=[jax.ShapeDtypeStruct((TOP_K, n), F32), jax.ShapeDtypeStruct((TOP_K, n), jnp.int32),
                   jax.ShapeDtypeStruct((TOP_K, n), jnp.int32), jax.ShapeDtypeStruct((ne, 128), F32)],
        scratch_shapes=[pltpu.VMEM((ne, 128), F32)],
        compiler_params=_cparams("arbitrary"),
    )(logits_t, b_router.reshape(ne, 1))


def _row_tok_kernel(lo_ref, hi_ref, pos_ref, out_ref, *, tb):
    j = pl.program_id(0)

    @pl.when(j == 0)
    def _():
        last = out_ref.shape[0] - 8

        def per_range(e, carry):
            lo = lo_ref[e]

            def fill(q, c):
                base = jnp.minimum(lo + q * 8, last)
                for r in range(8):
                    out_ref[base + r] = 0
                return c
            lax.fori_loop(0, (hi_ref[e] - lo + 7) // 8, fill, 0)
            return carry
        lax.fori_loop(0, lo_ref.shape[0], per_range, 0)

    def body(n, carry):
        for k in range(TOP_K):
            out_ref[pos_ref[k, n]] = j * tb + n
        return carry
    lax.fori_loop(0, tb, body, 0, unroll=16)


def _dispatch_plan(ei8, rk8, cnt, n_blocks, tb):
    n = ei8.shape[1]
    counts = cnt[:, 0].astype(jnp.int32)
    padded = (counts + MOE_ROWS - 1) // MOE_ROWS * MOE_ROWS
    pad_end = jnp.cumsum(padded)
    pad_start = pad_end - padded
    onehot = ei8[..., None] == jnp.arange(N_EXPERTS, dtype=jnp.int32)
    pos8 = rk8 + jnp.sum(jnp.where(onehot, pad_start, 0), axis=-1)
    blk_e = jnp.minimum(jnp.sum(jnp.arange(n_blocks, dtype=jnp.int32)[:, None] * MOE_ROWS >= pad_end[None, :],
                                axis=1), N_EXPERTS - 1).astype(jnp.int32)
    n_used = (pad_end[-1] // MOE_ROWS).astype(jnp.int32).reshape(1)
    run_end = jnp.sum(jnp.where(blk_e[:, None] == jnp.arange(N_EXPERTS, dtype=jnp.int32), pad_end // MOE_ROWS, 0),
                      axis=1)
    blk_at_run_end = jnp.sum(jnp.where(run_end[:, None] == jnp.arange(n_blocks, dtype=jnp.int32), blk_e[None, :], 0),
                             axis=1)
    nxt_e = jnp.where(run_end < n_used[0], blk_at_run_end, -1).astype(jnp.int32)
    grid_spec = pltpu.PrefetchScalarGridSpec(
        num_scalar_prefetch=2,
        grid=(n // tb,),
        in_specs=[pl.BlockSpec((TOP_K, tb), lambda j, lo, hi: (0, j), memory_space=pltpu.SMEM)],
        out_specs=pl.BlockSpec(memory_space=pltpu.SMEM))
    row_tok = pl.pallas_call(
        functools.partial(_row_tok_kernel, tb=tb),
        grid_spec=grid_spec,
        out_shape=jax.ShapeDtypeStruct((n_blocks * MOE_ROWS,), jnp.int32),
        compiler_params=_cparams("arbitrary"),
    )(jnp.concatenate([pad_start + counts, pad_end[-1:]]).astype(jnp.int32),
      jnp.concatenate([pad_end, jnp.full((1,), n_blocks * MOE_ROWS)]).astype(jnp.int32), pos8)
    return row_tok, pos8.reshape(-1), blk_e, nxt_e, n_used


def _expert_kernel(blk_e_ref, nxt_e_ref, nused_ref, tok_ref, h_ref, wg_hbm, wu_hbm, wd_hbm, y_ref,
                   xg, stage_g, stage_u, stage_d, sem, wgb, wub, wdb, *, layer):
    bm = MOE_ROWS
    i = pl.program_id(0)
    n_used = nused_ref[0]
    nt = xg.shape[1] // bm
    d = nt * 256
    slot = i % 2

    def weight_copies(e):
        return (pltpu.make_async_copy(wg_hbm.at[layer, e], stage_g, sem.at[0]),
                pltpu.make_async_copy(wu_hbm.at[layer, e], stage_u, sem.at[1]),
                pltpu.make_async_copy(wd_hbm.at[layer, e], stage_d, sem.at[2]))

    def gather_row(blk, slot_, r):
        tok = tok_ref[blk * bm + r]
        xg[slot_, pl.ds(r * nt, nt), :] = h_ref[pl.ds(pl.multiple_of(tok * nt, nt), nt), :]

    @pl.when(i < n_used)
    def _():
        e = blk_e_ref[i]

        @pl.when(i == 0)
        def _():
            for cp in weight_copies(e):
                cp.start()

            def first(r, carry):
                gather_row(0, 0, r)
                return carry
            lax.fori_loop(0, bm, first, 0, unroll=8)

        new_expert = jnp.logical_or(i == 0, e != blk_e_ref[jnp.maximum(i - 1, 0)])

        @pl.when(new_expert)
        def _():
            for cp in weight_copies(e):
                cp.wait()
            wgb[...] = stage_g[...].astype(BF16)
            wub[...] = stage_u[...].astype(BF16)
            wdb[...] = stage_d[...].astype(BF16)
            nxt = nxt_e_ref[i]

            @pl.when(nxt >= 0)
            def _():
                for cp in weight_copies(nxt):
                    cp.start()

        x = _load_token_rows(xg.at[slot], bm, d).astype(BF16)
        g = jnp.dot(x, wgb[...], preferred_element_type=F32)
        u = jnp.dot(x, wub[...], preferred_element_type=F32)
        _store_token_rows(y_ref, jnp.dot((_silu(g) * u).astype(BF16), wdb[...], preferred_element_type=F32))
        nxt_blk = jnp.minimum(i + 1, n_used - 1)
        for r in range(bm):
            gather_row(nxt_blk, 1 - slot, r)

    @pl.when(i >= n_used)
    def _():
        y_ref[...] = jnp.zeros_like(y_ref)


def _experts(h2c, row_tok, blk_e, nxt_e, n_used, w_gate, w_up, w_down, layer, n_blocks):
    d, de = w_gate.shape[-2:]
    nt = d // 256
    bm = MOE_ROWS
    hbm = pl.BlockSpec(memory_space=pl.ANY)
    grid_spec = pltpu.PrefetchScalarGridSpec(
        num_scalar_prefetch=4,
        grid=(n_blocks,),
        in_specs=[pl.BlockSpec(h2c.shape, lambda i, *_: (0, 0), pipeline_mode=pl.Buffered(1)), hbm, hbm, hbm],
        out_specs=pl.BlockSpec((bm * nt, 128), lambda i, *_: (i, 0)),
        scratch_shapes=[pltpu.VMEM((2, bm * nt, 128), U32),
                        pltpu.VMEM((d, de), F32), pltpu.VMEM((d, de), F32), pltpu.VMEM((de, d), F32),
                        pltpu.SemaphoreType.DMA((3,)),
                        pltpu.VMEM((d, de), BF16), pltpu.VMEM((d, de), BF16), pltpu.VMEM((de, d), BF16)])
    return pl.pallas_call(
        functools.partial(_expert_kernel, layer=layer),
        grid_spec=grid_spec,
        out_shape=jax.ShapeDtypeStruct((n_blocks * bm * nt, 128), U32),
        compiler_params=pltpu.CompilerParams(dimension_semantics=("arbitrary",),
                                             vmem_limit_bytes=EXPERT_VMEM_LIMIT),
    )(blk_e, nxt_e, n_used, row_tok, h2c, w_gate, w_up, w_down)


def _combine_kernel(pos_ref, ys_hbm, tw_ref, h2_ref, x_ref, mod_ref, wsg_ref, wsu_ref, wsd_ref, fg_ref,
                    o_ref, buf, sem, *, tmc, final):
    i = pl.program_id(0)
    n_tiles = pl.num_programs(0)
    n_tok = n_tiles * tmc
    slot = i % 2
    nt = buf.shape[2] // tmc
    d = nt * 256

    def row_copy(p, slot_, k, t):
        return pltpu.make_async_copy(ys_hbm.at[pl.ds(pl.multiple_of(p, nt), nt), :],
                                     buf.at[slot_, k, pl.ds(pl.multiple_of(t * nt, nt), nt), :], sem.at[slot_])

    def issue(tile, slot_):
        def body(t, carry):
            for k in range(TOP_K):
                row_copy(pos_ref[k * n_tok + tile * tmc + t], slot_, k, t).start(priority=k % 2)
            return carry
        lax.fori_loop(0, tmc, body, 0, unroll=8)

    @pl.when(i == 0)
    def _():
        issue(0, 0)

    @pl.when(i + 1 < n_tiles)
    def _():
        issue(i + 1, 1 - slot)

    h = _load_token_rows(h2_ref, tmc, d).astype(BF16)
    sg = jnp.dot(h, wsg_ref[...], preferred_element_type=F32)
    su = jnp.dot(h, wsu_ref[...], preferred_element_type=F32)
    ff = jnp.dot((_silu(sg) * su).astype(BF16), wsd_ref[...], preferred_element_type=F32)

    def wait_body(t, carry):
        for k in range(TOP_K):
            row_copy(0, slot, k, t).wait()
        return carry
    lax.fori_loop(0, tmc, wait_body, 0, unroll=4)

    tw = tw_ref[...]
    routed = tw[:, 0:1] * _load_token_rows(buf.at[slot, 0], tmc, d)
    for k in range(1, TOP_K):
        routed = routed + tw[:, k:k + 1] * _load_token_rows(buf.at[slot, k], tmc, d)
    xn = x_ref[...] + mod_ref[5:6, :] * (routed + ff)
    if final:
        xn = _rms(xn) * fg_ref[...]
    o_ref[...] = xn


def _combine(ys, pos, top_w, h2c, xs_flat, mods, wsg, wsu, wsd, final_g, n_ctx, t_per_batch, final):
    d, de = wsg.shape
    nt = d // 256
    n = xs_flat.shape[0]
    tmc = COMBINE_TOKENS
    tiles = n // tmc
    bsz = n // t_per_batch
    tiles_per_batch = t_per_batch // tmc
    ctx_tiles = n_ctx // tmc
    const = lambda i, p: (0, 0)
    row = lambda i, p: (i, 0)

    def mod_map(i, p):
        return (jnp.where(i % tiles_per_batch < ctx_tiles, bsz, i // tiles_per_batch), 0, 0)

    grid_spec = pltpu.PrefetchScalarGridSpec(
        num_scalar_prefetch=1,
        grid=(tiles,),
        in_specs=[pl.BlockSpec(memory_space=pl.ANY),
                  pl.BlockSpec((tmc, TOP_K), row),
                  pl.BlockSpec((tmc * nt, 128), row),
                  pl.BlockSpec((tmc, d), row),
                  pl.BlockSpec((None, N_MOD, d), mod_map),
                  pl.BlockSpec((d, de), const), pl.BlockSpec((d, de), const), pl.BlockSpec((de, d), const),
                  pl.BlockSpec((1, d), const)],
        out_specs=pl.BlockSpec((tmc, d), row),
        scratch_shapes=[pltpu.VMEM((2, TOP_K, tmc * nt, 128), U32), pltpu.SemaphoreType.DMA((2,))])
    return pl.pallas_call(
        functools.partial(_combine_kernel, tmc=tmc, final=final),
        grid_spec=grid_spec,
        out_shape=jax.ShapeDtypeStruct(xs_flat.shape, F32),
        input_output_aliases={4: 0},
        compiler_params=_cparams("arbitrary"),
    )(pos, ys, top_w, h2c, xs_flat, mods, wsg, wsu, wsd, final_g.reshape(1, d))


def _pick_tile(n, candidates):
    for c in candidates:
        if n % c == 0:
            return c
    raise ValueError(f"no tile for {n}")


def kernel(x, c, ctx, c_ctx, w_mod, b_mod, norm1_g, norm2_g, final_g, w_in, s5_lam_re, s5_lam_im, s5_log_dt, s5_b_re, s5_b_im, s5_c_re, s5_c_im, s5_d, s5_w_glu, ret_log_decay, ret_norm_g, w_br_s5, w_br_ret, w_out, moe_router, moe_router_bias, moe_w_gate, moe_w_up, moe_w_down, sh_w_gate, sh_w_up, sh_w_down):
    bsz, n_lat, d = x.shape
    n_ctx = ctx.shape[1]
    depth = w_mod.shape[0]
    t = n_ctx + n_lat
    n_tok = bsz * t
    g_n = s5_b_re.shape[1]
    assert n_ctx % RET_CHUNK == 0 and n_lat % RET_CHUNK == 0 and bsz + 1 <= 8
    assert d % 1024 == 0 and n_tok % COMBINE_TOKENS == 0 and n_ctx % COMBINE_TOKENS == 0

    cin = jnp.concatenate([c, c_ctx[None], jnp.zeros((8 - bsz - 1, d), F32)], axis=0)
    mods_all = _modulation(cin, w_mod, b_mod).reshape(depth, 8, N_MOD, d)
    cos, sin = _rope_tables(n_ctx, n_lat)
    s5_w1, s5_wo, s5_ar, s5_ai = jax.vmap(_s5_tables)(s5_lam_re, s5_lam_im, s5_log_dt, s5_b_re, s5_b_im,
                                                      s5_c_re, s5_c_im, s5_d)
    xs = jnp.concatenate([ctx, x], axis=1)

    tm_in = _pick_tile(t, (1088, 544, 272, 256, 128))
    tm_merge = _pick_tile(n_ctx, (256, 128))
    gb = 128 // S5_GROUP

    for l in range(depth):
        mods = mods_all[l]
        proj = _in_proj(xs, norm1_g[l], mods, w_in[l].astype(BF16), n_ctx, tm_in)

        y_s5 = _s5_mix(proj, s5_w1, s5_wo, s5_ar, s5_ai, l, n_ctx, gb)

        lg_tab = jnp.broadcast_to(jnp.pad(ret_log_decay[l].T, ((0, 0), (0, 6)))[:, :, None],
                                  (RET_HEADS, 8, RET_HEAD_DIM))
        o_ret = _retention(proj, cos, sin, lg_tab, n_ctx)

        last = l == depth - 1
        t_l, ctx_l = (n_lat, 0) if last else (t, n_ctx)
        n_l = bsz * t_l
        xs, h2c, logits_t = _merge(y_s5, o_ret, proj, xs, mods, ret_norm_g[l], norm2_g[l],
                                   s5_w_glu[l].astype(BF16), w_br_s5[l].astype(BF16), w_br_ret[l].astype(BF16),
                                   w_out[l].astype(BF16), moe_router[l], n_ctx, tm_merge, latent_only=last)

        n_blocks = -(-(n_l * TOP_K + N_EXPERTS * (MOE_ROWS - 1)) // MOE_ROWS)
        tw8, ei8, rk8, cnt = _route(logits_t, moe_router_bias[l], _pick_tile(n_l, (256, 128)), 0, n_l)
        row_tok, pos, blk_e, nxt_e, n_used = _dispatch_plan(ei8, rk8, cnt, n_blocks,
                                                            _pick_tile(n_l, (2176, 2048, 1024, 512, 256, 128)))
        ys = _experts(h2c, row_tok, blk_e, nxt_e, n_used, moe_w_gate, moe_w_up, moe_w_down, l, n_blocks)
        xs = _combine(ys, pos * (d // 256), tw8.T, h2c, xs.reshape(n_l, d), mods, sh_w_gate[l].astype(BF16),
                      sh_w_up[l].astype(BF16), sh_w_down[l].astype(BF16), final_g, ctx_l, t_l,
                      final=last).reshape(bsz, t_l, d)

    return xs
```
